```python
import math
import jax, jax.numpy as jnp
from jax import lax
import numpy as np

D_MODEL = 2048
BATCH = 4
SEQ = 2048
DEPTH = 2

RET_HEADS = 4
RET_HEAD_DIM = D_MODEL // 8
RET_V_DIM = D_MODEL // 8
RET_W = RET_HEADS * RET_HEAD_DIM
RET_VW = RET_HEADS * RET_V_DIM
RET_CHUNK = 128
ROPE_BASE = 10000.0
LRU_W = D_MODEL // 2
LRU_BLOCKS = 8
LRU_BW = LRU_W // LRU_BLOCKS
LRU_CONV = 4
LRU_C = 8.0
MOBA_HEADS = 16
MOBA_HEAD_DIM = D_MODEL // MOBA_HEADS
MOBA_W = MOBA_HEADS * MOBA_HEAD_DIM
MOBA_BLOCK = 256
MOBA_TOPK = 3
MOBA_QCHUNK = 32
REL_BUCKETS = 32
REL_MAX_DIST = 128
D_FF = 11 * D_MODEL // 4
FFN_CONV = 3
EPS = 1e-6

EV_IN = 3 * RET_W + RET_VW + 2 * LRU_W
EV_CAT = RET_VW + LRU_W
OD_IN = 3 * MOBA_W

kernel_name = "hybrid_retention_rglru_moba_convffn"


def rmsnorm(x, g):
    x32 = x.astype(jnp.float32)
    y = x32 * lax.rsqrt(jnp.mean(x32 * x32, axis=-1, keepdims=True) + EPS)
    return (y * g.astype(jnp.float32)).astype(x.dtype)


def causal_dwconv(x, w, b):
    width, c = w.shape
    y = lax.conv_general_dilated(x, w.astype(x.dtype)[:, None, :], window_strides=(1,),
                                 padding=[(width - 1, 0)],
                                 dimension_numbers=('NWC', 'WIO', 'NWC'),
                                 feature_group_count=c)
    return y + b.astype(x.dtype)


def rotary(x, pos):
    half = x.shape[-1] // 2
    freqs = ROPE_BASE ** (-jnp.arange(half, dtype=jnp.float32) / half)
    ang = pos.astype(jnp.float32)[:, None] * freqs[None, :]
    cos = jnp.cos(ang)[None, :, None, :]
    sin = jnp.sin(ang)[None, :, None, :]
    x32 = x.astype(jnp.float32)
    x1, x2 = x32[..., :half], x32[..., half:]
    return jnp.concatenate([x1 * cos - x2 * sin, x1 * sin + x2 * cos], axis=-1).astype(x.dtype)


def retention(q, k, v):
    b_, s_, h_, dk = q.shape
    dv = v.shape[-1]
    n = s_ // RET_CHUNK
    pos = jnp.arange(s_)
    q = rotary(q, pos)
    k = rotary(k, pos) * (dk ** -0.5)
    log_g = jnp.log1p(-jnp.exp2(-5.0 - jnp.arange(h_, dtype=jnp.float32)))
    i = jnp.arange(RET_CHUNK)
    diff = i[:, None] - i[None, :]
    decay = jnp.where(diff >= 0, jnp.exp(log_g[:, None, None] * jnp.maximum(diff, 0)), 0.0)
    zeta = jnp.exp(log_g[None, :] * (RET_CHUNK - 1 - i)[:, None])
    xi = jnp.exp(log_g[None, :] * (i + 1)[:, None])
    g_chunk = jnp.exp(log_g * RET_CHUNK)
    qc = q.reshape(b_, n, RET_CHUNK, h_, dk)
    kc = k.reshape(b_, n, RET_CHUNK, h_, dk)
    vc = v.reshape(b_, n, RET_CHUNK, h_, dv)
    scores = jnp.einsum('bnihd,bnjhd->bnhij', qc, kc) * decay
    inner = jnp.einsum('bnhij,bnjhe->bnihe', scores, vc)
    kv = jnp.einsum('bnjhd,bnjhe->nbhde', kc * zeta[:, :, None], vc)

    def step(state, kv_n):
        return g_chunk[None, :, None, None] * state + kv_n, state

    _, state_prev = lax.scan(step, jnp.zeros((b_, h_, dk, dv), kv.dtype), kv)
    cross = jnp.einsum('bnihd,nbhde->bnihe', qc, state_prev) * xi[:, :, None]
    return (inner + cross).reshape(b_, s_, h_, dv).astype(v.dtype)


def rglru_branch(xl, gl, conv_w, conv_b, w_a, b_a, w_i, b_i, lam):
    b_, s_, w_ = xl.shape
    xc = causal_dwconv(xl, conv_w, conv_b)
    xb = xc.reshape(b_, s_, LRU_BLOCKS, LRU_BW)
    r = jax.nn.sigmoid((jnp.einsum('bsgi,gij->bsgj', xb, w_a).reshape(b_, s_, w_) + b_a).astype(jnp.float32))
    gi = jax.nn.sigmoid((jnp.einsum('bsgi,gij->bsgj', xb, w_i).reshape(b_, s_, w_) + b_i).astype(jnp.float32))
    log_a = -LRU_C * r * jax.nn.softplus(-lam.astype(jnp.float32))
    a = jnp.exp(log_a)
    u = jnp.sqrt(-jnp.expm1(2.0 * log_a)) * (gi * xc.astype(jnp.float32))

    def combine(left, right):
        a1, b1 = left
        a2, b2 = right
        return a1 * a2, a2 * b1 + b2

    _, h = lax.associative_scan(combine, (a, u), axis=1)
    return h.astype(xl.dtype) * jax.nn.gelu(gl)


def even_mixer(h, w_in, ret_norm, conv_w, conv_b, w_a, b_a, w_i, b_i, lam, w_out):
    b_, s_, _ = h.shape
    proj = h @ w_in
    cuts = np.cumsum([RET_W, RET_W, RET_VW, RET_VW, LRU_W])
    q, k, v, g, xl, gl = jnp.split(proj, cuts, axis=-1)
    q = q.reshape(b_, s_, RET_HEADS, RET_HEAD_DIM)
    k = k.reshape(b_, s_, RET_HEADS, RET_HEAD_DIM)
    v = v.reshape(b_, s_, RET_HEADS, RET_V_DIM)
    ret = retention(q, k, v)
    ret = rmsnorm(ret, ret_norm.reshape(RET_HEADS, RET_V_DIM)).reshape(b_, s_, RET_VW)
    ret = ret * jax.nn.silu(g)
    lru = rglru_branch(xl, gl, conv_w, conv_b, w_a, b_a, w_i, b_i, lam)
    return jnp.concatenate([ret, lru], axis=-1) @ w_out


def t5_bucket(dist):
    n = jnp.maximum(dist, 0)
    max_exact = REL_BUCKETS // 2
    nf = jnp.maximum(n, 1).astype(jnp.float32)
    large = max_exact + (jnp.log(nf / max_exact) / math.log(REL_MAX_DIST / max_exact)
                         * (REL_BUCKETS - max_exact)).astype(jnp.int32)
    large = jnp.minimum(large, REL_BUCKETS - 1)
    return jnp.where(n < max_exact, n, large)


def moba_attention(q, k, v, rel_bias):
    b_, s_, h_, d_ = q.shape
    nb = -(-s_ // MOBA_BLOCK)
    pad = nb * MOBA_BLOCK - s_
    kp = jnp.pad(k, ((0, 0), (0, pad), (0, 0), (0, 0)))
    vp = jnp.pad(v, ((0, 0), (0, pad), (0, 0), (0, 0)))
    kb = kp.reshape(b_, nb, MOBA_BLOCK, h_, d_).transpose(0, 3, 1, 2, 4)
    vb = vp.reshape(b_, nb, MOBA_BLOCK, h_, d_).transpose(0, 3, 1, 2, 4)
    kmean = jnp.mean(kb.astype(jnp.float32), axis=3)
    qh = q.transpose(0, 2, 1, 3)
    gate = jnp.einsum('bhsd,bhnd->bhsn', qh.astype(jnp.float32), kmean)
    own = jnp.arange(s_) // MOBA_BLOCK
    fully_past = jnp.arange(nb)[None, :] < own[:, None]
    gate = jnp.where(fully_past, gate, -jnp.inf)
    k_eff = min(MOBA_TOPK, nb)
    _, sel = lax.top_k(gate, k_eff)
    sel = sel.astype(jnp.int32)
    nc = s_ // MOBA_QCHUNK
    q_chunks = qh.reshape(b_, h_, nc, MOBA_QCHUNK, d_).transpose(0, 2, 1, 3, 4)
    sel_chunks = sel.reshape(b_, h_, nc, MOBA_QCHUNK, k_eff).transpose(0, 2, 1, 3, 4)
    starts = jnp.arange(nc, dtype=jnp.int32) * MOBA_QCHUNK
    rel_t = rel_bias.T.astype(jnp.float32)
    gather = jax.vmap(lambda blocks, idx: blocks[idx])

    def per_batch(bargs):
        q_b, sel_b, kb_b, vb_b = bargs

        def per_chunk(cargs):
            q_c, sel_c, c0 = cargs
            k_sel = gather(kb_b, sel_c)
            v_sel = gather(vb_b, sel_c)
            qpos = c0 + jnp.arange(MOBA_QCHUNK)
            kpos_sel = sel_c[..., None] * MOBA_BLOCK + jnp.arange(MOBA_BLOCK)
            bucket_sel = t5_bucket(qpos[None, :, None, None] - kpos_sel)
            bias_sel = jnp.take_along_axis(rel_t, bucket_sel.reshape(h_, -1), axis=1).reshape(kpos_sel.shape)
            valid = (jnp.arange(k_eff)[None, :] < (qpos // MOBA_BLOCK)[:, None])[None, :, :, None]
            l_sel = jnp.einsum('hqd,hqkjd->hqkj', q_c, k_sel).astype(jnp.float32) + bias_sel
            l_sel = jnp.where(valid, l_sel, -jnp.inf).reshape(h_, MOBA_QCHUNK, k_eff * MOBA_BLOCK)
            blk = c0 // MOBA_BLOCK
            k_own = lax.dynamic_index_in_dim(kb_b, blk, axis=1, keepdims=False)
            v_own = lax.dynamic_index_in_dim(vb_b, blk, axis=1, keepdims=False)
            kpos_own = blk * MOBA_BLOCK + jnp.arange(MOBA_BLOCK)
            dist_own = qpos[:, None] - kpos_own[None, :]
            l_own = jnp.einsum('hqd,hjd->hqj', q_c, k_own).astype(jnp.float32) + rel_t[:, t5_bucket(dist_own)]
            l_own = jnp.where((dist_own >= 0)[None], l_own, -jnp.inf)
            p = jax.nn.softmax(jnp.concatenate([l_sel, l_own], axis=-1), axis=-1)
            p_sel = p[..., :k_eff * MOBA_BLOCK].astype(v_sel.dtype)
            p_own = p[..., k_eff * MOBA_BLOCK:].astype(v_own.dtype)
            out = jnp.einsum('hqj,hqjd->hqd', p_sel, v_sel.reshape(h_, MOBA_QCHUNK, k_eff * MOBA_BLOCK, d_))
            return out + jnp.einsum('hqj,hjd->hqd', p_own, v_own)

        return lax.map(per_chunk, (q_b, sel_b, starts))

    out = lax.map(per_batch, (q_chunks, sel_chunks, kb, vb))
    return out.transpose(0, 1, 3, 2, 4).reshape(b_, s_, h_, d_)


def odd_mixer(h, w_in, q_norm, k_norm, rel_bias, w_out):
    b_, s_, _ = h.shape
    q, k, v = jnp.split(h @ w_in, 3, axis=-1)
    q = rmsnorm(q.reshape(b_, s_, MOBA_HEADS, MOBA_HEAD_DIM), q_norm) * (MOBA_HEAD_DIM ** -0.5)
    k = rmsnorm(k.reshape(b_, s_, MOBA_HEADS, MOBA_HEAD_DIM), k_norm)
    v = v.reshape(b_, s_, MOBA_HEADS, MOBA_HEAD_DIM)
    o = moba_attention(q, k, v, rel_bias)
    return o.reshape(b_, s_, MOBA_W) @ w_out


def conv_ffn(h, w_up, conv_w, conv_b, w_down):
    u = causal_dwconv(h @ w_up, conv_w, conv_b)
    gate, val = jnp.split(u, 2, axis=-1)
    return (jax.nn.gelu(gate) * val) @ w_down


def setup_inputs(seed: int = 0) -> dict:
    key = jax.random.key(seed)
    ks = jax.random.split(key, 24)
    ne = (DEPTH + 1) // 2
    no = DEPTH // 2
    f32 = jnp.float32

    def nrm(k, shape, fan):
        return jax.random.normal(k, shape, f32) * (fan ** -0.5)

    def gain(k, shape):
        return 1.0 + 0.02 * jax.random.normal(k, shape, f32)

    def small(k, shape):
        return 0.01 * jax.random.normal(k, shape, f32)

    u = jax.random.uniform(ks[11], (ne, LRU_W), f32, minval=0.9, maxval=0.999)
    p = u ** (1.0 / LRU_C)
    lam = jnp.log(p) - jnp.log1p(-p)
    return {
        "x": jax.random.normal(ks[0], (BATCH, SEQ, D_MODEL), f32),
        "norm_mix": gain(ks[1], (DEPTH, D_MODEL)),
        "norm_ffn": gain(ks[2], (DEPTH, D_MODEL)),
        "ev_w_in": nrm(ks[3], (ne, D_MODEL, EV_IN), D_MODEL),
        "ret_norm": gain(ks[4], (ne, RET_VW)),
        "lru_conv_w": nrm(ks[5], (ne, LRU_CONV, LRU_W), LRU_CONV),
        "lru_conv_b": small(ks[6], (ne, LRU_W)),
        "lru_w_a": nrm(ks[7], (ne, LRU_BLOCKS, LRU_BW, LRU_BW), LRU_BW),
        "lru_b_a": small(ks[8], (ne, LRU_W)),
        "lru_w_i": nrm(ks[9], (ne, LRU_BLOCKS, LRU_BW, LRU_BW), LRU_BW),
        "lru_b_i": small(ks[10], (ne, LRU_W)),
        "lru_lambda": lam,
        "ev_w_out": nrm(ks[12], (ne, EV_CAT, D_MODEL), EV_CAT),
        "od_w_in": nrm(ks[13], (no, D_MODEL, OD_IN), D_MODEL),
        "q_norm": gain(ks[14], (no, MOBA_HEAD_DIM)),
        "k_norm": gain(ks[15], (no, MOBA_HEAD_DIM)),
        "od_w_out": nrm(ks[16], (no, MOBA_W, D_MODEL), MOBA_W),
        "rel_bias": 0.1 * jax.random.normal(ks[17], (REL_BUCKETS, MOBA_HEADS), f32),
        "ffn_w_up": nrm(ks[18], (DEPTH, D_MODEL, 2 * D_FF), D_MODEL),
        "ffn_conv_w": nrm(ks[19], (DEPTH, FFN_CONV, 2 * D_FF), FFN_CONV),
        "ffn_conv_b": small(ks[20], (DEPTH, 2 * D_FF)),
        "ffn_w_down": nrm(ks[21], (DEPTH, D_FF, D_MODEL), D_FF),
    }


def reference(x, norm_mix, norm_ffn, ev_w_in, ret_norm, lru_conv_w, lru_conv_b, lru_w_a, lru_b_a,
              lru_w_i, lru_b_i, lru_lambda, ev_w_out, od_w_in, q_norm, k_norm, od_w_out, rel_bias,
              ffn_w_up, ffn_conv_w, ffn_conv_b, ffn_w_down):
    for l in range(DEPTH):
        hn = rmsnorm(x, norm_mix[l])
        if l % 2 == 0:
            e = l // 2
            y = even_mixer(hn, ev_w_in[e], ret_norm[e], lru_conv_w[e], lru_conv_b[e], lru_w_a[e],
                           lru_b_a[e], lru_w_i[e], lru_b_i[e], lru_lambda[e], ev_w_out[e])
        else:
            o = l // 2
            y = odd_mixer(hn, od_w_in[o], q_norm[o], k_norm[o], rel_bias, od_w_out[o])
        x = x + y
        x = x + conv_ffn(rmsnorm(x, norm_ffn[l]), ffn_w_up[l], ffn_conv_w[l], ffn_conv_b[l], ffn_w_down[l])
    return x
```

```python
import functools
import math

import jax
import jax.numpy as jnp
from jax import lax
from jax.experimental import pallas as pl
from jax.experimental.pallas import tpu as pltpu

D_MODEL = 2048
SEQ = 2048
RET_HEADS = 4
RET_DK = 256
RET_DV = 256
RET_W = RET_HEADS * RET_DK
RET_CHUNK = 128
ROPE_BASE = 10000.0
LRU_W = 1024
LRU_BLOCKS = 8
LRU_BW = 128
LRU_CONV = 4
LRU_C = 8.0
LRU_SCAN_CHUNK = 64
MOBA_HEADS = 16
MOBA_DH = 128
MOBA_BLOCK = 256
MOBA_TOPK = 3
REL_BUCKETS = 32
REL_MAX_DIST = 128
D_FF = 5632
FFN_CONV = 3
EPS = 1e-6

VMEM_LIMIT_BYTES = 56 * 1024 * 1024
SUBLANES = 8

BF16 = jnp.bfloat16
F32 = jnp.float32
NEG_INF = float("-inf")


def _params(*sem):
    return pltpu.CompilerParams(dimension_semantics=sem, vmem_limit_bytes=VMEM_LIMIT_BYTES)


def _gelu_tanh(x):
    c = math.sqrt(2.0 / math.pi)
    return 0.5 * x * (1.0 + jnp.tanh(c * (x + 0.044715 * (x * x * x))))


def _rms_scale(x):
    return lax.rsqrt(jnp.mean(x * x, axis=-1, keepdims=True) + EPS)


def _norm_mm_kernel(x_ref, g_ref, w_ref, o_ref, hn_ref):
    @pl.when(pl.program_id(1) == 0)
    def _():
        x = x_ref[...]
        hn_ref[...] = (x * _rms_scale(x) * g_ref[...]).astype(BF16)

    o_ref[...] = jnp.dot(hn_ref[...], w_ref[...], preferred_element_type=F32).astype(o_ref.dtype)


def norm_matmul(x, g, w, *, tm, tn, out_dtype):
    t, d = x.shape
    n = w.shape[1]
    return pl.pallas_call(
        _norm_mm_kernel,
        grid=(t // tm, n // tn),
        in_specs=[
            pl.BlockSpec((tm, d), lambda i, j: (i, 0)),
            pl.BlockSpec((1, d), lambda i, j: (0, 0)),
            pl.BlockSpec((d, tn), lambda i, j: (0, j)),
        ],
        out_specs=pl.BlockSpec((tm, tn), lambda i, j: (i, j)),
        out_shape=jax.ShapeDtypeStruct((t, n), out_dtype),
        scratch_shapes=[pltpu.VMEM((tm, d), BF16)],
        compiler_params=_params("arbitrary", "arbitrary"),
        name="norm_matmul",
    )(x, g.reshape(1, d), w)


def _retention_kernel(q_ref, k_ref, v_ref, g_ref, cos_ref, sin_ref, decay_ref, zeta_ref, xi_ref,
                      gch_ref, rn_ref, o_ref, state_ref):
    state_ref[...] = jnp.zeros_like(state_ref)
    half = RET_DK // 2
    n_chunks = q_ref.shape[0] // RET_CHUNK

    def rotary(x, cos, sin):
        x1, x2 = x[:, :half], x[:, half:]
        return jnp.concatenate([x1 * cos - x2 * sin, x1 * sin + x2 * cos], axis=-1)

    def chunk(n, carry):
        rows = pl.ds(pl.multiple_of(n * RET_CHUNK, RET_CHUNK), RET_CHUNK)
        cos, sin = cos_ref[rows, :], sin_ref[rows, :]
        qr = rotary(q_ref[rows, :].astype(F32), cos, sin)
        kr = rotary(k_ref[rows, :].astype(F32), cos, sin) * (RET_DK ** -0.5)
        vb = v_ref[rows, :]
        qb = qr.astype(BF16)
        scores = lax.dot_general(qb, kr.astype(BF16), (((1,), (1,)), ((), ())),
                                 preferred_element_type=F32) * decay_ref[0]
        inner = jnp.dot(scores.astype(BF16), vb, preferred_element_type=F32)
        state = state_ref[...]
        cross = jnp.dot(qb, state.astype(BF16), preferred_element_type=F32) * xi_ref[0]
        kz = (kr * zeta_ref[0]).astype(BF16)
        kv = lax.dot_general(kz, vb, (((0,), (0,)), ((), ())), preferred_element_type=F32)
        state_ref[...] = gch_ref[0] * state + kv
        out = inner + cross
        out = out * _rms_scale(out) * rn_ref[0]
        g = g_ref[rows, :].astype(F32)
        o_ref[rows, :] = (out * (g * jax.nn.sigmoid(g))).astype(o_ref.dtype)
        return carry

    lax.fori_loop(0, n_chunks, chunk, 0)


def retention(proj, tables, ret_norm, batch):
    cos, sin, decay, zeta, xi, gch = tables
    t = proj.shape[0]
    hq = RET_W // RET_DK
    col = lambda off: pl.BlockSpec((SEQ, RET_DK), lambda b, h: (b, off + h))
    per_head = lambda shape: pl.BlockSpec((1,) + shape, lambda b, h: (h, 0, 0))
    full = pl.BlockSpec((SEQ, RET_DK // 2), lambda b, h: (0, 0))
    return pl.pallas_call(
        _retention_kernel,
        grid=(batch, RET_HEADS),
        in_specs=[col(0), col(hq), col(2 * hq), col(3 * hq), full, full,
                  per_head((RET_CHUNK, RET_CHUNK)), per_head((RET_CHUNK, RET_DK)),
                  per_head((RET_CHUNK, RET_DV)), per_head((1, RET_DV)), per_head((1, RET_DV))],
        out_specs=pl.BlockSpec((SEQ, RET_DV), lambda b, h: (b, h)),
        out_shape=jax.ShapeDtypeStruct((t, RET_HEADS * RET_DV), BF16),
        scratch_shapes=[pltpu.VMEM((RET_DK, RET_DV), F32)],
        compiler_params=_params("arbitrary", "arbitrary"),
        name="retention",
    )(proj, proj, proj, proj, cos, sin, decay, zeta, xi, gch,
      ret_norm.reshape(RET_HEADS, 1, RET_DV))


def retention_tables():
    half = RET_DK // 2
    pos = jnp.arange(SEQ)
    freqs = ROPE_BASE ** (-jnp.arange(half, dtype=F32) / half)
    ang = pos.astype(F32)[:, None] * freqs[None, :]
    cos, sin = jnp.cos(ang), jnp.sin(ang)
    log_g = jnp.log1p(-jnp.exp2(-5.0 - jnp.arange(RET_HEADS, dtype=F32)))
    i = jnp.arange(RET_CHUNK)
    diff = i[:, None] - i[None, :]
    decay = jnp.where(diff >= 0, jnp.exp(log_g[:, None, None] * jnp.maximum(diff, 0)), 0.0)
    zeta = jnp.exp(log_g[None, :] * (RET_CHUNK - 1 - i)[:, None])
    xi = jnp.exp(log_g[None, :] * (i + 1)[:, None])
    g_chunk = jnp.exp(log_g * RET_CHUNK)
    zeta_b = jnp.broadcast_to(zeta.T[:, :, None], (RET_HEADS, RET_CHUNK, RET_DK))
    xi_b = jnp.broadcast_to(xi.T[:, :, None], (RET_HEADS, RET_CHUNK, RET_DV))
    gch_b = jnp.broadcast_to(g_chunk[:, None, None], (RET_HEADS, 1, RET_DV))
    return cos, sin, decay, zeta_b, xi_b, gch_b


def _shift_rows(v, d, fill, row):
    if d % SUBLANES == 0:
        return jnp.concatenate([jnp.full((d, v.shape[1]), fill, v.dtype), v[:-d]], axis=0)
    return jnp.where(row >= d, pltpu.roll(v, d, axis=0), fill)


def _rglru_kernel(xl_ref, gl_ref, cw_ref, cb_ref, wai_ref, bai_ref, lam_ref, o_ref, a_ref, u_ref):
    s = xl_ref.shape[0]
    x = xl_ref[...].astype(F32)
    row = lax.broadcasted_iota(jnp.int32, x.shape, 0)
    xc = x * cw_ref[LRU_CONV - 1:LRU_CONV, :] + cb_ref[...]
    for d in range(1, LRU_CONV):
        xs = jnp.where(row >= d, pltpu.roll(x, d, axis=0), 0.0)
        xc = xc + xs * cw_ref[LRU_CONV - 1 - d:LRU_CONV - d, :]
    y = jnp.dot(xc.astype(BF16), wai_ref[0], preferred_element_type=F32) + bai_ref[...]
    r = jax.nn.sigmoid(y[:, :LRU_BW])
    gi = jax.nn.sigmoid(y[:, LRU_BW:])
    neg_lam = -lam_ref[...]
    softplus = jnp.maximum(neg_lam, 0.0) + jnp.log1p(jnp.exp(-jnp.abs(neg_lam)))
    log_a = (-LRU_C) * r * softplus
    a = jnp.exp(log_a)
    a_ref[...] = a
    u_ref[...] = jnp.sqrt(1.0 - a * a) * (gi * xc)

    c = LRU_SCAN_CHUNK
    crow = lax.broadcasted_iota(jnp.int32, (c, LRU_BW), 0)

    def chunk(n, h_prev):
        rows = pl.ds(pl.multiple_of(n * c, c), c)
        av, bv = a_ref[rows, :], u_ref[rows, :]
        d = 1
        while d < c:
            a_sh = _shift_rows(av, d, 1.0, crow)
            b_sh = _shift_rows(bv, d, 0.0, crow)
            bv = av * b_sh + bv
            av = av * a_sh
            d *= 2
        h = bv + av * h_prev
        o_ref[rows, :] = (h * _gelu_tanh(gl_ref[rows, :].astype(F32))).astype(o_ref.dtype)
        return h[c - 1:c, :]

    lax.fori_loop(0, s // c, chunk, jnp.zeros((1, LRU_BW), F32))


def rglru(proj, conv_w, conv_b, w_ai, b_ai, lam, batch):
    t = proj.shape[0]
    x_off = 4 * RET_W // LRU_BW
    g_off = x_off + LRU_W // LRU_BW
    vec = lambda rows: pl.BlockSpec((rows, LRU_BW), lambda b, g: (0, g))
    return pl.pallas_call(
        _rglru_kernel,
        grid=(batch, LRU_BLOCKS),
        in_specs=[
            pl.BlockSpec((SEQ, LRU_BW), lambda b, g: (b, x_off + g)),
            pl.BlockSpec((SEQ, LRU_BW), lambda b, g: (b, g_off + g)),
            vec(LRU_CONV), vec(1),
            pl.BlockSpec((1, LRU_BW, 2 * LRU_BW), lambda b, g: (g, 0, 0)),
            pl.BlockSpec((1, 2 * LRU_BW), lambda b, g: (0, g)),
            vec(1),
        ],
        out_specs=pl.BlockSpec((SEQ, LRU_BW), lambda b, g: (b, g)),
        out_shape=jax.ShapeDtypeStruct((t, LRU_W), BF16),
        scratch_shapes=[pltpu.VMEM((SEQ, LRU_BW), F32), pltpu.VMEM((SEQ, LRU_BW), F32)],
        compiler_params=_params("arbitrary", "arbitrary"),
        name="rglru",
    )(proj, proj, conv_w, conv_b.reshape(1, LRU_W), w_ai, b_ai, lam.reshape(1, LRU_W))


def _proj_res_kernel(*refs, splits):
    part_refs = refs[:len(splits)]
    w_ref, x_ref, o_ref = refs[len(splits):]
    acc = x_ref[...]
    off = 0
    for a_ref, k in zip(part_refs, splits):
        acc = acc + jnp.dot(a_ref[...], w_ref[off:off + k, :], preferred_element_type=F32)
        off += k
    o_ref[...] = acc


def proj_residual(parts, w, x, *, tm):
    t, n = x.shape
    splits = tuple(p.shape[1] for p in parts)
    return pl.pallas_call(
        functools.partial(_proj_res_kernel, splits=splits),
        grid=(t // tm,),
        in_specs=[pl.BlockSpec((tm, k), lambda i: (i, 0)) for k in splits] + [
            pl.BlockSpec(w.shape, lambda i: (0, 0)),
            pl.BlockSpec((tm, n), lambda i: (i, 0)),
        ],
        out_specs=pl.BlockSpec((tm, n), lambda i: (i, 0)),
        out_shape=jax.ShapeDtypeStruct((t, n), F32),
        compiler_params=_params("arbitrary"),
        name="proj_residual",
    )(*parts, w, x)


def _ffn_up_kernel(x_ref, g_ref, wg_ref, wv_ref, cwg_ref, cwv_ref, cbg_ref, cbv_ref, o_ref,
                   hn_ref, carry_g_ref, carry_v_ref, *, tiles_per_seq):
    i, j = pl.program_id(0), pl.program_id(1)
    tm = x_ref.shape[0]

    @pl.when(j == 0)
    def _():
        x = x_ref[...]
        hn_ref[...] = (x * _rms_scale(x) * g_ref[...]).astype(BF16)

    seq_start = (i % tiles_per_seq) == 0

    def conv(w_ref, cw_ref, cb_ref, carry_ref):
        u = jnp.dot(hn_ref[...], w_ref[...], preferred_element_type=F32)
        halo = jnp.where(seq_start, 0.0, carry_ref[j])
        carry_ref[j] = u[tm - SUBLANES:, :]
        ext = jnp.concatenate([halo, u], axis=0)
        out = u * cw_ref[FFN_CONV - 1:FFN_CONV, :] + cb_ref[...]
        for d in range(1, FFN_CONV):
            prev = pltpu.roll(ext, d, axis=0)[SUBLANES:, :]
            out = out + prev * cw_ref[FFN_CONV - 1 - d:FFN_CONV - d, :]
        return out

    gate = conv(wg_ref, cwg_ref, cbg_ref, carry_g_ref)
    val = conv(wv_ref, cwv_ref, cbv_ref, carry_v_ref)
    o_ref[...] = (_gelu_tanh(gate) * val).astype(o_ref.dtype)


def ffn_up(x, g, w_up, conv_w, conv_b, *, tm, tn):
    t, d = x.shape
    nj = D_FF // tn
    cb = conv_b.reshape(1, 2 * D_FF)
    return pl.pallas_call(
        functools.partial(_ffn_up_kernel, tiles_per_seq=SEQ // tm),
        grid=(t // tm, nj),
        in_specs=[
            pl.BlockSpec((tm, d), lambda i, j: (i, 0)),
            pl.BlockSpec((1, d), lambda i, j: (0, 0)),
            pl.BlockSpec((d, tn), lambda i, j: (0, j)),
            pl.BlockSpec((d, tn), lambda i, j: (0, nj + j)),
            pl.BlockSpec((FFN_CONV, tn), lambda i, j: (0, j)),
            pl.BlockSpec((FFN_CONV, tn), lambda i, j: (0, nj + j)),
            pl.BlockSpec((1, tn), lambda i, j: (0, j)),
            pl.BlockSpec((1, tn), lambda i, j: (0, nj + j)),
        ],
        out_specs=pl.BlockSpec((tm, tn), lambda i, j: (i, j)),
        out_shape=jax.ShapeDtypeStruct((t, D_FF), BF16),
        scratch_shapes=[pltpu.VMEM((tm, d), BF16),
                        pltpu.VMEM((nj, SUBLANES, tn), F32),
                        pltpu.VMEM((nj, SUBLANES, tn), F32)],
        compiler_params=_params("arbitrary", "arbitrary"),
        name="ffn_up",
    )(x, g.reshape(1, d), w_up, w_up, conv_w, conv_w, cb, cb)


def _mm_res_kernel(a_ref, w_ref, x_ref, o_ref):
    o_ref[...] = x_ref[...] + jnp.dot(a_ref[...], w_ref[...], preferred_element_type=F32)


def matmul_residual(a, w, x, *, tm, tn):
    t, k = a.shape
    n = w.shape[1]
    return pl.pallas_call(
        _mm_res_kernel,
        grid=(n // tn, t // tm),
        in_specs=[
            pl.BlockSpec((tm, k), lambda j, i: (i, 0)),
            pl.BlockSpec((k, tn), lambda j, i: (0, j)),
            pl.BlockSpec((tm, tn), lambda j, i: (i, j)),
        ],
        out_specs=pl.BlockSpec((tm, tn), lambda j, i: (i, j)),
        out_shape=jax.ShapeDtypeStruct((t, n), F32),
        compiler_params=_params("arbitrary", "arbitrary"),
        name="matmul_residual",
    )(a, w, x)


def _moba_kernel(relb_ref, bkt_ref, qg_ref, kg_ref, q_ref, k_ref, v_ref, o_ref,
                 bias_ref, qs_ref, ks_ref, gate_ref):
    h, b = pl.program_id(0), pl.program_id(1)
    nb = SEQ // MOBA_BLOCK
    blk = MOBA_BLOCK

    @pl.when(b == 0)
    def _():
        for m in range(2):
            bk = bkt_ref[m]
            acc = jnp.full(bk.shape, NEG_INF, F32)
            for t in range(REL_BUCKETS):
                acc = jnp.where(bk == t, relb_ref[t, h], acc)
            bias_ref[m] = acc

    far_bias = relb_ref[REL_BUCKETS - 1, h]

    q = q_ref[...].astype(F32)
    qs_ref[...] = (q * _rms_scale(q) * qg_ref[...] * (MOBA_DH ** -0.5)).astype(BF16)
    k = k_ref[...].astype(F32)
    kn = k * _rms_scale(k) * kg_ref[...]
    ks_ref[...] = kn.astype(BF16)

    kmean = jnp.concatenate(
        [jnp.mean(kn[n * blk:(n + 1) * blk, :], axis=0, keepdims=True) for n in range(nb)], axis=0)
    kmean_rep = jnp.concatenate([kmean] * (MOBA_DH // nb), axis=0).astype(BF16)
    gate_ref[...] = lax.dot_general(qs_ref[...], kmean_rep, (((1,), (1,)), ((), ())),
                                    preferred_element_type=F32)

    lane = lax.broadcasted_iota(jnp.int32, (blk, MOBA_DH), 1)
    lane_blk = lane % nb
    nt = (((1,), (1,)), ((), ()))

    for i in range(nb):
        rows = slice(i * blk, (i + 1) * blk)
        qi = qs_ref[rows, :]

        if i >= 1:
            valid = lane_blk < i
            gm = jnp.where(valid, gate_ref[rows, :], NEG_INF)
            cnt = jnp.zeros(gm.shape, jnp.int32)
            for s in range(1, nb):
                gs = pltpu.roll(gm, s, axis=1)
                other_lower = ((lane_blk - s) % nb) < lane_blk
                beats = (gs > gm) | ((gs == gm) & other_lower)
                cnt = cnt + beats.astype(jnp.int32)
            sel_bias = jnp.where(valid & (cnt < MOBA_TOPK), 0.0, NEG_INF)

        s_own = lax.dot_general(qi, ks_ref[rows, :], nt, preferred_element_type=F32) + bias_ref[0]
        m_run = jnp.max(s_own, axis=-1, keepdims=True)
        p = jnp.exp(s_own - m_run)
        l_run = jnp.sum(p, axis=-1, keepdims=True)
        acc = jnp.dot(p.astype(BF16), v_ref[rows, :], preferred_element_type=F32)

        for jb in range(i - 1, -1, -1):
            krows = slice(jb * blk, (jb + 1) * blk)
            s_blk = lax.dot_general(qi, ks_ref[krows, :], nt, preferred_element_type=F32)
            s_blk = s_blk + (bias_ref[1] if jb == i - 1 else far_bias)
            s_blk = s_blk + sel_bias[:, jb:jb + 1]
            m_new = jnp.maximum(m_run, jnp.max(s_blk, axis=-1, keepdims=True))
            alpha = jnp.exp(m_run - m_new)
            p = jnp.exp(s_blk - m_new)
            l_run = alpha * l_run + jnp.sum(p, axis=-1, keepdims=True)
            acc = alpha * acc + jnp.dot(p.astype(BF16), v_ref[krows, :], preferred_element_type=F32)
            m_run = m_new

        o_ref[rows, :] = (acc / l_run).astype(o_ref.dtype)


def moba_bucket_tables():
    r = jnp.arange(MOBA_BLOCK)
    dist_own = r[:, None] - r[None, :]

    def bucket(dist):
        n = jnp.maximum(dist, 0)
        max_exact = REL_BUCKETS // 2
        nf = jnp.maximum(n, 1).astype(F32)
        large = max_exact + (jnp.log(nf / max_exact) / math.log(REL_MAX_DIST / max_exact)
                             * (REL_BUCKETS - max_exact)).astype(jnp.int32)
        large = jnp.minimum(large, REL_BUCKETS - 1)
        return jnp.where(n < max_exact, n, large)

    own = jnp.where(dist_own >= 0, bucket(dist_own), -1)
    adj = bucket(dist_own + MOBA_BLOCK)
    return jnp.stack([own, adj]).astype(jnp.int32)


def moba(proj, q_norm, k_norm, rel_bias, batch):
    t = proj.shape[0]
    col = lambda off: pl.BlockSpec((SEQ, MOBA_DH), lambda h, b: (b, off + h))
    return pl.pallas_call(
        _moba_kernel,
        grid=(MOBA_HEADS, batch),
        in_specs=[
            pl.BlockSpec(memory_space=pltpu.SMEM),
            pl.BlockSpec((2, MOBA_BLOCK, MOBA_BLOCK), lambda h, b: (0, 0, 0)),
            pl.BlockSpec((1, MOBA_DH), lambda h, b: (0, 0)),
            pl.BlockSpec((1, MOBA_DH), lambda h, b: (0, 0)),
            col(0), col(MOBA_HEADS), col(2 * MOBA_HEADS),
        ],
        out_specs=pl.BlockSpec((SEQ, MOBA_DH), lambda h, b: (b, h)),
        out_shape=jax.ShapeDtypeStruct((t, MOBA_HEADS * MOBA_DH), BF16),
        scratch_shapes=[pltpu.VMEM((2, MOBA_BLOCK, MOBA_BLOCK), F32),
                        pltpu.VMEM((SEQ, MOBA_DH), BF16),
                        pltpu.VMEM((SEQ, MOBA_DH), BF16),
                        pltpu.VMEM((SEQ, MOBA_DH), F32)],
        compiler_params=_params("arbitrary", "arbitrary"),
        name="moba",
    )(rel_bias, moba_bucket_tables(), q_norm.reshape(1, MOBA_DH), k_norm.reshape(1, MOBA_DH),
      proj, proj, proj)


def kernel(x, norm_mix, norm_ffn, ev_w_in, ret_norm, lru_conv_w, lru_conv_b, lru_w_a, lru_b_a,
           lru_w_i, lru_b_i, lru_lambda, ev_w_out, od_w_in, q_norm, k_norm, od_w_out, rel_bias,
           ffn_w_up, ffn_conv_w, ffn_conv_b, ffn_w_down):
    batch, seq, d = x.shape
    assert (seq, d) == (SEQ, D_MODEL)
    xf = x.reshape(batch * seq, d)

    def ffn(xf, l):
        act = ffn_up(xf, norm_ffn[l], ffn_w_up[l].astype(BF16), ffn_conv_w[l], ffn_conv_b[l],
                     tm=1024, tn=512)
        return matmul_residual(act, ffn_w_down[l].astype(BF16), xf, tm=512, tn=1024)

    proj = norm_matmul(xf, norm_mix[0], ev_w_in[0].astype(BF16), tm=1024, tn=1024, out_dtype=BF16)
    ret = retention(proj, retention_tables(), ret_norm[0], batch)
    w_ai = jnp.concatenate([lru_w_a[0], lru_w_i[0]], axis=-1).astype(BF16)
    b_ai = jnp.concatenate([lru_b_a[0].reshape(LRU_BLOCKS, LRU_BW),
                            lru_b_i[0].reshape(LRU_BLOCKS, LRU_BW)], axis=-1).reshape(1, 2 * LRU_W)
    lru = rglru(proj, lru_conv_w[0], lru_conv_b[0], w_ai, b_ai, lru_lambda[0], batch)
    xf = proj_residual([ret, lru], ev_w_out[0].astype(BF16), xf, tm=512)
    xf = ffn(xf, 0)

    proj = norm_matmul(xf, norm_mix[1], od_w_in[0].astype(BF16), tm=1024, tn=1024, out_dtype=BF16)
    att = moba(proj, q_norm[0], k_norm[0], rel_bias, batch)
    xf = proj_residual([att], od_w_out[0].astype(BF16), xf, tm=512)
    xf = ffn(xf, 1)
    return xf.reshape(batch, seq, d)
```

```python
import functools
import math

import jax
import jax.numpy as jnp
from jax import lax
from jax.experimental import pallas as pl
from jax.experimental.pallas import tpu as pltpu

D_MODEL = 2048
SEQ = 2048
RET_HEADS = 4
RET_DK = 256
RET_DV = 256
RET_W = RET_HEADS * RET_DK
RET_CHUNK = 128
ROPE_BASE = 10000.0
LRU_W = 1024
LRU_BLOCKS = 8
LRU_BW = 128
LRU_CONV = 4
LRU_C = 8.0
LRU_SCAN_CHUNK = 64
MOBA_HEADS = 16
MOBA_DH = 128
MOBA_BLOCK = 256
MOBA_TOPK = 3
REL_BUCKETS = 32
REL_MAX_DIST = 128
D_FF = 5632
FFN_CONV = 3
EPS = 1e-6

VMEM_LIMIT_BYTES = 56 * 1024 * 1024
SUBLANES = 8

BF16 = jnp.bfloat16
F32 = jnp.float32
NEG_INF = float("-inf")
MASK_VALUE = -1e30
LOG2E = math.log2(math.e)


def _params(*sem):
    return pltpu.CompilerParams(dimension_semantics=sem, vmem_limit_bytes=VMEM_LIMIT_BYTES)


def _gelu_tanh(x):
    c = math.sqrt(2.0 / math.pi)
    return 0.5 * x * (1.0 + jnp.tanh(c * (x + 0.044715 * (x * x * x))))


def _rms_scale(x):
    return lax.rsqrt(jnp.mean(x * x, axis=-1, keepdims=True) + EPS)


def _norm_mm_kernel(x_ref, g_ref, w_ref, o_ref, hn_ref):
    @pl.when(pl.program_id(1) == 0)
    def _():
        x = x_ref[...]
        hn_ref[...] = (x * _rms_scale(x) * g_ref[...]).astype(BF16)

    o_ref[...] = jnp.dot(hn_ref[...], w_ref[...], preferred_element_type=F32).astype(o_ref.dtype)


def norm_matmul(x, g, w, *, tm, tn, out_dtype):
    t, d = x.shape
    n = w.shape[1]
    return pl.pallas_call(
        _norm_mm_kernel,
        grid=(t // tm, n // tn),
        in_specs=[
            pl.BlockSpec((tm, d), lambda i, j: (i, 0)),
            pl.BlockSpec((1, d), lambda i, j: (0, 0)),
            pl.BlockSpec((d, tn), lambda i, j: (0, j)),
        ],
        out_specs=pl.BlockSpec((tm, tn), lambda i, j: (i, j)),
        out_shape=jax.ShapeDtypeStruct((t, n), out_dtype),
        scratch_shapes=[pltpu.VMEM((tm, d), BF16)],
        compiler_params=_params("arbitrary", "arbitrary"),
        name="norm_matmul",
    )(x, g.reshape(1, d), w)


def _retention_kernel(q_ref, k_ref, v_ref, g_ref, cos_ref, sin_ref, decay_ref, zeta_ref, xi_ref,
                      gch_ref, rn_ref, o_ref, state_ref):
    state_ref[...] = jnp.zeros_like(state_ref)
    half = RET_DK // 2
    n_chunks = q_ref.shape[0] // RET_CHUNK

    def rotary(x, cos, sin):
        x1, x2 = x[:, :half], x[:, half:]
        return jnp.concatenate([x1 * cos - x2 * sin, x1 * sin + x2 * cos], axis=-1)

    def chunk(n, carry):
        rows = pl.ds(pl.multiple_of(n * RET_CHUNK, RET_CHUNK), RET_CHUNK)
        cos, sin = cos_ref[rows, :], sin_ref[rows, :]
        qr = rotary(q_ref[rows, :].astype(F32), cos, sin)
        kr = rotary(k_ref[rows, :].astype(F32), cos, sin) * (RET_DK ** -0.5)
        vb = v_ref[rows, :]
        qb = qr.astype(BF16)
        scores = lax.dot_general(qb, kr.astype(BF16), (((1,), (1,)), ((), ())),
                                 preferred_element_type=F32) * decay_ref[0]
        inner = jnp.dot(scores.astype(BF16), vb, preferred_element_type=F32)
        state = state_ref[...]
        cross = jnp.dot(qb, state.astype(BF16), preferred_element_type=F32) * xi_ref[0]
        kz = (kr * zeta_ref[0]).astype(BF16)
        kv = lax.dot_general(kz, vb, (((0,), (0,)), ((), ())), preferred_element_type=F32)
        state_ref[...] = gch_ref[0] * state + kv
        out = inner + cross
        out = out * _rms_scale(out) * rn_ref[0]
        g = g_ref[rows, :].astype(F32)
        o_ref[rows, :] = (out * (g * jax.nn.sigmoid(g))).astype(o_ref.dtype)
        return carry

    lax.fori_loop(0, n_chunks, chunk, 0)


def retention(proj, tables, ret_norm, batch):
    cos, sin, decay, zeta, xi, gch = tables
    t = proj.shape[0]
    hq = RET_W // RET_DK
    col = lambda off: pl.BlockSpec((SEQ, RET_DK), lambda b, h: (b, off + h))
    per_head = lambda shape: pl.BlockSpec((1,) + shape, lambda b, h: (h, 0, 0))
    full = pl.BlockSpec((SEQ, RET_DK // 2), lambda b, h: (0, 0))
    return pl.pallas_call(
        _retention_kernel,
        grid=(batch, RET_HEADS),
        in_specs=[col(0), col(hq), col(2 * hq), col(3 * hq), full, full,
                  per_head((RET_CHUNK, RET_CHUNK)), per_head((RET_CHUNK, RET_DK)),
                  per_head((RET_CHUNK, RET_DV)), per_head((1, RET_DV)), per_head((1, RET_DV))],
        out_specs=pl.BlockSpec((SEQ, RET_DV), lambda b, h: (b, h)),
        out_shape=jax.ShapeDtypeStruct((t, RET_HEADS * RET_DV), BF16),
        scratch_shapes=[pltpu.VMEM((RET_DK, RET_DV), F32)],
        compiler_params=_params("arbitrary", "arbitrary"),
        name="retention",
    )(proj, proj, proj, proj, cos, sin, decay, zeta, xi, gch,
      ret_norm.reshape(RET_HEADS, 1, RET_DV))


def retention_tables():
    half = RET_DK // 2
    pos = jnp.arange(SEQ)
    freqs = ROPE_BASE ** (-jnp.arange(half, dtype=F32) / half)
    ang = pos.astype(F32)[:, None] * freqs[None, :]
    cos, sin = jnp.cos(ang), jnp.sin(ang)
    log_g = jnp.log1p(-jnp.exp2(-5.0 - jnp.arange(RET_HEADS, dtype=F32)))
    i = jnp.arange(RET_CHUNK)
    diff = i[:, None] - i[None, :]
    decay = jnp.where(diff >= 0, jnp.exp(log_g[:, None, None] * jnp.maximum(diff, 0)), 0.0)
    zeta = jnp.exp(log_g[None, :] * (RET_CHUNK - 1 - i)[:, None])
    xi = jnp.exp(log_g[None, :] * (i + 1)[:, None])
    g_chunk = jnp.exp(log_g * RET_CHUNK)
    zeta_b = jnp.broadcast_to(zeta.T[:, :, None], (RET_HEADS, RET_CHUNK, RET_DK))
    xi_b = jnp.broadcast_to(xi.T[:, :, None], (RET_HEADS, RET_CHUNK, RET_DV))
    gch_b = jnp.broadcast_to(g_chunk[:, None, None], (RET_HEADS, 1, RET_DV))
    return cos, sin, decay, zeta_b, xi_b, gch_b


def _shift_rows(v, d, fill, row):
    if d % SUBLANES == 0:
        return jnp.concatenate([jnp.full((d, v.shape[1]), fill, v.dtype), v[:-d]], axis=0)
    return jnp.where(row >= d, pltpu.roll(v, d, axis=0), fill)


def _rglru_kernel(xl_ref, gl_ref, cw_ref, cb_ref, wai_ref, bai_ref, lam_ref, o_ref, a_ref, u_ref):
    s = xl_ref.shape[0]
    x = xl_ref[...].astype(F32)
    row = lax.broadcasted_iota(jnp.int32, x.shape, 0)
    xc = x * cw_ref[LRU_CONV - 1:LRU_CONV, :] + cb_ref[...]
    for d in range(1, LRU_CONV):
        xs = jnp.where(row >= d, pltpu.roll(x, d, axis=0), 0.0)
        xc = xc + xs * cw_ref[LRU_CONV - 1 - d:LRU_CONV - d, :]
    y = jnp.dot(xc.astype(BF16), wai_ref[0], preferred_element_type=F32) + bai_ref[...]
    r = jax.nn.sigmoid(y[:, :LRU_BW])
    gi = jax.nn.sigmoid(y[:, LRU_BW:])
    neg_lam = -lam_ref[...]
    softplus = jnp.maximum(neg_lam, 0.0) + jnp.log1p(jnp.exp(-jnp.abs(neg_lam)))
    log_a = (-LRU_C) * r * softplus
    a = jnp.exp(log_a)
    a_ref[...] = a
    u_ref[...] = jnp.sqrt(1.0 - a * a) * (gi * xc)

    c = LRU_SCAN_CHUNK
    crow = lax.broadcasted_iota(jnp.int32, (c, LRU_BW), 0)

    def chunk(n, h_prev):
        rows = pl.ds(pl.multiple_of(n * c, c), c)
        av, bv = a_ref[rows, :], u_ref[rows, :]
        d = 1
        while d < c:
            a_sh = _shift_rows(av, d, 1.0, crow)
            b_sh = _shift_rows(bv, d, 0.0, crow)
            bv = av * b_sh + bv
            av = av * a_sh
            d *= 2
        h = bv + av * h_prev
        o_ref[rows, :] = (h * _gelu_tanh(gl_ref[rows, :].astype(F32))).astype(o_ref.dtype)
        return h[c - 1:c, :]

    lax.fori_loop(0, s // c, chunk, jnp.zeros((1, LRU_BW), F32))


def rglru(proj, conv_w, conv_b, w_ai, b_ai, lam, batch):
    t = proj.shape[0]
    x_off = 4 * RET_W // LRU_BW
    g_off = x_off + LRU_W // LRU_BW
    vec = lambda rows: pl.BlockSpec((rows, LRU_BW), lambda b, g: (0, g))
    return pl.pallas_call(
        _rglru_kernel,
        grid=(batch, LRU_BLOCKS),
        in_specs=[
            pl.BlockSpec((SEQ, LRU_BW), lambda b, g: (b, x_off + g)),
            pl.BlockSpec((SEQ, LRU_BW), lambda b, g: (b, g_off + g)),
            vec(LRU_CONV), vec(1),
            pl.BlockSpec((1, LRU_BW, 2 * LRU_BW), lambda b, g: (g, 0, 0)),
            pl.BlockSpec((1, 2 * LRU_BW), lambda b, g: (0, g)),
            vec(1),
        ],
        out_specs=pl.BlockSpec((SEQ, LRU_BW), lambda b, g: (b, g)),
        out_shape=jax.ShapeDtypeStruct((t, LRU_W), BF16),
        scratch_shapes=[pltpu.VMEM((SEQ, LRU_BW), F32), pltpu.VMEM((SEQ, LRU_BW), F32)],
        compiler_params=_params("arbitrary", "arbitrary"),
        name="rglru",
    )(proj, proj, conv_w, conv_b.reshape(1, LRU_W), w_ai, b_ai, lam.reshape(1, LRU_W))


def _proj_res_kernel(*refs, splits):
    part_refs = refs[:len(splits)]
    w_ref, x_ref, o_ref = refs[len(splits):]
    acc = x_ref[...]
    off = 0
    for a_ref, k in zip(part_refs, splits):
        acc = acc + jnp.dot(a_ref[...], w_ref[off:off + k, :], preferred_element_type=F32)
        off += k
    o_ref[...] = acc


def proj_residual(parts, w, x, *, tm):
    t, n = x.shape
    splits = tuple(p.shape[1] for p in parts)
    return pl.pallas_call(
        functools.partial(_proj_res_kernel, splits=splits),
        grid=(t // tm,),
        in_specs=[pl.BlockSpec((tm, k), lambda i: (i, 0)) for k in splits] + [
            pl.BlockSpec(w.shape, lambda i: (0, 0)),
            pl.BlockSpec((tm, n), lambda i: (i, 0)),
        ],
        out_specs=pl.BlockSpec((tm, n), lambda i: (i, 0)),
        out_shape=jax.ShapeDtypeStruct((t, n), F32),
        compiler_params=_params("arbitrary"),
        name="proj_residual",
    )(*parts, w, x)


def _ffn_up_kernel(x_ref, g_ref, wg_ref, wv_ref, cwg_ref, cwv_ref, cbg_ref, cbv_ref, o_ref,
                   hn_ref, carry_g_ref, carry_v_ref, *, tiles_per_seq):
    i, j = pl.program_id(0), pl.program_id(1)
    tm = x_ref.shape[0]

    @pl.when(j == 0)
    def _():
        x = x_ref[...]
        hn_ref[...] = (x * _rms_scale(x) * g_ref[...]).astype(BF16)

    seq_start = (i % tiles_per_seq) == 0

    def conv(w_ref, cw_ref, cb_ref, carry_ref):
        u = jnp.dot(hn_ref[...], w_ref[...], preferred_element_type=F32)
        halo = jnp.where(seq_start, 0.0, carry_ref[j])
        carry_ref[j] = u[tm - SUBLANES:, :]
        ext = jnp.concatenate([halo, u], axis=0)
        out = u * cw_ref[FFN_CONV - 1:FFN_CONV, :] + cb_ref[...]
        for d in range(1, FFN_CONV):
            prev = pltpu.roll(ext, d, axis=0)[SUBLANES:, :]
            out = out + prev * cw_ref[FFN_CONV - 1 - d:FFN_CONV - d, :]
        return out

    gate = conv(wg_ref, cwg_ref, cbg_ref, carry_g_ref)
    val = conv(wv_ref, cwv_ref, cbv_ref, carry_v_ref)
    o_ref[...] = (_gelu_tanh(gate) * val).astype(o_ref.dtype)


def ffn_up(x, g, w_up, conv_w, conv_b, *, tm, tn):
    t, d = x.shape
    nj = D_FF // tn
    cb = conv_b.reshape(1, 2 * D_FF)
    return pl.pallas_call(
        functools.partial(_ffn_up_kernel, tiles_per_seq=SEQ // tm),
        grid=(t // tm, nj),
        in_specs=[
            pl.BlockSpec((tm, d), lambda i, j: (i, 0)),
            pl.BlockSpec((1, d), lambda i, j: (0, 0)),
            pl.BlockSpec((d, tn), lambda i, j: (0, j)),
            pl.BlockSpec((d, tn), lambda i, j: (0, nj + j)),
            pl.BlockSpec((FFN_CONV, tn), lambda i, j: (0, j)),
            pl.BlockSpec((FFN_CONV, tn), lambda i, j: (0, nj + j)),
            pl.BlockSpec((1, tn), lambda i, j: (0, j)),
            pl.BlockSpec((1, tn), lambda i, j: (0, nj + j)),
        ],
        out_specs=pl.BlockSpec((tm, tn), lambda i, j: (i, j)),
        out_shape=jax.ShapeDtypeStruct((t, D_FF), BF16),
        scratch_shapes=[pltpu.VMEM((tm, d), BF16),
                        pltpu.VMEM((nj, SUBLANES, tn), F32),
                        pltpu.VMEM((nj, SUBLANES, tn), F32)],
        compiler_params=_params("arbitrary", "arbitrary"),
        name="ffn_up",
    )(x, g.reshape(1, d), w_up, w_up, conv_w, conv_w, cb, cb)


def _mm_res_kernel(a_ref, w_ref, x_ref, o_ref):
    o_ref[...] = x_ref[...] + jnp.dot(a_ref[...], w_ref[...], preferred_element_type=F32)


def matmul_residual(a, w, x, *, tm, tn):
    t, k = a.shape
    n = w.shape[1]
    return pl.pallas_call(
        _mm_res_kernel,
        grid=(n // tn, t // tm),
        in_specs=[
            pl.BlockSpec((tm, k), lambda j, i: (i, 0)),
            pl.BlockSpec((k, tn), lambda j, i: (0, j)),
            pl.BlockSpec((tm, tn), lambda j, i: (i, j)),
        ],
        out_specs=pl.BlockSpec((tm, tn), lambda j, i: (i, j)),
        out_shape=jax.ShapeDtypeStruct((t, n), F32),
        compiler_params=_params("arbitrary", "arbitrary"),
        name="matmul_residual",
    )(a, w, x)


def _moba_kernel(relb_ref, bkt_ref, qg_ref, kg_ref, q_ref, k_ref, v_ref, o_ref,
                 bias_ref, qt_ref, ks_ref, vt_ref):
    h, b = pl.program_id(0), pl.program_id(1)
    nb = SEQ // MOBA_BLOCK
    blk = MOBA_BLOCK
    dh = MOBA_DH

    @pl.when((h == 0) & (b == 0))
    def _():
        key_blk = lax.broadcasted_iota(jnp.int32, (SEQ, dh), 0) // blk
        lane = lax.broadcasted_iota(jnp.int32, (SEQ, dh), 1)
        onehot = (lane < 2 * nb) & ((lane % nb) == key_blk)
        ks_ref[:, dh:] = jnp.where(onehot, 1.0, 0.0).astype(BF16)
        qt_ref[dh:, :] = jnp.zeros((dh, SEQ), BF16)
        vt_ref[dh:, :] = jnp.ones((vt_ref.shape[0] - dh, SEQ), BF16)

    @pl.when(b == 0)
    def _():
        for m in range(2):
            bk = bkt_ref[m]
            acc = jnp.full(bk.shape, NEG_INF, F32)
            for t in range(REL_BUCKETS):
                acc = jnp.where(bk == t, relb_ref[t, h] * LOG2E, acc)
            bias_ref[m] = acc

    far_bias = relb_ref[REL_BUCKETS - 1, h] * LOG2E

    q = q_ref[...].astype(F32)
    qt_ref[:dh, :] = (q * _rms_scale(q) * (qg_ref[...] * (dh ** -0.5 * LOG2E))).T.astype(BF16)
    k = k_ref[...].astype(F32)
    kn = k * _rms_scale(k) * kg_ref[...]
    ks_ref[:, :dh] = kn.astype(BF16)
    vt_ref[:dh, :] = v_ref[...].astype(F32).T.astype(BF16)

    kmean = jnp.concatenate(
        [jnp.mean(kn[n * blk:(n + 1) * blk, :], axis=0, keepdims=True) for n in range(nb)], axis=0)
    gate = jnp.dot(kmean.astype(BF16), qt_ref[:dh, :], preferred_element_type=F32)
    blk_id = lax.broadcasted_iota(jnp.int32, gate.shape, 0)
    q_blk = lax.broadcasted_iota(jnp.int32, gate.shape, 1) // blk
    valid = blk_id < q_blk
    gm = jnp.where(valid, gate, NEG_INF)
    cnt = jnp.zeros(gate.shape, F32)
    for s in range(1, nb):
        gs = pltpu.roll(gm, s, axis=0)
        tie = jnp.where(((blk_id - s) % nb) < blk_id, 1.0, 0.0)
        cnt = cnt + jnp.where(gs > gm, 1.0, jnp.where(gs == gm, tie, 0.0))
    picked = jnp.where(blk_id < q_blk - 1, far_bias, 0.0)
    row = jnp.where(valid, jnp.where(cnt < MOBA_TOPK, picked, MASK_VALUE), 0.0)
    row_hi = row.astype(BF16).astype(F32)
    row_lo = jnp.where(row > 0.5 * MASK_VALUE, row - row_hi, 0.0)
    qt_ref[dh:dh + 2 * nb, :] = jnp.concatenate([row_hi, row_lo], axis=0).astype(BF16)

    def scores(i):
        return jnp.dot(ks_ref[:(i + 1) * blk, :], qt_ref[:, i * blk:(i + 1) * blk],
                       preferred_element_type=F32)

    def softmax(i, s_all):
        tiles = []
        for jb in range(i + 1):
            s_t = s_all[jb * blk:(jb + 1) * blk, :]
            if jb >= i - 1:
                s_t = s_t + bias_ref[i - jb]
            tiles.append(s_t)
        m = functools.reduce(jnp.maximum, [jnp.max(t, axis=0, keepdims=True) for t in tiles])
        return jnp.concatenate([jnp.exp2(t - m).astype(BF16) for t in tiles], axis=0)

    def attend(i, p_all):
        acc = jnp.dot(vt_ref[:, :(i + 1) * blk], p_all, preferred_element_type=F32)
        out = acc[:dh, :] / acc[dh:dh + 1, :]
        o_ref[i * blk:(i + 1) * blk, :] = out.T.astype(o_ref.dtype)

    s_next = scores(0)
    pending = None
    for i in range(nb):
        s_all = s_next
        if i + 1 < nb:
            s_next = scores(i + 1)
        p_all = softmax(i, s_all)
        if pending is not None:
            attend(*pending)
        pending = (i, p_all)
    attend(*pending)


def moba_bucket_tables():
    r = jnp.arange(MOBA_BLOCK)
    dist_own = r[None, :] - r[:, None]

    def bucket(dist):
        n = jnp.maximum(dist, 0)
        max_exact = REL_BUCKETS // 2
        nf = jnp.maximum(n, 1).astype(F32)
        large = max_exact + (jnp.log(nf / max_exact) / math.log(REL_MAX_DIST / max_exact)
                             * (REL_BUCKETS - max_exact)).astype(jnp.int32)
        large = jnp.minimum(large, REL_BUCKETS - 1)
        return jnp.where(n < max_exact, n, large)

    own = jnp.where(dist_own >= 0, bucket(dist_own), -1)
    adj = bucket(dist_own + MOBA_BLOCK)
    return jnp.stack([own, adj]).astype(jnp.int32)


def moba(proj, q_norm, k_norm, rel_bias, batch):
    t = proj.shape[0]
    col = lambda off: pl.BlockSpec((SEQ, MOBA_DH), lambda h, b: (b, off + h))
    return pl.pallas_call(
        _moba_kernel,
        grid=(MOBA_HEADS, batch),
        in_specs=[
            pl.BlockSpec(memory_space=pltpu.SMEM),
            pl.BlockSpec((2, MOBA_BLOCK, MOBA_BLOCK), lambda h, b: (0, 0, 0)),
            pl.BlockSpec((1, MOBA_DH), lambda h, b: (0, 0)),
            pl.BlockSpec((1, MOBA_DH), lambda h, b: (0, 0)),
            col(0), col(MOBA_HEADS), col(2 * MOBA_HEADS),
        ],
        out_specs=pl.BlockSpec((SEQ, MOBA_DH), lambda h, b: (b, h)),
        out_shape=jax.ShapeDtypeStruct((t, MOBA_HEADS * MOBA_DH), BF16),
        scratch_shapes=[pltpu.VMEM((2, MOBA_BLOCK, MOBA_BLOCK), F32),
                        pltpu.VMEM((2 * MOBA_DH, SEQ), BF16),
                        pltpu.VMEM((SEQ, 2 * MOBA_DH), BF16),
                        pltpu.VMEM((MOBA_DH + 2 * SUBLANES, SEQ), BF16)],
        compiler_params=_params("arbitrary", "arbitrary"),
        name="moba",
    )(rel_bias, moba_bucket_tables(), q_norm.reshape(1, MOBA_DH), k_norm.reshape(1, MOBA_DH),
      proj, proj, proj)


def kernel(x, norm_mix, norm_ffn, ev_w_in, ret_norm, lru_conv_w, lru_conv_b, lru_w_a, lru_b_a,
           lru_w_i, lru_b_i, lru_lambda, ev_w_out, od_w_in, q_norm, k_norm, od_w_out, rel_bias,
           ffn_w_up, ffn_conv_w, ffn_conv_b, ffn_w_down):
    batch, seq, d = x.shape
    assert (seq, d) == (SEQ, D_MODEL)
    xf = x.reshape(batch * seq, d)

    def ffn(xf, l):
        act = ffn_up(xf, norm_ffn[l], ffn_w_up[l].astype(BF16), ffn_conv_w[l], ffn_conv_b[l],
                     tm=1024, tn=512)
        return matmul_residual(act, ffn_w_down[l].astype(BF16), xf, tm=512, tn=1024)

    proj = norm_matmul(xf, norm_mix[0], ev_w_in[0].astype(BF16), tm=1024, tn=1024, out_dtype=BF16)
    ret = retention(proj, retention_tables(), ret_norm[0], batch)
    w_ai = jnp.concatenate([lru_w_a[0], lru_w_i[0]], axis=-1).astype(BF16)
    b_ai = jnp.concatenate([lru_b_a[0].reshape(LRU_BLOCKS, LRU_BW),
                            lru_b_i[0].reshape(LRU_BLOCKS, LRU_BW)], axis=-1).reshape(1, 2 * LRU_W)
    lru = rglru(proj, lru_conv_w[0], lru_conv_b[0], w_ai, b_ai, lru_lambda[0], batch)
    xf = proj_residual([ret, lru], ev_w_out[0].astype(BF16), xf, tm=512)
    xf = ffn(xf, 0)

    proj = norm_matmul(xf, norm_mix[1], od_w_in[0].astype(BF16), tm=1024, tn=1024, out_dtype=BF16)
    att = moba(proj, q_norm[0], k_norm[0], rel_bias, batch)
    xf = proj_residual([att], od_w_out[0].astype(BF16), xf, tm=512)
    xf = ffn(xf, 1)
    return xf.reshape(batch, seq, d)
```

```python
import functools
import math

import jax
import jax.numpy as jnp
from jax import lax
from jax.experimental import pallas as pl
from jax.experimental.pallas import tpu as pltpu

D_MODEL = 2048
SEQ = 2048
RET_HEADS = 4
RET_DK = 256
RET_DV = 256
RET_W = RET_HEADS * RET_DK
RET_CHUNK = 128
ROPE_BASE = 10000.0
LRU_W = 1024
LRU_BLOCKS = 8
LRU_BW = 128
LRU_CONV = 4
LRU_C = 8.0
LRU_SCAN_CHUNK = 64
MOBA_HEADS = 16
MOBA_DH = 128
MOBA_BLOCK = 256
MOBA_TOPK = 3
REL_BUCKETS = 32
REL_MAX_DIST = 128
D_FF = 5632
FFN_CONV = 3
EPS = 1e-6

VMEM_LIMIT_BYTES = 56 * 1024 * 1024
SUBLANES = 8

BF16 = jnp.bfloat16
F32 = jnp.float32
NEG_INF = float("-inf")
MASK_VALUE = -1e30
LOG2E = math.log2(math.e)


def _params(*sem):
    return pltpu.CompilerParams(dimension_semantics=sem, vmem_limit_bytes=VMEM_LIMIT_BYTES)


def _gelu_tanh(x):
    c = math.sqrt(2.0 / math.pi)
    return 0.5 * x * (1.0 + jnp.tanh(c * (x + 0.044715 * (x * x * x))))


def _rms_scale(x):
    return lax.rsqrt(jnp.mean(x * x, axis=-1, keepdims=True) + EPS)


def _norm_mm_kernel(x_ref, g_ref, w_ref, o_ref, hn_ref):
    @pl.when(pl.program_id(1) == 0)
    def _():
        x = x_ref[...]
        hn_ref[...] = (x * _rms_scale(x) * g_ref[...]).astype(BF16)

    o_ref[...] = jnp.dot(hn_ref[...], w_ref[...].astype(BF16),
                         preferred_element_type=F32).astype(o_ref.dtype)


def norm_matmul(x, g, w, layer, *, tm, tn, out_dtype):
    t, d = x.shape
    n = w.shape[2]
    return pl.pallas_call(
        _norm_mm_kernel,
        grid=(t // tm, n // tn),
        in_specs=[
            pl.BlockSpec((tm, d), lambda i, j: (i, 0)),
            pl.BlockSpec((1, d), lambda i, j: (0, 0)),
            pl.BlockSpec((None, d, tn), lambda i, j: (layer, 0, j)),
        ],
        out_specs=pl.BlockSpec((tm, tn), lambda i, j: (i, j)),
        out_shape=jax.ShapeDtypeStruct((t, n), out_dtype),
        scratch_shapes=[pltpu.VMEM((tm, d), BF16)],
        compiler_params=_params("arbitrary", "arbitrary"),
        name="norm_matmul",
    )(x, g.reshape(1, d), w)


def _retention_kernel(q_ref, k_ref, v_ref, g_ref, cos_ref, sin_ref, decay_ref, zeta_ref, xi_ref,
                      gch_ref, rn_ref, o_ref, state_ref):
    state_ref[...] = jnp.zeros_like(state_ref)
    half = RET_DK // 2
    n_chunks = q_ref.shape[0] // RET_CHUNK

    def rotary(x, cos, sin):
        x1, x2 = x[:, :half], x[:, half:]
        return jnp.concatenate([x1 * cos - x2 * sin, x1 * sin + x2 * cos], axis=-1)

    def chunk(n, carry):
        rows = pl.ds(pl.multiple_of(n * RET_CHUNK, RET_CHUNK), RET_CHUNK)
        cos, sin = cos_ref[rows, :], sin_ref[rows, :]
        qr = rotary(q_ref[rows, :].astype(F32), cos, sin)
        kr = rotary(k_ref[rows, :].astype(F32), cos, sin) * (RET_DK ** -0.5)
        vb = v_ref[rows, :]
        qb = qr.astype(BF16)
        scores = lax.dot_general(qb, kr.astype(BF16), (((1,), (1,)), ((), ())),
                                 preferred_element_type=F32) * decay_ref[0]
        inner = jnp.dot(scores.astype(BF16), vb, preferred_element_type=F32)
        state = state_ref[...]
        cross = jnp.dot(qb, state.astype(BF16), preferred_element_type=F32) * xi_ref[0]
        kz = (kr * zeta_ref[0]).astype(BF16)
        kv = lax.dot_general(kz, vb, (((0,), (0,)), ((), ())), preferred_element_type=F32)
        state_ref[...] = gch_ref[0] * state + kv
        out = inner + cross
        out = out * _rms_scale(out) * rn_ref[0]
        g = g_ref[rows, :].astype(F32)
        o_ref[rows, :] = (out * (g * jax.nn.sigmoid(g))).astype(o_ref.dtype)
        return carry

    lax.fori_loop(0, n_chunks, chunk, 0)


def retention(proj, tables, ret_norm, batch):
    cos, sin, decay, zeta, xi, gch = tables
    t = proj.shape[0]
    hq = RET_W // RET_DK
    col = lambda off: pl.BlockSpec((SEQ, RET_DK), lambda b, h: (b, off + h))
    per_head = lambda shape: pl.BlockSpec((1,) + shape, lambda b, h: (h, 0, 0))
    full = pl.BlockSpec((SEQ, RET_DK // 2), lambda b, h: (0, 0))
    return pl.pallas_call(
        _retention_kernel,
        grid=(batch, RET_HEADS),
        in_specs=[col(0), col(hq), col(2 * hq), col(3 * hq), full, full,
                  per_head((RET_CHUNK, RET_CHUNK)), per_head((RET_CHUNK, RET_DK)),
                  per_head((RET_CHUNK, RET_DV)), per_head((1, RET_DV)), per_head((1, RET_DV))],
        out_specs=pl.BlockSpec((SEQ, RET_DV), lambda b, h: (b, h)),
        out_shape=jax.ShapeDtypeStruct((t, RET_HEADS * RET_DV), BF16),
        scratch_shapes=[pltpu.VMEM((RET_DK, RET_DV), F32)],
        compiler_params=_params("arbitrary", "arbitrary"),
        name="retention",
    )(proj, proj, proj, proj, cos, sin, decay, zeta, xi, gch,
      ret_norm.reshape(RET_HEADS, 1, RET_DV))


def retention_tables():
    half = RET_DK // 2
    pos = jnp.arange(SEQ)
    freqs = ROPE_BASE ** (-jnp.arange(half, dtype=F32) / half)
    ang = pos.astype(F32)[:, None] * freqs[None, :]
    cos, sin = jnp.cos(ang), jnp.sin(ang)
    log_g = jnp.log1p(-jnp.exp2(-5.0 - jnp.arange(RET_HEADS, dtype=F32)))
    i = jnp.arange(RET_CHUNK)
    diff = i[:, None] - i[None, :]
    decay = jnp.where(diff >= 0, jnp.exp(log_g[:, None, None] * jnp.maximum(diff, 0)), 0.0)
    zeta = jnp.exp(log_g[None, :] * (RET_CHUNK - 1 - i)[:, None])
    xi = jnp.exp(log_g[None, :] * (i + 1)[:, None])
    g_chunk = jnp.exp(log_g * RET_CHUNK)
    zeta_b = jnp.broadcast_to(zeta.T[:, :, None], (RET_HEADS, RET_CHUNK, RET_DK))
    xi_b = jnp.broadcast_to(xi.T[:, :, None], (RET_HEADS, RET_CHUNK, RET_DV))
    gch_b = jnp.broadcast_to(g_chunk[:, None, None], (RET_HEADS, 1, RET_DV))
    return cos, sin, decay, zeta_b, xi_b, gch_b


def _shift_rows(v, d, fill, row):
    if d % SUBLANES == 0:
        return jnp.concatenate([jnp.full((d, v.shape[1]), fill, v.dtype), v[:-d]], axis=0)
    return jnp.where(row >= d, pltpu.roll(v, d, axis=0), fill)


def _rglru_kernel(xl_ref, gl_ref, cw_ref, cb_ref, wai_ref, bai_ref, lam_ref, o_ref, a_ref, u_ref):
    s = xl_ref.shape[0]
    x = xl_ref[...].astype(F32)
    row = lax.broadcasted_iota(jnp.int32, x.shape, 0)
    xc = x * cw_ref[LRU_CONV - 1:LRU_CONV, :] + cb_ref[...]
    for d in range(1, LRU_CONV):
        xs = jnp.where(row >= d, pltpu.roll(x, d, axis=0), 0.0)
        xc = xc + xs * cw_ref[LRU_CONV - 1 - d:LRU_CONV - d, :]
    y = jnp.dot(xc.astype(BF16), wai_ref[0], preferred_element_type=F32) + bai_ref[...]
    r = jax.nn.sigmoid(y[:, :LRU_BW])
    gi = jax.nn.sigmoid(y[:, LRU_BW:])
    neg_lam = -lam_ref[...]
    softplus = jnp.maximum(neg_lam, 0.0) + jnp.log1p(jnp.exp(-jnp.abs(neg_lam)))
    log_a = (-LRU_C) * r * softplus
    a = jnp.exp(log_a)
    a_ref[...] = a
    u_ref[...] = jnp.sqrt(1.0 - a * a) * (gi * xc)

    c = LRU_SCAN_CHUNK
    crow = lax.broadcasted_iota(jnp.int32, (c, LRU_BW), 0)

    def chunk(n, h_prev):
        rows = pl.ds(pl.multiple_of(n * c, c), c)
        av, bv = a_ref[rows, :], u_ref[rows, :]
        d = 1
        while d < c:
            a_sh = _shift_rows(av, d, 1.0, crow)
            b_sh = _shift_rows(bv, d, 0.0, crow)
            bv = av * b_sh + bv
            av = av * a_sh
            d *= 2
        h = bv + av * h_prev
        o_ref[rows, :] = (h * _gelu_tanh(gl_ref[rows, :].astype(F32))).astype(o_ref.dtype)
        return h[c - 1:c, :]

    lax.fori_loop(0, s // c, chunk, jnp.zeros((1, LRU_BW), F32))


def rglru(proj, conv_w, conv_b, w_ai, b_ai, lam, batch):
    t = proj.shape[0]
    x_off = 4 * RET_W // LRU_BW
    g_off = x_off + LRU_W // LRU_BW
    vec = lambda rows: pl.BlockSpec((rows, LRU_BW), lambda b, g: (0, g))
    return pl.pallas_call(
        _rglru_kernel,
        grid=(batch, LRU_BLOCKS),
        in_specs=[
            pl.BlockSpec((SEQ, LRU_BW), lambda b, g: (b, x_off + g)),
            pl.BlockSpec((SEQ, LRU_BW), lambda b, g: (b, g_off + g)),
            vec(LRU_CONV), vec(1),
            pl.BlockSpec((1, LRU_BW, 2 * LRU_BW), lambda b, g: (g, 0, 0)),
            pl.BlockSpec((1, 2 * LRU_BW), lambda b, g: (0, g)),
            vec(1),
        ],
        out_specs=pl.BlockSpec((SEQ, LRU_BW), lambda b, g: (b, g)),
        out_shape=jax.ShapeDtypeStruct((t, LRU_W), BF16),
        scratch_shapes=[pltpu.VMEM((SEQ, LRU_BW), F32), pltpu.VMEM((SEQ, LRU_BW), F32)],
        compiler_params=_params("arbitrary", "arbitrary"),
        name="rglru",
    )(proj, proj, conv_w, conv_b.reshape(1, LRU_W), w_ai, b_ai, lam.reshape(1, LRU_W))


def _proj_res_kernel(*refs, splits, nk, kc):
    part_refs = refs[:len(splits)]
    w_ref, x_ref, g_ref, o_ref, hn_ref, wb_ref = refs[len(splits):]
    s = pl.program_id(0)

    @pl.when(s < nk)
    def _():
        wb_ref[pl.ds(pl.multiple_of(s * kc, kc), kc), :] = w_ref[...].astype(BF16)

    @pl.when(s >= nk)
    def _():
        acc = x_ref[...]
        off = 0
        for a_ref, k in zip(part_refs, splits):
            acc = acc + jnp.dot(a_ref[...], wb_ref[off:off + k, :], preferred_element_type=F32)
            off += k
        o_ref[...] = acc
        hn_ref[...] = (acc * _rms_scale(acc) * g_ref[...]).astype(BF16)


def proj_residual(parts, w, layer, x, g, *, tm, kc):
    t, n = x.shape
    k_total = w.shape[1]
    nk = k_total // kc
    splits = tuple(p.shape[1] for p in parts)
    row = lambda s: (jnp.maximum(s - nk, 0), 0)
    return pl.pallas_call(
        functools.partial(_proj_res_kernel, splits=splits, nk=nk, kc=kc),
        grid=(nk + t // tm,),
        in_specs=[pl.BlockSpec((tm, k), row) for k in splits] + [
            pl.BlockSpec((None, kc, n), lambda s: (layer, jnp.minimum(s, nk - 1), 0)),
            pl.BlockSpec((tm, n), row),
            pl.BlockSpec((1, n), lambda s: (0, 0)),
        ],
        out_specs=[pl.BlockSpec((tm, n), row), pl.BlockSpec((tm, n), row)],
        out_shape=[jax.ShapeDtypeStruct((t, n), F32), jax.ShapeDtypeStruct((t, n), BF16)],
        scratch_shapes=[pltpu.VMEM((k_total, n), BF16)],
        compiler_params=_params("arbitrary"),
        name="proj_residual",
    )(*parts, w, x, g.reshape(1, n))


def _ffn_up_kernel(hn_ref, wg_ref, wv_ref, cwg_ref, cwv_ref, cbg_ref, cbv_ref, o_ref,
                   wb_ref, u_ref, carry_ref, *, tiles_per_seq, sub):
    i = pl.program_id(1)
    tm = hn_ref.shape[0]
    tn = wg_ref.shape[1]

    @pl.when(i == 0)
    def _():
        wb_ref[:, :tn] = wg_ref[...].astype(BF16)
        wb_ref[:, tn:] = wv_ref[...].astype(BF16)

    def up(r):
        u_ref[r % 2, SUBLANES:, :] = jnp.dot(hn_ref[r * sub:(r + 1) * sub, :], wb_ref[...],
                                             preferred_element_type=F32)

    def conv(slot, cols, cw_ref, cb_ref):
        out = u_ref[slot, SUBLANES:, cols] * cw_ref[FFN_CONV - 1:FFN_CONV, :] + cb_ref[...]
        for d in range(1, FFN_CONV):
            prev = u_ref[slot, SUBLANES - d:SUBLANES - d + sub, cols]
            out = out + prev * cw_ref[FFN_CONV - 1 - d:FFN_CONV - d, :]
        return out

    seq_start = (i % tiles_per_seq) == 0
    u_ref[0, :SUBLANES, :] = jnp.where(seq_start, 0.0, carry_ref[...])
    up(0)
    for r in range(tm // sub):
        slot = r % 2
        u_ref[1 - slot, :SUBLANES, :] = u_ref[slot, sub:, :]
        if (r + 1) * sub < tm:
            up(r + 1)
        gate = conv(slot, slice(0, tn), cwg_ref, cbg_ref)
        val = conv(slot, slice(tn, 2 * tn), cwv_ref, cbv_ref)
        o_ref[r * sub:(r + 1) * sub, :] = (_gelu_tanh(gate) * val).astype(o_ref.dtype)
    carry_ref[...] = u_ref[(tm // sub) % 2, :SUBLANES, :]


def ffn_up(hn, w_up, conv_w, conv_b, layer, *, tm, tn, sub):
    t, d = hn.shape
    nj = D_FF // tn
    cb = conv_b.reshape(conv_b.shape[0], 1, 2 * D_FF)
    per_col = lambda rows, off: pl.BlockSpec((None, rows, tn), lambda j, i: (layer, 0, off + j))
    return pl.pallas_call(
        functools.partial(_ffn_up_kernel, tiles_per_seq=SEQ // tm, sub=sub),
        grid=(nj, t // tm),
        in_specs=[
            pl.BlockSpec((tm, d), lambda j, i: (i, 0)),
            per_col(d, 0), per_col(d, nj),
            per_col(FFN_CONV, 0), per_col(FFN_CONV, nj),
            per_col(1, 0), per_col(1, nj),
        ],
        out_specs=pl.BlockSpec((tm, tn), lambda j, i: (i, j)),
        out_shape=jax.ShapeDtypeStruct((t, D_FF), BF16),
        scratch_shapes=[pltpu.VMEM((d, 2 * tn), BF16),
                        pltpu.VMEM((2, SUBLANES + sub, 2 * tn), F32),
                        pltpu.VMEM((SUBLANES, 2 * tn), F32)],
        compiler_params=_params("arbitrary", "arbitrary"),
        name="ffn_up",
    )(hn, w_up, w_up, conv_w, conv_w, cb, cb)


def _mm_res_kernel(a_ref, w_ref, x_ref, o_ref, wb_ref):
    @pl.when(pl.program_id(1) == 0)
    def _():
        wb_ref[...] = w_ref[...].astype(BF16)

    o_ref[...] = x_ref[...] + jnp.dot(a_ref[...], wb_ref[...], preferred_element_type=F32)


def matmul_residual(a, w, layer, x, *, tm, tn):
    t, k = a.shape
    n = w.shape[2]
    return pl.pallas_call(
        _mm_res_kernel,
        grid=(n // tn, t // tm),
        in_specs=[
            pl.BlockSpec((tm, k), lambda j, i: (i, 0)),
            pl.BlockSpec((None, k, tn), lambda j, i: (layer, 0, j)),
            pl.BlockSpec((tm, tn), lambda j, i: (i, j)),
        ],
        out_specs=pl.BlockSpec((tm, tn), lambda j, i: (i, j)),
        out_shape=jax.ShapeDtypeStruct((t, n), F32),
        scratch_shapes=[pltpu.VMEM((k, tn), BF16)],
        compiler_params=_params("arbitrary", "arbitrary"),
        name="matmul_residual",
    )(a, w, x)


def _moba_kernel(relb_ref, bkt_ref, qg_ref, kg_ref, q_ref, k_ref, v_ref, o_ref,
                 bias_ref, qt_ref, ks_ref, vt_ref):
    h, b = pl.program_id(0), pl.program_id(1)
    nb = SEQ // MOBA_BLOCK
    blk = MOBA_BLOCK
    dh = MOBA_DH

    @pl.when((h == 0) & (b == 0))
    def _():
        key_blk = lax.broadcasted_iota(jnp.int32, (SEQ, dh), 0) // blk
        lane = lax.broadcasted_iota(jnp.int32, (SEQ, dh), 1)
        onehot = (lane < 2 * nb) & ((lane % nb) == key_blk)
        ks_ref[:, dh:] = jnp.where(onehot, 1.0, 0.0).astype(BF16)
        qt_ref[dh:, :] = jnp.zeros((dh, SEQ), BF16)
        vt_ref[dh:, :] = jnp.ones((vt_ref.shape[0] - dh, SEQ), BF16)

    @pl.when(b == 0)
    def _():
        for m in range(2):
            bk = bkt_ref[m]
            acc = jnp.full(bk.shape, NEG_INF, F32)
            for t in range(REL_BUCKETS):
                acc = jnp.where(bk == t, relb_ref[t, h] * LOG2E, acc)
            bias_ref[m] = acc

    far_bias = relb_ref[REL_BUCKETS - 1, h] * LOG2E

    q = q_ref[...].astype(F32)
    qt_ref[:dh, :] = (q * _rms_scale(q) * (qg_ref[...] * (dh ** -0.5 * LOG2E))).T.astype(BF16)
    k = k_ref[...].astype(F32)
    kn = k * _rms_scale(k) * kg_ref[...]
    ks_ref[:, :dh] = kn.astype(BF16)
    vt_ref[:dh, :] = v_ref[...].astype(F32).T.astype(BF16)

    kmean = jnp.concatenate(
        [jnp.mean(kn[n * blk:(n + 1) * blk, :], axis=0, keepdims=True) for n in range(nb)], axis=0)
    gate = jnp.dot(kmean.astype(BF16), qt_ref[:dh, :], preferred_element_type=F32)
    blk_id = lax.broadcasted_iota(jnp.int32, gate.shape, 0)
    q_blk = lax.broadcasted_iota(jnp.int32, gate.shape, 1) // blk
    valid = blk_id < q_blk
    gm = jnp.where(valid, gate, NEG_INF)
    cnt = jnp.zeros(gate.shape, F32)
    for s in range(1, nb):
        gs = pltpu.roll(gm, s, axis=0)
        tie = jnp.where(((blk_id - s) % nb) < blk_id, 1.0, 0.0)
        cnt = cnt + jnp.where(gs > gm, 1.0, jnp.where(gs == gm, tie, 0.0))
    picked = jnp.where(blk_id < q_blk - 1, far_bias, 0.0)
    row = jnp.where(valid, jnp.where(cnt < MOBA_TOPK, picked, MASK_VALUE), 0.0)
    row_hi = row.astype(BF16).astype(F32)
    row_lo = jnp.where(row > 0.5 * MASK_VALUE, row - row_hi, 0.0)
    qt_ref[dh:dh + 2 * nb, :] = jnp.concatenate([row_hi, row_lo], axis=0).astype(BF16)

    def scores(i):
        return jnp.dot(ks_ref[:(i + 1) * blk, :], qt_ref[:, i * blk:(i + 1) * blk],
                       preferred_element_type=F32)

    def softmax(i, s_all):
        tiles = []
        for jb in range(i + 1):
            s_t = s_all[jb * blk:(jb + 1) * blk, :]
            if jb >= i - 1:
                s_t = s_t + bias_ref[i - jb]
            tiles.append(s_t)
        m = functools.reduce(jnp.maximum, [jnp.max(t, axis=0, keepdims=True) for t in tiles])
        return jnp.concatenate([jnp.exp2(t - m).astype(BF16) for t in tiles], axis=0)

    def attend(i, p_all):
        acc = jnp.dot(vt_ref[:, :(i + 1) * blk], p_all, preferred_element_type=F32)
        out = acc[:dh, :] / acc[dh:dh + 1, :]
        o_ref[i * blk:(i + 1) * blk, :] = out.T.astype(o_ref.dtype)

    s_next = scores(0)
    pending = None
    for i in range(nb):
        s_all = s_next
        if i + 1 < nb:
            s_next = scores(i + 1)
        p_all = softmax(i, s_all)
        if pending is not None:
            attend(*pending)
        pending = (i, p_all)
    attend(*pending)


def moba_bucket_tables():
    r = jnp.arange(MOBA_BLOCK)
    dist_own = r[None, :] - r[:, None]

    def bucket(dist):
        n = jnp.maximum(dist, 0)
        max_exact = REL_BUCKETS // 2
        nf = jnp.maximum(n, 1).astype(F32)
        large = max_exact + (jnp.log(nf / max_exact) / math.log(REL_MAX_DIST / max_exact)
                             * (REL_BUCKETS - max_exact)).astype(jnp.int32)
        large = jnp.minimum(large, REL_BUCKETS - 1)
        return jnp.where(n < max_exact, n, large)

    own = jnp.where(dist_own >= 0, bucket(dist_own), -1)
    adj = bucket(dist_own + MOBA_BLOCK)
    return jnp.stack([own, adj]).astype(jnp.int32)


def moba(proj, q_norm, k_norm, rel_bias, batch):
    t = proj.shape[0]
    col = lambda off: pl.BlockSpec((SEQ, MOBA_DH), lambda h, b: (b, off + h))
    return pl.pallas_call(
        _moba_kernel,
        grid=(MOBA_HEADS, batch),
        in_specs=[
            pl.BlockSpec(memory_space=pltpu.SMEM),
            pl.BlockSpec((2, MOBA_BLOCK, MOBA_BLOCK), lambda h, b: (0, 0, 0)),
            pl.BlockSpec((1, MOBA_DH), lambda h, b: (0, 0)),
            pl.BlockSpec((1, MOBA_DH), lambda h, b: (0, 0)),
            col(0), col(MOBA_HEADS), col(2 * MOBA_HEADS),
        ],
        out_specs=pl.BlockSpec((SEQ, MOBA_DH), lambda h, b: (b, h)),
        out_shape=jax.ShapeDtypeStruct((t, MOBA_HEADS * MOBA_DH), BF16),
        scratch_shapes=[pltpu.VMEM((2, MOBA_BLOCK, MOBA_BLOCK), F32),
                        pltpu.VMEM((2 * MOBA_DH, SEQ), BF16),
                        pltpu.VMEM((SEQ, 2 * MOBA_DH), BF16),
                        pltpu.VMEM((MOBA_DH + 2 * SUBLANES, SEQ), BF16)],
        compiler_params=_params("arbitrary", "arbitrary"),
        name="moba",
    )(rel_bias, moba_bucket_tables(), q_norm.reshape(1, MOBA_DH), k_norm.reshape(1, MOBA_DH),
      proj, proj, proj)


def kernel(x, norm_mix, norm_ffn, ev_w_in, ret_norm, lru_conv_w, lru_conv_b, lru_w_a, lru_b_a,
           lru_w_i, lru_b_i, lru_lambda, ev_w_out, od_w_in, q_norm, k_norm, od_w_out, rel_bias,
           ffn_w_up, ffn_conv_w, ffn_conv_b, ffn_w_down):
    batch, seq, d = x.shape
    assert (seq, d) == (SEQ, D_MODEL)
    xf = x.reshape(batch * seq, d)

    def ffn(xf, hn, l):
        act = ffn_up(hn, ffn_w_up, ffn_conv_w, ffn_conv_b, l, tm=2048, tn=512, sub=256)
        return matmul_residual(act, ffn_w_down, l, xf, tm=512, tn=512)

    proj = norm_matmul(xf, norm_mix[0], ev_w_in, 0, tm=1024, tn=1024, out_dtype=BF16)
    ret = retention(proj, retention_tables(), ret_norm[0], batch)
    w_ai = jnp.concatenate([lru_w_a[0], lru_w_i[0]], axis=-1).astype(BF16)
    b_ai = jnp.concatenate([lru_b_a[0].reshape(LRU_BLOCKS, LRU_BW),
                            lru_b_i[0].reshape(LRU_BLOCKS, LRU_BW)], axis=-1).reshape(1, 2 * LRU_W)
    lru = rglru(proj, lru_conv_w[0], lru_conv_b[0], w_ai, b_ai, lru_lambda[0], batch)
    xf, hn = proj_residual([ret, lru], ev_w_out, 0, xf, norm_ffn[0], tm=512, kc=512)
    xf = ffn(xf, hn, 0)

    proj = norm_matmul(xf, norm_mix[1], od_w_in, 0, tm=1024, tn=1024, out_dtype=BF16)
    att = moba(proj, q_norm[0], k_norm[0], rel_bias, batch)
    xf, hn = proj_residual([att], od_w_out, 0, xf, norm_ffn[1], tm=512, kc=512)
    xf = ffn(xf, hn, 1)
    return xf.reshape(batch, seq, d)
```

```python
import functools
import math

import jax
import jax.numpy as jnp
from jax import lax
from jax.experimental import pallas as pl
from jax.experimental.pallas import tpu as pltpu

D_MODEL = 2048
SEQ = 2048
RET_HEADS = 4
RET_DK = 256
RET_DV = 256
RET_W = RET_HEADS * RET_DK
RET_CHUNK = 128
ROPE_BASE = 10000.0
LRU_W = 1024
LRU_BLOCKS = 8
LRU_BW = 128
LRU_CONV = 4
LRU_C = 8.0
LRU_SCAN_CHUNK = 64
MOBA_HEADS = 16
MOBA_DH = 128
MOBA_BLOCK = 256
MOBA_TOPK = 3
REL_BUCKETS = 32
REL_MAX_DIST = 128
D_FF = 5632
FFN_CONV = 3
EPS = 1e-6

VMEM_LIMIT_BYTES = 56 * 1024 * 1024
SUBLANES = 8

BF16 = jnp.bfloat16
F32 = jnp.float32
NEG_INF = float("-inf")
MASK_VALUE = -1e30
LOG2E = math.log2(math.e)


def _params(*sem):
    return pltpu.CompilerParams(dimension_semantics=sem, vmem_limit_bytes=VMEM_LIMIT_BYTES)


def _gelu_tanh(x):
    c = math.sqrt(2.0 / math.pi)
    return 0.5 * x * (1.0 + jnp.tanh(c * (x + 0.044715 * (x * x * x))))


def _rms_scale(x):
    return lax.rsqrt(jnp.mean(x * x, axis=-1, keepdims=True) + EPS)


def _norm_mm_kernel(x_ref, g_ref, w_ref, hg_ref, o_ref, hn_ref, *scratch,
                    head_norm_tiles, head_dim, sub):
    if head_norm_tiles:
        wb_ref, acc_ref = scratch
    j = pl.program_id(1)

    @pl.when(j == 0)
    def _():
        x = x_ref[...]
        hn_ref[...] = (x * _rms_scale(x) * g_ref[...]).astype(BF16)

    def plain():
        o_ref[...] = jnp.dot(hn_ref[...], w_ref[...].astype(BF16),
                             preferred_element_type=F32).astype(o_ref.dtype)

    def head_normed():
        tm, tn = o_ref.shape
        wb_ref[...] = w_ref[...].astype(BF16)

        def mm(r):
            acc_ref[r % 2] = jnp.dot(hn_ref[r * sub:(r + 1) * sub, :], wb_ref[...],
                                     preferred_element_type=F32)

        mm(0)
        for r in range(tm // sub):
            if (r + 1) * sub < tm:
                mm(r + 1)
            heads = [acc_ref[r % 2, :, s:s + head_dim] for s in range(0, tn, head_dim)]
            normed = jnp.concatenate([a * _rms_scale(a) for a in heads], axis=-1)
            o_ref[r * sub:(r + 1) * sub, :] = (normed * hg_ref[...]).astype(o_ref.dtype)

    if head_norm_tiles:
        pl.when(j < head_norm_tiles)(head_normed)
        pl.when(j >= head_norm_tiles)(plain)
    else:
        plain()


def norm_matmul(x, g, w, layer, *, tm, tn, out_dtype, head_gain=None, head_dim=None, sub=256):
    t, d = x.shape
    n = w.shape[2]
    if head_gain is None:
        head_norm_tiles, hg = 0, jnp.ones((1, n), F32)
    else:
        head_norm_tiles = head_gain.shape[0] // tn
        assert head_norm_tiles * tn == head_gain.shape[0] and tn % head_dim == 0
        hg = jnp.concatenate([head_gain, jnp.ones((n - head_gain.shape[0],), F32)]).reshape(1, n)
    scratch = [pltpu.VMEM((tm, d), BF16)]
    if head_norm_tiles:
        scratch += [pltpu.VMEM((d, tn), BF16), pltpu.VMEM((2, sub, tn), F32)]
    return pl.pallas_call(
        functools.partial(_norm_mm_kernel, head_norm_tiles=head_norm_tiles, head_dim=head_dim,
                          sub=sub),
        grid=(t // tm, n // tn),
        in_specs=[
            pl.BlockSpec((tm, d), lambda i, j: (i, 0)),
            pl.BlockSpec((1, d), lambda i, j: (0, 0)),
            pl.BlockSpec((None, d, tn), lambda i, j: (layer, 0, j)),
            pl.BlockSpec((1, tn), lambda i, j: (0, j)),
        ],
        out_specs=pl.BlockSpec((tm, tn), lambda i, j: (i, j)),
        out_shape=jax.ShapeDtypeStruct((t, n), out_dtype),
        scratch_shapes=scratch,
        compiler_params=_params("arbitrary", "arbitrary"),
        name="norm_matmul",
    )(x, g.reshape(1, d), w, hg)


def _retention_kernel(q_ref, k_ref, v_ref, g_ref, cos_ref, sin_ref, decay_ref, zeta_ref, xi_ref,
                      gch_ref, rn_ref, o_ref, qb_ref, kb_ref, kv_ref, st_ref):
    half = RET_DK // 2
    n_chunks = q_ref.shape[0] // RET_CHUNK
    chunk_rows = lambda n: slice(n * RET_CHUNK, (n + 1) * RET_CHUNK)

    def rotary(x, cos, sin):
        x1, x2 = x[:, :half], x[:, half:]
        return jnp.concatenate([x1 * cos - x2 * sin, x1 * sin + x2 * cos], axis=-1)

    for n in range(n_chunks):
        rows = chunk_rows(n)
        cos, sin = cos_ref[rows, :], sin_ref[rows, :]
        qb_ref[rows, :] = rotary(q_ref[rows, :].astype(F32), cos, sin).astype(BF16)
        kr = rotary(k_ref[rows, :].astype(F32), cos, sin) * (RET_DK ** -0.5)
        kb_ref[rows, :] = kr.astype(BF16)
        kz = (kr * zeta_ref[0]).astype(BF16)
        kv_ref[n] = lax.dot_general(kz, v_ref[rows, :], (((0,), (0,)), ((), ())),
                                    preferred_element_type=F32)

    state = jnp.zeros((RET_DK, RET_DV), F32)
    for n in range(n_chunks):
        st_ref[n] = state.astype(BF16)
        state = gch_ref[0] * state + kv_ref[n]

    for n in range(n_chunks):
        rows = chunk_rows(n)
        qb, vb = qb_ref[rows, :], v_ref[rows, :]
        scores = lax.dot_general(qb, kb_ref[rows, :], (((1,), (1,)), ((), ())),
                                 preferred_element_type=F32) * decay_ref[0]
        inner = jnp.dot(scores.astype(BF16), vb, preferred_element_type=F32)
        cross = jnp.dot(qb, st_ref[n], preferred_element_type=F32) * xi_ref[0]
        out = inner + cross
        out = out * _rms_scale(out) * rn_ref[0]
        g = g_ref[rows, :].astype(F32)
        o_ref[rows, :] = (out * (g * jax.nn.sigmoid(g))).astype(o_ref.dtype)


def retention(proj, tables, ret_norm, batch):
    cos, sin, decay, zeta, xi, gch = tables
    t = proj.shape[0]
    hq = RET_W // RET_DK
    col = lambda off: pl.BlockSpec((SEQ, RET_DK), lambda b, h: (b, off + h))
    per_head = lambda shape: pl.BlockSpec((1,) + shape, lambda b, h: (h, 0, 0))
    full = pl.BlockSpec((SEQ, RET_DK // 2), lambda b, h: (0, 0))
    return pl.pallas_call(
        _retention_kernel,
        grid=(batch, RET_HEADS),
        in_specs=[col(0), col(hq), col(2 * hq), col(3 * hq), full, full,
                  per_head((RET_CHUNK, RET_CHUNK)), per_head((RET_CHUNK, RET_DK)),
                  per_head((RET_CHUNK, RET_DV)), per_head((1, RET_DV)), per_head((1, RET_DV))],
        out_specs=pl.BlockSpec((SEQ, RET_DV), lambda b, h: (b, h)),
        out_shape=jax.ShapeDtypeStruct((t, RET_HEADS * RET_DV), BF16),
        scratch_shapes=[pltpu.VMEM((SEQ, RET_DK), BF16), pltpu.VMEM((SEQ, RET_DK), BF16),
                        pltpu.VMEM((SEQ // RET_CHUNK, RET_DK, RET_DV), F32),
                        pltpu.VMEM((SEQ // RET_CHUNK, RET_DK, RET_DV), BF16)],
        compiler_params=_params("arbitrary", "arbitrary"),
        name="retention",
    )(proj, proj, proj, proj, cos, sin, decay, zeta, xi, gch,
      ret_norm.reshape(RET_HEADS, 1, RET_DV))


def retention_tables():
    half = RET_DK // 2
    pos = jnp.arange(SEQ)
    freqs = ROPE_BASE ** (-jnp.arange(half, dtype=F32) / half)
    ang = pos.astype(F32)[:, None] * freqs[None, :]
    cos, sin = jnp.cos(ang), jnp.sin(ang)
    log_g = jnp.log1p(-jnp.exp2(-5.0 - jnp.arange(RET_HEADS, dtype=F32)))
    i = jnp.arange(RET_CHUNK)
    diff = i[:, None] - i[None, :]
    decay = jnp.where(diff >= 0, jnp.exp(log_g[:, None, None] * jnp.maximum(diff, 0)), 0.0)
    zeta = jnp.exp(log_g[None, :] * (RET_CHUNK - 1 - i)[:, None])
    xi = jnp.exp(log_g[None, :] * (i + 1)[:, None])
    g_chunk = jnp.exp(log_g * RET_CHUNK)
    zeta_b = jnp.broadcast_to(zeta.T[:, :, None], (RET_HEADS, RET_CHUNK, RET_DK))
    xi_b = jnp.broadcast_to(xi.T[:, :, None], (RET_HEADS, RET_CHUNK, RET_DV))
    gch_b = jnp.broadcast_to(g_chunk[:, None, None], (RET_HEADS, 1, RET_DV))
    return cos, sin, decay, zeta_b, xi_b, gch_b


def _shift_rows(v, d, fill, row):
    if d % SUBLANES == 0:
        return jnp.concatenate([jnp.full((d, v.shape[1]), fill, v.dtype), v[:-d]], axis=0)
    return jnp.where(row >= d, pltpu.roll(v, d, axis=0), fill)


def _rglru_kernel(xl_ref, gl_ref, cw_ref, cb_ref, wai_ref, bai_ref, lam_ref, o_ref, a_ref, u_ref):
    s = xl_ref.shape[0]
    x = xl_ref[...].astype(F32)
    row = lax.broadcasted_iota(jnp.int32, x.shape, 0)
    xc = x * cw_ref[LRU_CONV - 1:LRU_CONV, :] + cb_ref[...]
    for d in range(1, LRU_CONV):
        xs = jnp.where(row >= d, pltpu.roll(x, d, axis=0), 0.0)
        xc = xc + xs * cw_ref[LRU_CONV - 1 - d:LRU_CONV - d, :]
    y = jnp.dot(xc.astype(BF16), wai_ref[0], preferred_element_type=F32) + bai_ref[...]
    r = jax.nn.sigmoid(y[:, :LRU_BW])
    gi = jax.nn.sigmoid(y[:, LRU_BW:])
    neg_lam = -lam_ref[...]
    softplus = jnp.maximum(neg_lam, 0.0) + jnp.log1p(jnp.exp(-jnp.abs(neg_lam)))
    log_a = (-LRU_C) * r * softplus
    a = jnp.exp(log_a)
    a_ref[...] = a
    u_ref[...] = jnp.sqrt(1.0 - a * a) * (gi * xc)

    c = LRU_SCAN_CHUNK
    crow = lax.broadcasted_iota(jnp.int32, (c, LRU_BW), 0)

    def chunk(n, h_prev):
        rows = pl.ds(pl.multiple_of(n * c, c), c)
        av, bv = a_ref[rows, :], u_ref[rows, :]
        d = 1
        while d < c:
            a_sh = _shift_rows(av, d, 1.0, crow)
            b_sh = _shift_rows(bv, d, 0.0, crow)
            bv = av * b_sh + bv
            av = av * a_sh
            d *= 2
        h = bv + av * h_prev
        o_ref[rows, :] = (h * _gelu_tanh(gl_ref[rows, :].astype(F32))).astype(o_ref.dtype)
        return h[c - 1:c, :]

    lax.fori_loop(0, s // c, chunk, jnp.zeros((1, LRU_BW), F32))


def rglru(proj, conv_w, conv_b, w_ai, b_ai, lam, batch):
    t = proj.shape[0]
    x_off = 4 * RET_W // LRU_BW
    g_off = x_off + LRU_W // LRU_BW
    vec = lambda rows: pl.BlockSpec((rows, LRU_BW), lambda b, g: (0, g))
    return pl.pallas_call(
        _rglru_kernel,
        grid=(batch, LRU_BLOCKS),
        in_specs=[
            pl.BlockSpec((SEQ, LRU_BW), lambda b, g: (b, x_off + g)),
            pl.BlockSpec((SEQ, LRU_BW), lambda b, g: (b, g_off + g)),
            vec(LRU_CONV), vec(1),
            pl.BlockSpec((1, LRU_BW, 2 * LRU_BW), lambda b, g: (g, 0, 0)),
            pl.BlockSpec((1, 2 * LRU_BW), lambda b, g: (0, g)),
            vec(1),
        ],
        out_specs=pl.BlockSpec((SEQ, LRU_BW), lambda b, g: (b, g)),
        out_shape=jax.ShapeDtypeStruct((t, LRU_W), BF16),
        scratch_shapes=[pltpu.VMEM((SEQ, LRU_BW), F32), pltpu.VMEM((SEQ, LRU_BW), F32)],
        compiler_params=_params("arbitrary", "arbitrary"),
        name="rglru",
    )(proj, proj, conv_w, conv_b.reshape(1, LRU_W), w_ai, b_ai, lam.reshape(1, LRU_W))


def _proj_res_kernel(*refs, splits, nk, kc):
    part_refs = refs[:len(splits)]
    w_ref, x_ref, g_ref, o_ref, hn_ref, wb_ref = refs[len(splits):]
    s = pl.program_id(0)

    @pl.when(s < nk)
    def _():
        wb_ref[pl.ds(pl.multiple_of(s * kc, kc), kc), :] = w_ref[...].astype(BF16)

    @pl.when(s >= nk)
    def _():
        acc = x_ref[...]
        off = 0
        for a_ref, k in zip(part_refs, splits):
            acc = acc + jnp.dot(a_ref[...], wb_ref[off:off + k, :], preferred_element_type=F32)
            off += k
        o_ref[...] = acc
        hn_ref[...] = (acc * _rms_scale(acc) * g_ref[...]).astype(BF16)


def proj_residual(parts, w, layer, x, g, *, tm, kc):
    t, n = x.shape
    k_total = w.shape[1]
    nk = k_total // kc
    splits = tuple(p.shape[1] for p in parts)
    row = lambda s: (jnp.maximum(s - nk, 0), 0)
    return pl.pallas_call(
        functools.partial(_proj_res_kernel, splits=splits, nk=nk, kc=kc),
        grid=(nk + t // tm,),
        in_specs=[pl.BlockSpec((tm, k), row) for k in splits] + [
            pl.BlockSpec((None, kc, n), lambda s: (layer, jnp.minimum(s, nk - 1), 0)),
            pl.BlockSpec((tm, n), row),
            pl.BlockSpec((1, n), lambda s: (0, 0)),
        ],
        out_specs=[pl.BlockSpec((tm, n), row), pl.BlockSpec((tm, n), row)],
        out_shape=[jax.ShapeDtypeStruct((t, n), F32), jax.ShapeDtypeStruct((t, n), BF16)],
        scratch_shapes=[pltpu.VMEM((k_total, n), BF16)],
        compiler_params=_params("arbitrary"),
        name="proj_residual",
    )(*parts, w, x, g.reshape(1, n))


def _ffn_up_kernel(hn_ref, wg_ref, wv_ref, cwg_ref, cwv_ref, cbg_ref, cbv_ref, o_ref,
                   wb_ref, u_ref, carry_ref, *, tiles_per_seq, sub):
    i = pl.program_id(1)
    tm = hn_ref.shape[0]
    tn = wg_ref.shape[1]

    @pl.when(i == 0)
    def _():
        wb_ref[:, :tn] = wg_ref[...].astype(BF16)
        wb_ref[:, tn:] = wv_ref[...].astype(BF16)

    def up(r):
        u_ref[r % 2, SUBLANES:, :] = jnp.dot(hn_ref[r * sub:(r + 1) * sub, :], wb_ref[...],
                                             preferred_element_type=F32)

    def conv(slot, cols, cw_ref, cb_ref):
        out = u_ref[slot, SUBLANES:, cols] * cw_ref[FFN_CONV - 1:FFN_CONV, :] + cb_ref[...]
        for d in range(1, FFN_CONV):
            prev = u_ref[slot, SUBLANES - d:SUBLANES - d + sub, cols]
            out = out + prev * cw_ref[FFN_CONV - 1 - d:FFN_CONV - d, :]
        return out

    seq_start = (i % tiles_per_seq) == 0
    u_ref[0, :SUBLANES, :] = jnp.where(seq_start, 0.0, carry_ref[...])
    up(0)
    for r in range(tm // sub):
        slot = r % 2
        u_ref[1 - slot, :SUBLANES, :] = u_ref[slot, sub:, :]
        if (r + 1) * sub < tm:
            up(r + 1)
        gate = conv(slot, slice(0, tn), cwg_ref, cbg_ref)
        val = conv(slot, slice(tn, 2 * tn), cwv_ref, cbv_ref)
        o_ref[r * sub:(r + 1) * sub, :] = (_gelu_tanh(gate) * val).astype(o_ref.dtype)
    carry_ref[...] = u_ref[(tm // sub) % 2, :SUBLANES, :]


def ffn_up(hn, w_up, conv_w, conv_b, layer, *, tm, tn, sub):
    t, d = hn.shape
    nj = D_FF // tn
    cb = conv_b.reshape(conv_b.shape[0], 1, 2 * D_FF)
    per_col = lambda rows, off: pl.BlockSpec((None, rows, tn), lambda j, i: (layer, 0, off + j))
    return pl.pallas_call(
        functools.partial(_ffn_up_kernel, tiles_per_seq=SEQ // tm, sub=sub),
        grid=(nj, t // tm),
        in_specs=[
            pl.BlockSpec((tm, d), lambda j, i: (i, 0)),
            per_col(d, 0), per_col(d, nj),
            per_col(FFN_CONV, 0), per_col(FFN_CONV, nj),
            per_col(1, 0), per_col(1, nj),
        ],
        out_specs=pl.BlockSpec((tm, tn), lambda j, i: (i, j)),
        out_shape=jax.ShapeDtypeStruct((t, D_FF), BF16),
        scratch_shapes=[pltpu.VMEM((d, 2 * tn), BF16),
                        pltpu.VMEM((2, SUBLANES + sub, 2 * tn), F32),
                        pltpu.VMEM((SUBLANES, 2 * tn), F32)],
        compiler_params=_params("arbitrary", "arbitrary"),
        name="ffn_up",
    )(hn, w_up, w_up, conv_w, conv_w, cb, cb)


def _mm_res_kernel(a_ref, w_ref, x_ref, o_ref, wb_ref):
    @pl.when(pl.program_id(1) == 0)
    def _():
        wb_ref[...] = w_ref[...].astype(BF16)

    o_ref[...] = x_ref[...] + jnp.dot(a_ref[...], wb_ref[...], preferred_element_type=F32)


def matmul_residual(a, w, layer, x, *, tm, tn):
    t, k = a.shape
    n = w.shape[2]
    return pl.pallas_call(
        _mm_res_kernel,
        grid=(n // tn, t // tm),
        in_specs=[
            pl.BlockSpec((tm, k), lambda j, i: (i, 0)),
            pl.BlockSpec((None, k, tn), lambda j, i: (layer, 0, j)),
            pl.BlockSpec((tm, tn), lambda j, i: (i, j)),
        ],
        out_specs=pl.BlockSpec((tm, tn), lambda j, i: (i, j)),
        out_shape=jax.ShapeDtypeStruct((t, n), F32),
        scratch_shapes=[pltpu.VMEM((k, tn), BF16)],
        compiler_params=_params("arbitrary", "arbitrary"),
        name="matmul_residual",
    )(a, w, x)


def _moba_kernel(relb_ref, bkt_ref, q_ref, k_ref, v_ref, o_ref,
                 bias_ref, qt_ref, ks_ref, vt_ref):
    h, b = pl.program_id(0), pl.program_id(1)
    nb = SEQ // MOBA_BLOCK
    blk = MOBA_BLOCK
    dh = MOBA_DH

    @pl.when((h == 0) & (b == 0))
    def _():
        key_blk = lax.broadcasted_iota(jnp.int32, (SEQ, dh), 0) // blk
        lane = lax.broadcasted_iota(jnp.int32, (SEQ, dh), 1)
        onehot = (lane < 2 * nb) & ((lane % nb) == key_blk)
        ks_ref[:, dh:] = jnp.where(onehot, 1.0, 0.0).astype(BF16)
        qt_ref[dh:, :] = jnp.zeros((dh, SEQ), BF16)
        vt_ref[dh:, :] = jnp.ones((vt_ref.shape[0] - dh, SEQ), BF16)

    @pl.when(b == 0)
    def _():
        for m in range(2):
            bk = bkt_ref[m]
            acc = jnp.full(bk.shape, NEG_INF, F32)
            for t in range(REL_BUCKETS):
                acc = jnp.where(bk == t, relb_ref[t, h] * LOG2E, acc)
            bias_ref[m] = acc

    far_bias = relb_ref[REL_BUCKETS - 1, h] * LOG2E

    qt_ref[:dh, :] = q_ref[...].astype(F32).T.astype(BF16)
    ks_ref[:, :dh] = k_ref[...]
    vt_ref[:dh, :] = v_ref[...].astype(F32).T.astype(BF16)

    blk_id = lax.broadcasted_iota(jnp.int32, (nb, SEQ), 0)
    q_blk = lax.broadcasted_iota(jnp.int32, (nb, SEQ), 1) // blk
    averager = jnp.where(blk_id == q_blk, 1.0 / blk, 0.0).astype(BF16)
    kmean = jnp.dot(averager, k_ref[...], preferred_element_type=F32)
    gate = jnp.dot(kmean.astype(BF16), qt_ref[:dh, :], preferred_element_type=F32)
    valid = blk_id < q_blk
    gm = jnp.where(valid, gate, NEG_INF)
    cnt = jnp.zeros(gate.shape, F32)
    for s in range(1, nb):
        gs = pltpu.roll(gm, s, axis=0)
        tie = jnp.where(((blk_id - s) % nb) < blk_id, 1.0, 0.0)
        cnt = cnt + jnp.where(gs > gm, 1.0, jnp.where(gs == gm, tie, 0.0))
    picked = jnp.where(blk_id < q_blk - 1, far_bias, 0.0)
    row = jnp.where(valid, jnp.where(cnt < MOBA_TOPK, picked, MASK_VALUE), 0.0)
    row_hi = row.astype(BF16).astype(F32)
    row_lo = jnp.where(row > 0.5 * MASK_VALUE, row - row_hi, 0.0)
    qt_ref[dh:dh + 2 * nb, :] = jnp.concatenate([row_hi, row_lo], axis=0).astype(BF16)

    def scores(i):
        return jnp.dot(ks_ref[:(i + 1) * blk, :], qt_ref[:, i * blk:(i + 1) * blk],
                       preferred_element_type=F32)

    def softmax(i, s_all):
        tiles = []
        for jb in range(i + 1):
            s_t = s_all[jb * blk:(jb + 1) * blk, :]
            if jb >= i - 1:
                s_t = s_t + bias_ref[i - jb]
            tiles.append(s_t)
        m = functools.reduce(jnp.maximum, [jnp.max(t, axis=0, keepdims=True) for t in tiles])
        return jnp.concatenate([jnp.exp2(t - m).astype(BF16) for t in tiles], axis=0)

    def attend(i, p_all):
        acc = jnp.dot(vt_ref[:, :(i + 1) * blk], p_all, preferred_element_type=F32)
        out = acc[:dh, :] / acc[dh:dh + 1, :]
        o_ref[i * blk:(i + 1) * blk, :] = out.T.astype(o_ref.dtype)

    s_next = scores(0)
    pending = None
    for i in range(nb):
        s_all = s_next
        if i + 1 < nb:
            s_next = scores(i + 1)
        p_all = softmax(i, s_all)
        if pending is not None:
            attend(*pending)
        pending = (i, p_all)
    attend(*pending)


def moba_bucket_tables():
    r = jnp.arange(MOBA_BLOCK)
    dist_own = r[None, :] - r[:, None]

    def bucket(dist):
        n = jnp.maximum(dist, 0)
        max_exact = REL_BUCKETS // 2
        nf = jnp.maximum(n, 1).astype(F32)
        large = max_exact + (jnp.log(nf / max_exact) / math.log(REL_MAX_DIST / max_exact)
                             * (REL_BUCKETS - max_exact)).astype(jnp.int32)
        large = jnp.minimum(large, REL_BUCKETS - 1)
        return jnp.where(n < max_exact, n, large)

    own = jnp.where(dist_own >= 0, bucket(dist_own), -1)
    adj = bucket(dist_own + MOBA_BLOCK)
    return jnp.stack([own, adj]).astype(jnp.int32)


def moba(proj, rel_bias, batch):
    t = proj.shape[0]
    col = lambda off: pl.BlockSpec((SEQ, MOBA_DH), lambda h, b: (b, off + h))
    return pl.pallas_call(
        _moba_kernel,
        grid=(MOBA_HEADS, batch),
        in_specs=[
            pl.BlockSpec(memory_space=pltpu.SMEM),
            pl.BlockSpec((2, MOBA_BLOCK, MOBA_BLOCK), lambda h, b: (0, 0, 0)),
            col(0), col(MOBA_HEADS), col(2 * MOBA_HEADS),
        ],
        out_specs=pl.BlockSpec((SEQ, MOBA_DH), lambda h, b: (b, h)),
        out_shape=jax.ShapeDtypeStruct((t, MOBA_HEADS * MOBA_DH), BF16),
        scratch_shapes=[pltpu.VMEM((2, MOBA_BLOCK, MOBA_BLOCK), F32),
                        pltpu.VMEM((2 * MOBA_DH, SEQ), BF16),
                        pltpu.VMEM((SEQ, 2 * MOBA_DH), BF16),
                        pltpu.VMEM((MOBA_DH + 2 * SUBLANES, SEQ), BF16)],
        compiler_params=_params("arbitrary", "arbitrary"),
        name="moba",
    )(rel_bias, moba_bucket_tables(), proj, proj, proj)


def moba_head_gain(q_norm, k_norm):
    qg = jnp.tile(q_norm * (MOBA_DH ** -0.5 * LOG2E), MOBA_HEADS)
    return jnp.concatenate([qg, jnp.tile(k_norm, MOBA_HEADS)])


def kernel(x, norm_mix, norm_ffn, ev_w_in, ret_norm, lru_conv_w, lru_conv_b, lru_w_a, lru_b_a,
           lru_w_i, lru_b_i, lru_lambda, ev_w_out, od_w_in, q_norm, k_norm, od_w_out, rel_bias,
           ffn_w_up, ffn_conv_w, ffn_conv_b, ffn_w_down):
    batch, seq, d = x.shape
    assert (seq, d) == (SEQ, D_MODEL)
    xf = x.reshape(batch * seq, d)

    def ffn(xf, hn, l):
        act = ffn_up(hn, ffn_w_up, ffn_conv_w, ffn_conv_b, l, tm=2048, tn=512, sub=256)
        return matmul_residual(act, ffn_w_down, l, xf, tm=512, tn=512)

    proj = norm_matmul(xf, norm_mix[0], ev_w_in, 0, tm=1024, tn=1024, out_dtype=BF16)
    ret = retention(proj, retention_tables(), ret_norm[0], batch)
    w_ai = jnp.concatenate([lru_w_a[0], lru_w_i[0]], axis=-1).astype(BF16)
    b_ai = jnp.concatenate([lru_b_a[0].reshape(LRU_BLOCKS, LRU_BW),
                            lru_b_i[0].reshape(LRU_BLOCKS, LRU_BW)], axis=-1).reshape(1, 2 * LRU_W)
    lru = rglru(proj, lru_conv_w[0], lru_conv_b[0], w_ai, b_ai, lru_lambda[0], batch)
    xf, hn = proj_residual([ret, lru], ev_w_out, 0, xf, norm_ffn[0], tm=512, kc=512)
    xf = ffn(xf, hn, 0)

    proj = norm_matmul(xf, norm_mix[1], od_w_in, 0, tm=1024, tn=1024, out_dtype=BF16,
                       head_gain=moba_head_gain(q_norm[0], k_norm[0]), head_dim=MOBA_DH)
    att = moba(proj, rel_bias, batch)
    xf, hn = proj_residual([att], od_w_out, 0, xf, norm_ffn[1], tm=512, kc=512)
    xf = ffn(xf, hn, 1)
    return xf.reshape(batch, seq, d)
```

```python
import functools
import math

import jax
import jax.numpy as jnp
from jax import lax
from jax.experimental import pallas as pl
from jax.experimental.pallas import tpu as pltpu

D_MODEL = 2048
SEQ = 2048
RET_HEADS = 4
RET_DK = 256
RET_DV = 256
RET_W = RET_HEADS * RET_DK
RET_CHUNK = 128
ROPE_BASE = 10000.0
LRU_W = 1024
LRU_BLOCKS = 8
LRU_BW = 128
LRU_CONV = 4
LRU_C = 8.0
LRU_SCAN_CHUNK = 64
MOBA_HEADS = 16
MOBA_DH = 128
MOBA_BLOCK = 256
MOBA_TOPK = 3
REL_BUCKETS = 32
REL_MAX_DIST = 128
D_FF = 5632
FFN_CONV = 3
EPS = 1e-6

VMEM_LIMIT_BYTES = 56 * 1024 * 1024
SUBLANES = 8

BF16 = jnp.bfloat16
F32 = jnp.float32
NEG_INF = float("-inf")
MASK_VALUE = -1e30
LOG2E = math.log2(math.e)


def _params(*sem):
    return pltpu.CompilerParams(dimension_semantics=sem, vmem_limit_bytes=VMEM_LIMIT_BYTES)


def _gelu_tanh(x):
    c = math.sqrt(2.0 / math.pi)
    return 0.5 * x * (1.0 + jnp.tanh(c * (x + 0.044715 * (x * x * x))))


def _rms_scale(x):
    return lax.rsqrt(jnp.mean(x * x, axis=-1, keepdims=True) + EPS)


class SideCast:
    def __init__(self, w, layer, n_steps, step_of):
        k, n = w.shape[1:]
        rows = k // n_steps
        assert rows * n_steps == k and rows % (2 * SUBLANES) == 0
        self.w = w
        self.in_spec = pl.BlockSpec((None, rows, n), lambda *ids: (layer, step_of(*ids), 0))
        self.out_spec = pl.BlockSpec((rows, n), lambda *ids: (step_of(*ids), 0))
        self.out_shape = jax.ShapeDtypeStruct((k, n), BF16)


def _with_side_cast(body, n_in, n_out):
    def wrapped(*refs):
        side_in, side_out = refs[n_in], refs[n_in + 1 + n_out]
        side_out[...] = side_in[...].astype(BF16)
        body(*refs[:n_in], *refs[n_in + 1:n_in + 1 + n_out], *refs[n_in + 2 + n_out:])
    return wrapped


def _norm_mm_kernel(x_ref, g_ref, w_ref, hg_ref, o_ref, hn_ref, *scratch,
                    head_norm_tiles, head_dim, sub):
    w_is_bf16 = w_ref.dtype == BF16
    if head_norm_tiles:
        acc_ref = scratch[-1]
        wb_ref = w_ref if w_is_bf16 else scratch[0]
    j = pl.program_id(1)

    @pl.when(j == 0)
    def _():
        x = x_ref[...]
        hn_ref[...] = (x * _rms_scale(x) * g_ref[...]).astype(BF16)

    def plain():
        o_ref[...] = jnp.dot(hn_ref[...], w_ref[...].astype(BF16),
                             preferred_element_type=F32).astype(o_ref.dtype)

    def head_normed():
        tm, tn = o_ref.shape
        if not w_is_bf16:
            wb_ref[...] = w_ref[...].astype(BF16)

        def mm(r):
            acc_ref[r % 2] = jnp.dot(hn_ref[r * sub:(r + 1) * sub, :], wb_ref[...],
                                     preferred_element_type=F32)

        mm(0)
        for r in range(tm // sub):
            if (r + 1) * sub < tm:
                mm(r + 1)
            heads = [acc_ref[r % 2, :, s:s + head_dim] for s in range(0, tn, head_dim)]
            normed = jnp.concatenate([a * _rms_scale(a) for a in heads], axis=-1)
            o_ref[r * sub:(r + 1) * sub, :] = (normed * hg_ref[...]).astype(o_ref.dtype)

    if head_norm_tiles:
        pl.when(j < head_norm_tiles)(head_normed)
        pl.when(j >= head_norm_tiles)(plain)
    else:
        plain()


def norm_matmul(x, g, w, layer, *, tm, tn, out_dtype, head_gain=None, head_dim=None, sub=256):
    t, d = x.shape
    n = w.shape[-1]
    if layer is None:
        assert w.dtype == BF16
        w_spec = pl.BlockSpec((d, tn), lambda i, j: (0, j))
    else:
        w_spec = pl.BlockSpec((None, d, tn), lambda i, j: (layer, 0, j))
    if head_gain is None:
        head_norm_tiles, hg = 0, jnp.ones((1, n), F32)
    else:
        head_norm_tiles = head_gain.shape[0] // tn
        assert head_norm_tiles * tn == head_gain.shape[0] and tn % head_dim == 0
        hg = jnp.concatenate([head_gain, jnp.ones((n - head_gain.shape[0],), F32)]).reshape(1, n)
    scratch = [pltpu.VMEM((tm, d), BF16)]
    if head_norm_tiles:
        if layer is not None:
            scratch.append(pltpu.VMEM((d, tn), BF16))
        scratch.append(pltpu.VMEM((2, sub, tn), F32))
    return pl.pallas_call(
        functools.partial(_norm_mm_kernel, head_norm_tiles=head_norm_tiles, head_dim=head_dim,
                          sub=sub),
        grid=(t // tm, n // tn),
        in_specs=[
            pl.BlockSpec((tm, d), lambda i, j: (i, 0)),
            pl.BlockSpec((1, d), lambda i, j: (0, 0)),
            w_spec,
            pl.BlockSpec((1, tn), lambda i, j: (0, j)),
        ],
        out_specs=pl.BlockSpec((tm, tn), lambda i, j: (i, j)),
        out_shape=jax.ShapeDtypeStruct((t, n), out_dtype),
        scratch_shapes=scratch,
        compiler_params=_params("arbitrary", "arbitrary"),
        name="norm_matmul",
    )(x, g.reshape(1, d), w, hg)


def _retention_kernel(q_ref, k_ref, v_ref, g_ref, cos_ref, sin_ref, decay_ref, zeta_ref, xi_ref,
                      gch_ref, rn_ref, o_ref, qb_ref, kb_ref, kv_ref, st_ref):
    half = RET_DK // 2
    n_chunks = q_ref.shape[0] // RET_CHUNK
    chunk_rows = lambda n: slice(n * RET_CHUNK, (n + 1) * RET_CHUNK)

    def rotary(x, cos, sin):
        x1, x2 = x[:, :half], x[:, half:]
        return jnp.concatenate([x1 * cos - x2 * sin, x1 * sin + x2 * cos], axis=-1)

    for n in range(n_chunks):
        rows = chunk_rows(n)
        cos, sin = cos_ref[rows, :], sin_ref[rows, :]
        qb_ref[rows, :] = rotary(q_ref[rows, :].astype(F32), cos, sin).astype(BF16)
        kr = rotary(k_ref[rows, :].astype(F32), cos, sin) * (RET_DK ** -0.5)
        kb_ref[rows, :] = kr.astype(BF16)
        kz = (kr * zeta_ref[0]).astype(BF16)
        kv_ref[n] = lax.dot_general(kz, v_ref[rows, :], (((0,), (0,)), ((), ())),
                                    preferred_element_type=F32)

    state = jnp.zeros((RET_DK, RET_DV), F32)
    for n in range(n_chunks):
        st_ref[n] = state.astype(BF16)
        state = gch_ref[0] * state + kv_ref[n]

    for n in range(n_chunks):
        rows = chunk_rows(n)
        qb, vb = qb_ref[rows, :], v_ref[rows, :]
        scores = lax.dot_general(qb, kb_ref[rows, :], (((1,), (1,)), ((), ())),
                                 preferred_element_type=F32) * decay_ref[0]
        inner = jnp.dot(scores.astype(BF16), vb, preferred_element_type=F32)
        cross = jnp.dot(qb, st_ref[n], preferred_element_type=F32) * xi_ref[0]
        out = inner + cross
        out = out * _rms_scale(out) * rn_ref[0]
        g = g_ref[rows, :].astype(F32)
        o_ref[rows, :] = (out * (g * jax.nn.sigmoid(g))).astype(o_ref.dtype)


def retention(proj, tables, ret_norm, batch, side_w, side_layer):
    cos, sin, decay, zeta, xi, gch = tables
    t = proj.shape[0]
    hq = RET_W // RET_DK
    col = lambda off: pl.BlockSpec((SEQ, RET_DK), lambda b, h: (b, off + h))
    per_head = lambda shape: pl.BlockSpec((1,) + shape, lambda b, h: (h, 0, 0))
    full = pl.BlockSpec((SEQ, RET_DK // 2), lambda b, h: (0, 0))
    side = SideCast(side_w, side_layer, batch * RET_HEADS, lambda b, h: b * RET_HEADS + h)
    in_specs = [col(0), col(hq), col(2 * hq), col(3 * hq), full, full,
                per_head((RET_CHUNK, RET_CHUNK)), per_head((RET_CHUNK, RET_DK)),
                per_head((RET_CHUNK, RET_DV)), per_head((1, RET_DV)), per_head((1, RET_DV))]
    return pl.pallas_call(
        _with_side_cast(_retention_kernel, len(in_specs), 1),
        grid=(batch, RET_HEADS),
        in_specs=in_specs + [side.in_spec],
        out_specs=[pl.BlockSpec((SEQ, RET_DV), lambda b, h: (b, h)), side.out_spec],
        out_shape=[jax.ShapeDtypeStruct((t, RET_HEADS * RET_DV), BF16), side.out_shape],
        scratch_shapes=[pltpu.VMEM((SEQ, RET_DK), BF16), pltpu.VMEM((SEQ, RET_DK), BF16),
                        pltpu.VMEM((SEQ // RET_CHUNK, RET_DK, RET_DV), F32),
                        pltpu.VMEM((SEQ // RET_CHUNK, RET_DK, RET_DV), BF16)],
        compiler_params=_params("arbitrary", "arbitrary"),
        name="retention",
    )(proj, proj, proj, proj, cos, sin, decay, zeta, xi, gch,
      ret_norm.reshape(RET_HEADS, 1, RET_DV), side.w)


def retention_tables():
    half = RET_DK // 2
    pos = jnp.arange(SEQ)
    freqs = ROPE_BASE ** (-jnp.arange(half, dtype=F32) / half)
    ang = pos.astype(F32)[:, None] * freqs[None, :]
    cos, sin = jnp.cos(ang), jnp.sin(ang)
    log_g = jnp.log1p(-jnp.exp2(-5.0 - jnp.arange(RET_HEADS, dtype=F32)))
    i = jnp.arange(RET_CHUNK)
    diff = i[:, None] - i[None, :]
    decay = jnp.where(diff >= 0, jnp.exp(log_g[:, None, None] * jnp.maximum(diff, 0)), 0.0)
    zeta = jnp.exp(log_g[None, :] * (RET_CHUNK - 1 - i)[:, None])
    xi = jnp.exp(log_g[None, :] * (i + 1)[:, None])
    g_chunk = jnp.exp(log_g * RET_CHUNK)
    zeta_b = jnp.broadcast_to(zeta.T[:, :, None], (RET_HEADS, RET_CHUNK, RET_DK))
    xi_b = jnp.broadcast_to(xi.T[:, :, None], (RET_HEADS, RET_CHUNK, RET_DV))
    gch_b = jnp.broadcast_to(g_chunk[:, None, None], (RET_HEADS, 1, RET_DV))
    return cos, sin, decay, zeta_b, xi_b, gch_b


def _shift_rows(v, d, fill, row):
    if d % SUBLANES == 0:
        return jnp.concatenate([jnp.full((d, v.shape[1]), fill, v.dtype), v[:-d]], axis=0)
    return jnp.where(row >= d, pltpu.roll(v, d, axis=0), fill)


def _rglru_kernel(xl_ref, gl_ref, cw_ref, cb_ref, wai_ref, bai_ref, lam_ref, o_ref, a_ref, u_ref):
    s = xl_ref.shape[0]
    x = xl_ref[...].astype(F32)
    row = lax.broadcasted_iota(jnp.int32, x.shape, 0)
    xc = x * cw_ref[LRU_CONV - 1:LRU_CONV, :] + cb_ref[...]
    for d in range(1, LRU_CONV):
        xs = jnp.where(row >= d, pltpu.roll(x, d, axis=0), 0.0)
        xc = xc + xs * cw_ref[LRU_CONV - 1 - d:LRU_CONV - d, :]
    y = jnp.dot(xc.astype(BF16), wai_ref[0], preferred_element_type=F32) + bai_ref[...]
    r = jax.nn.sigmoid(y[:, :LRU_BW])
    gi = jax.nn.sigmoid(y[:, LRU_BW:])
    neg_lam = -lam_ref[...]
    softplus = jnp.maximum(neg_lam, 0.0) + jnp.log1p(jnp.exp(-jnp.abs(neg_lam)))
    log_a = (-LRU_C) * r * softplus
    a = jnp.exp(log_a)
    a_ref[...] = a
    u_ref[...] = jnp.sqrt(1.0 - a * a) * (gi * xc)

    c = LRU_SCAN_CHUNK
    crow = lax.broadcasted_iota(jnp.int32, (c, LRU_BW), 0)

    def chunk(n, h_prev):
        rows = pl.ds(pl.multiple_of(n * c, c), c)
        av, bv = a_ref[rows, :], u_ref[rows, :]
        d = 1
        while d < c:
            a_sh = _shift_rows(av, d, 1.0, crow)
            b_sh = _shift_rows(bv, d, 0.0, crow)
            bv = av * b_sh + bv
            av = av * a_sh
            d *= 2
        h = bv + av * h_prev
        o_ref[rows, :] = (h * _gelu_tanh(gl_ref[rows, :].astype(F32))).astype(o_ref.dtype)
        return h[c - 1:c, :]

    lax.fori_loop(0, s // c, chunk, jnp.zeros((1, LRU_BW), F32))


def rglru(proj, conv_w, conv_b, w_ai, b_ai, lam, batch):
    t = proj.shape[0]
    x_off = 4 * RET_W // LRU_BW
    g_off = x_off + LRU_W // LRU_BW
    vec = lambda rows: pl.BlockSpec((rows, LRU_BW), lambda b, g: (0, g))
    return pl.pallas_call(
        _rglru_kernel,
        grid=(batch, LRU_BLOCKS),
        in_specs=[
            pl.BlockSpec((SEQ, LRU_BW), lambda b, g: (b, x_off + g)),
            pl.BlockSpec((SEQ, LRU_BW), lambda b, g: (b, g_off + g)),
            vec(LRU_CONV), vec(1),
            pl.BlockSpec((1, LRU_BW, 2 * LRU_BW), lambda b, g: (g, 0, 0)),
            pl.BlockSpec((1, 2 * LRU_BW), lambda b, g: (0, g)),
            vec(1),
        ],
        out_specs=pl.BlockSpec((SEQ, LRU_BW), lambda b, g: (b, g)),
        out_shape=jax.ShapeDtypeStruct((t, LRU_W), BF16),
        scratch_shapes=[pltpu.VMEM((SEQ, LRU_BW), F32), pltpu.VMEM((SEQ, LRU_BW), F32)],
        compiler_params=_params("arbitrary", "arbitrary"),
        name="rglru",
    )(proj, proj, conv_w, conv_b.reshape(1, LRU_W), w_ai, b_ai, lam.reshape(1, LRU_W))


def _proj_res_kernel(*refs, splits):
    part_refs = refs[:len(splits)]
    w_ref, x_ref, g_ref, o_ref, hn_ref = refs[len(splits):]
    acc = x_ref[...]
    off = 0
    for a_ref, k in zip(part_refs, splits):
        acc = acc + jnp.dot(a_ref[...], w_ref[off:off + k, :], preferred_element_type=F32)
        off += k
    o_ref[...] = acc
    hn_ref[...] = (acc * _rms_scale(acc) * g_ref[...]).astype(BF16)


def proj_residual(parts, w, x, g, *, tm):
    t, n = x.shape
    splits = tuple(p.shape[1] for p in parts)
    row = lambda i: (i, 0)
    return pl.pallas_call(
        functools.partial(_proj_res_kernel, splits=splits),
        grid=(t // tm,),
        in_specs=[pl.BlockSpec((tm, k), row) for k in splits] + [
            pl.BlockSpec(w.shape, lambda i: (0, 0)),
            pl.BlockSpec((tm, n), row),
            pl.BlockSpec((1, n), lambda i: (0, 0)),
        ],
        out_specs=[pl.BlockSpec((tm, n), row), pl.BlockSpec((tm, n), row)],
        out_shape=[jax.ShapeDtypeStruct((t, n), F32), jax.ShapeDtypeStruct((t, n), BF16)],
        compiler_params=_params("arbitrary"),
        name="proj_residual",
    )(*parts, w, x, g.reshape(1, n))


def _ffn_up_kernel(hn_ref, wg_ref, wv_ref, cwg_ref, cwv_ref, cbg_ref, cbv_ref, o_ref,
                   wb_ref, u_ref, carry_ref, *, tiles_per_seq, sub):
    i = pl.program_id(1)
    tm = hn_ref.shape[0]
    tn = wg_ref.shape[1]

    @pl.when(i == 0)
    def _():
        wb_ref[:, :tn] = wg_ref[...].astype(BF16)
        wb_ref[:, tn:] = wv_ref[...].astype(BF16)

    def up(r):
        u_ref[r % 2, SUBLANES:, :] = jnp.dot(hn_ref[r * sub:(r + 1) * sub, :], wb_ref[...],
                                             preferred_element_type=F32)

    def conv(slot, cols, cw_ref, cb_ref):
        out = u_ref[slot, SUBLANES:, cols] * cw_ref[FFN_CONV - 1:FFN_CONV, :] + cb_ref[...]
        for d in range(1, FFN_CONV):
            prev = u_ref[slot, SUBLANES - d:SUBLANES - d + sub, cols]
            out = out + prev * cw_ref[FFN_CONV - 1 - d:FFN_CONV - d, :]
        return out

    seq_start = (i % tiles_per_seq) == 0
    u_ref[0, :SUBLANES, :] = jnp.where(seq_start, 0.0, carry_ref[...])
    up(0)
    for r in range(tm // sub):
        slot = r % 2
        u_ref[1 - slot, :SUBLANES, :] = u_ref[slot, sub:, :]
        if (r + 1) * sub < tm:
            up(r + 1)
        gate = conv(slot, slice(0, tn), cwg_ref, cbg_ref)
        val = conv(slot, slice(tn, 2 * tn), cwv_ref, cbv_ref)
        o_ref[r * sub:(r + 1) * sub, :] = (_gelu_tanh(gate) * val).astype(o_ref.dtype)
    carry_ref[...] = u_ref[(tm // sub) % 2, :SUBLANES, :]


def ffn_up(hn, w_up, conv_w, conv_b, layer, side_w, side_layer, *, tm, tn, sub):
    t, d = hn.shape
    nj, ni = D_FF // tn, t // tm
    cb = conv_b.reshape(conv_b.shape[0], 1, 2 * D_FF)
    per_col = lambda rows, off: pl.BlockSpec((None, rows, tn), lambda j, i: (layer, 0, off + j))
    side = SideCast(side_w, side_layer, nj * ni, lambda j, i: j * ni + i)
    in_specs = [pl.BlockSpec((tm, d), lambda j, i: (i, 0)),
                per_col(d, 0), per_col(d, nj),
                per_col(FFN_CONV, 0), per_col(FFN_CONV, nj),
                per_col(1, 0), per_col(1, nj)]
    kern = functools.partial(_ffn_up_kernel, tiles_per_seq=SEQ // tm, sub=sub)
    return pl.pallas_call(
        _with_side_cast(kern, len(in_specs), 1),
        grid=(nj, ni),
        in_specs=in_specs + [side.in_spec],
        out_specs=[pl.BlockSpec((tm, tn), lambda j, i: (i, j)), side.out_spec],
        out_shape=[jax.ShapeDtypeStruct((t, D_FF), BF16), side.out_shape],
        scratch_shapes=[pltpu.VMEM((d, 2 * tn), BF16),
                        pltpu.VMEM((2, SUBLANES + sub, 2 * tn), F32),
                        pltpu.VMEM((SUBLANES, 2 * tn), F32)],
        compiler_params=_params("arbitrary", "arbitrary"),
        name="ffn_up",
    )(hn, w_up, w_up, conv_w, conv_w, cb, cb, side.w)


def _mm_res_kernel(a_ref, w_ref, x_ref, o_ref):
    o_ref[...] = x_ref[...] + jnp.dot(a_ref[...], w_ref[...], preferred_element_type=F32)


def matmul_residual(a, w, x, *, tm, tn, side_w=None, side_layer=None):
    t, k = a.shape
    n = w.shape[1]
    nj, ni = n // tn, t // tm
    in_specs = [pl.BlockSpec((tm, k), lambda j, i: (i, 0)),
                pl.BlockSpec((k, tn), lambda j, i: (0, j)),
                pl.BlockSpec((tm, tn), lambda j, i: (i, j))]
    out_specs = [pl.BlockSpec((tm, tn), lambda j, i: (i, j))]
    out_shape = [jax.ShapeDtypeStruct((t, n), F32)]
    operands = [a, w, x]
    kern = _mm_res_kernel
    if side_w is not None:
        side = SideCast(side_w, side_layer, nj * ni, lambda j, i: j * ni + i)
        kern = _with_side_cast(kern, len(in_specs), 1)
        in_specs.append(side.in_spec)
        out_specs.append(side.out_spec)
        out_shape.append(side.out_shape)
        operands.append(side.w)
    outs = pl.pallas_call(
        kern,
        grid=(nj, ni),
        in_specs=in_specs,
        out_specs=out_specs,
        out_shape=out_shape,
        compiler_params=_params("arbitrary", "arbitrary"),
        name="matmul_residual",
    )(*operands)
    return outs if side_w is not None else outs[0]


def _moba_kernel(relb_ref, bkt_ref, q_ref, k_ref, v_ref, o_ref,
                 bias_ref, qt_ref, ks_ref, vt_ref):
    h, b = pl.program_id(0), pl.program_id(1)
    nb = SEQ // MOBA_BLOCK
    blk = MOBA_BLOCK
    dh = MOBA_DH

    @pl.when((h == 0) & (b == 0))
    def _():
        key_blk = lax.broadcasted_iota(jnp.int32, (SEQ, dh), 0) // blk
        lane = lax.broadcasted_iota(jnp.int32, (SEQ, dh), 1)
        onehot = (lane < 2 * nb) & ((lane % nb) == key_blk)
        ks_ref[:, dh:] = jnp.where(onehot, 1.0, 0.0).astype(BF16)
        qt_ref[dh:, :] = jnp.zeros((dh, SEQ), BF16)
        vt_ref[dh:, :] = jnp.ones((vt_ref.shape[0] - dh, SEQ), BF16)

    @pl.when(b == 0)
    def _():
        for m in range(2):
            bk = bkt_ref[m]
            acc = jnp.full(bk.shape, NEG_INF, F32)
            for t in range(REL_BUCKETS):
                acc = jnp.where(bk == t, relb_ref[t, h] * LOG2E, acc)
            bias_ref[m] = acc

    far_bias = relb_ref[REL_BUCKETS - 1, h] * LOG2E

    qt_ref[:dh, :] = q_ref[...].astype(F32).T.astype(BF16)
    ks_ref[:, :dh] = k_ref[...]
    vt_ref[:dh, :] = v_ref[...].astype(F32).T.astype(BF16)

    blk_id = lax.broadcasted_iota(jnp.int32, (nb, SEQ), 0)
    q_blk = lax.broadcasted_iota(jnp.int32, (nb, SEQ), 1) // blk
    averager = jnp.where(blk_id == q_blk, 1.0 / blk, 0.0).astype(BF16)
    kmean = jnp.dot(averager, k_ref[...], preferred_element_type=F32)
    gate = jnp.dot(kmean.astype(BF16), qt_ref[:dh, :], preferred_element_type=F32)
    valid = blk_id < q_blk
    gm = jnp.where(valid, gate, NEG_INF)
    cnt = jnp.zeros(gate.shape, F32)
    for s in range(1, nb):
        gs = pltpu.roll(gm, s, axis=0)
        tie = jnp.where(((blk_id - s) % nb) < blk_id, 1.0, 0.0)
        cnt = cnt + jnp.where(gs > gm, 1.0, jnp.where(gs == gm, tie, 0.0))
    picked = jnp.where(blk_id < q_blk - 1, far_bias, 0.0)
    row = jnp.where(valid, jnp.where(cnt < MOBA_TOPK, picked, MASK_VALUE), 0.0)
    row_hi = row.astype(BF16).astype(F32)
    row_lo = jnp.where(row > 0.5 * MASK_VALUE, row - row_hi, 0.0)
    qt_ref[dh:dh + 2 * nb, :] = jnp.concatenate([row_hi, row_lo], axis=0).astype(BF16)

    def scores(i):
        return jnp.dot(ks_ref[:(i + 1) * blk, :], qt_ref[:, i * blk:(i + 1) * blk],
                       preferred_element_type=F32)

    def softmax(i, s_all):
        tiles = []
        for jb in range(i + 1):
            s_t = s_all[jb * blk:(jb + 1) * blk, :]
            if jb >= i - 1:
                s_t = s_t + bias_ref[i - jb]
            tiles.append(s_t)
        m = functools.reduce(jnp.maximum, [jnp.max(t, axis=0, keepdims=True) for t in tiles])
        return jnp.concatenate([jnp.exp2(t - m).astype(BF16) for t in tiles], axis=0)

    def attend(i, p_all):
        acc = jnp.dot(vt_ref[:, :(i + 1) * blk], p_all, preferred_element_type=F32)
        out = acc[:dh, :] / acc[dh:dh + 1, :]
        o_ref[i * blk:(i + 1) * blk, :] = out.T.astype(o_ref.dtype)

    s_next = scores(0)
    pending = None
    for i in range(nb):
        s_all = s_next
        if i + 1 < nb:
            s_next = scores(i + 1)
        p_all = softmax(i, s_all)
        if pending is not None:
            attend(*pending)
        pending = (i, p_all)
    attend(*pending)


def moba_bucket_tables():
    r = jnp.arange(MOBA_BLOCK)
    dist_own = r[None, :] - r[:, None]

    def bucket(dist):
        n = jnp.maximum(dist, 0)
        max_exact = REL_BUCKETS // 2
        nf = jnp.maximum(n, 1).astype(F32)
        large = max_exact + (jnp.log(nf / max_exact) / math.log(REL_MAX_DIST / max_exact)
                             * (REL_BUCKETS - max_exact)).astype(jnp.int32)
        large = jnp.minimum(large, REL_BUCKETS - 1)
        return jnp.where(n < max_exact, n, large)

    own = jnp.where(dist_own >= 0, bucket(dist_own), -1)
    adj = bucket(dist_own + MOBA_BLOCK)
    return jnp.stack([own, adj]).astype(jnp.int32)


def moba(proj, rel_bias, batch, side_w, side_layer):
    t = proj.shape[0]
    col = lambda off: pl.BlockSpec((SEQ, MOBA_DH), lambda h, b: (b, off + h))
    side = SideCast(side_w, side_layer, MOBA_HEADS * batch, lambda h, b: h * batch + b)
    in_specs = [pl.BlockSpec(memory_space=pltpu.SMEM),
                pl.BlockSpec((2, MOBA_BLOCK, MOBA_BLOCK), lambda h, b: (0, 0, 0)),
                col(0), col(MOBA_HEADS), col(2 * MOBA_HEADS)]
    return pl.pallas_call(
        _with_side_cast(_moba_kernel, len(in_specs), 1),
        grid=(MOBA_HEADS, batch),
        in_specs=in_specs + [side.in_spec],
        out_specs=[pl.BlockSpec((SEQ, MOBA_DH), lambda h, b: (b, h)), side.out_spec],
        out_shape=[jax.ShapeDtypeStruct((t, MOBA_HEADS * MOBA_DH), BF16), side.out_shape],
        scratch_shapes=[pltpu.VMEM((2, MOBA_BLOCK, MOBA_BLOCK), F32),
                        pltpu.VMEM((2 * MOBA_DH, SEQ), BF16),
                        pltpu.VMEM((SEQ, 2 * MOBA_DH), BF16),
                        pltpu.VMEM((MOBA_DH + 2 * SUBLANES, SEQ), BF16)],
        compiler_params=_params("arbitrary", "arbitrary"),
        name="moba",
    )(rel_bias, moba_bucket_tables(), proj, proj, proj, side.w)


def moba_head_gain(q_norm, k_norm):
    qg = jnp.tile(q_norm * (MOBA_DH ** -0.5 * LOG2E), MOBA_HEADS)
    return jnp.concatenate([qg, jnp.tile(k_norm, MOBA_HEADS)])


def kernel(x, norm_mix, norm_ffn, ev_w_in, ret_norm, lru_conv_w, lru_conv_b, lru_w_a, lru_b_a,
           lru_w_i, lru_b_i, lru_lambda, ev_w_out, od_w_in, q_norm, k_norm, od_w_out, rel_bias,
           ffn_w_up, ffn_conv_w, ffn_conv_b, ffn_w_down):
    batch, seq, d = x.shape
    assert (seq, d) == (SEQ, D_MODEL)
    xf = x.reshape(batch * seq, d)


    proj = norm_matmul(xf, norm_mix[0], ev_w_in, 0, tm=1024, tn=1024, out_dtype=BF16)
    ret, w_out_b = retention(proj, retention_tables(), ret_norm[0], batch, ev_w_out, 0)
    w_ai = jnp.concatenate([lru_w_a[0], lru_w_i[0]], axis=-1).astype(BF16)
    b_ai = jnp.concatenate([lru_b_a[0].reshape(LRU_BLOCKS, LRU_BW),
                            lru_b_i[0].reshape(LRU_BLOCKS, LRU_BW)], axis=-1).reshape(1, 2 * LRU_W)
    lru = rglru(proj, lru_conv_w[0], lru_conv_b[0], w_ai, b_ai, lru_lambda[0], batch)
    xf, hn = proj_residual([ret, lru], w_out_b, xf, norm_ffn[0], tm=512)
    act, w_down_b = ffn_up(hn, ffn_w_up, ffn_conv_w, ffn_conv_b, 0, ffn_w_down, 0,
                           tm=2048, tn=512, sub=256)
    xf, w_in_b = matmul_residual(act, w_down_b, xf, tm=512, tn=1024, side_w=od_w_in, side_layer=0)

    proj = norm_matmul(xf, norm_mix[1], w_in_b, None, tm=1024, tn=1024, out_dtype=BF16,
                       head_gain=moba_head_gain(q_norm[0], k_norm[0]), head_dim=MOBA_DH)
    att, w_out_b = moba(proj, rel_bias, batch, od_w_out, 0)
    xf, hn = proj_residual([att], w_out_b, xf, norm_ffn[1], tm=512)
    act, w_down_b = ffn_up(hn, ffn_w_up, ffn_conv_w, ffn_conv_b, 1, ffn_w_down, 1,
                           tm=2048, tn=512, sub=256)
    xf = matmul_residual(act, w_down_b, xf, tm=512, tn=1024)
    return xf.reshape(batch, seq, d)
```

```python
import functools
import math

import jax
import jax.numpy as jnp
from jax import lax
from jax.experimental import pallas as pl
from jax.experimental.pallas import tpu as pltpu

D_MODEL = 2048
SEQ = 2048
RET_HEADS = 4
RET_DK = 256
RET_DV = 256
RET_W = RET_HEADS * RET_DK
RET_CHUNK = 128
ROPE_BASE = 10000.0
LRU_W = 1024
LRU_BLOCKS = 8
LRU_BW = 128
LRU_CONV = 4
LRU_C = 8.0
LRU_SCAN_CHUNK = 64
MOBA_HEADS = 16
MOBA_DH = 128
MOBA_BLOCK = 256
MOBA_TOPK = 3
REL_BUCKETS = 32
REL_MAX_DIST = 128
D_FF = 5632
FFN_CONV = 3
EPS = 1e-6

VMEM_LIMIT_BYTES = 56 * 1024 * 1024
SUBLANES = 8

BF16 = jnp.bfloat16
F32 = jnp.float32
NEG_INF = float("-inf")
MASK_VALUE = -1e30
LOG2E = math.log2(math.e)


def _params(*sem):
    return pltpu.CompilerParams(dimension_semantics=sem, vmem_limit_bytes=VMEM_LIMIT_BYTES)


def _gelu_tanh(x):
    a = -2.0 * math.sqrt(2.0 / math.pi) * LOG2E
    return x / (1.0 + jnp.exp2(x * (a + (a * 0.044715) * (x * x))))


def _rms_scale(x):
    return lax.rsqrt(jnp.mean(x * x, axis=-1, keepdims=True) + EPS)


class SideCast:
    def __init__(self, w, layer, n_steps, step_of):
        k, n = w.shape[1:]
        rows = k // n_steps
        assert rows * n_steps == k and rows % (2 * SUBLANES) == 0
        self.w = w
        self.in_spec = pl.BlockSpec((None, rows, n), lambda *ids: (layer, step_of(*ids), 0))
        self.out_spec = pl.BlockSpec((rows, n), lambda *ids: (step_of(*ids), 0))
        self.out_shape = jax.ShapeDtypeStruct((k, n), BF16)


def _with_side_cast(body, n_in, n_out):
    def wrapped(*refs):
        side_in, side_out = refs[n_in], refs[n_in + 1 + n_out]
        side_out[...] = side_in[...].astype(BF16)
        body(*refs[:n_in], *refs[n_in + 1:n_in + 1 + n_out], *refs[n_in + 2 + n_out:])
    return wrapped


def _norm_mm_kernel(x_ref, g_ref, w_ref, hg_ref, o_ref, hn_ref, *scratch,
                    head_norm_tiles, head_dim, sub):
    w_is_bf16 = w_ref.dtype == BF16
    wb_ref = w_ref if w_is_bf16 else scratch[0]
    acc_ref = scratch[-1] if head_norm_tiles else None
    j = pl.program_id(1)
    tm, tn = o_ref.shape
    n_sub = tm // sub

    def plain():
        o_ref[...] = jnp.dot(hn_ref[...], w_ref[...].astype(BF16),
                             preferred_element_type=F32).astype(o_ref.dtype)

    def staged(norm_rows, head_norm):
        if not w_is_bf16:
            wb_ref[...] = w_ref[...].astype(BF16)

        def norm(r):
            x = x_ref[r * sub:(r + 1) * sub, :]
            hn_ref[r * sub:(r + 1) * sub, :] = (x * _rms_scale(x) * g_ref[...]).astype(BF16)

        def mm(r):
            acc = jnp.dot(hn_ref[r * sub:(r + 1) * sub, :], wb_ref[...],
                          preferred_element_type=F32)
            if head_norm:
                acc_ref[r % 2] = acc
            else:
                o_ref[r * sub:(r + 1) * sub, :] = acc.astype(o_ref.dtype)

        def epilogue(r):
            heads = [acc_ref[r % 2, :, s:s + head_dim] for s in range(0, tn, head_dim)]
            normed = jnp.concatenate([a * _rms_scale(a) for a in heads], axis=-1)
            o_ref[r * sub:(r + 1) * sub, :] = (normed * hg_ref[...]).astype(o_ref.dtype)

        if norm_rows:
            norm(0)
        for r in range(n_sub):
            if norm_rows and r + 1 < n_sub:
                norm(r + 1)
            mm(r)
            if head_norm and r > 0:
                epilogue(r - 1)
        if head_norm:
            epilogue(n_sub - 1)

    pl.when(j == 0)(functools.partial(staged, True, head_norm_tiles > 0))
    if head_norm_tiles > 1:
        pl.when((j > 0) & (j < head_norm_tiles))(functools.partial(staged, False, True))
    pl.when(j >= max(head_norm_tiles, 1))(plain)


def norm_matmul(x, g, w, layer, *, tm, tn, out_dtype, head_gain=None, head_dim=None, sub=256):
    t, d = x.shape
    n = w.shape[-1]
    if layer is None:
        assert w.dtype == BF16
        w_spec = pl.BlockSpec((d, tn), lambda i, j: (0, j))
    else:
        w_spec = pl.BlockSpec((None, d, tn), lambda i, j: (layer, 0, j))
    if head_gain is None:
        head_norm_tiles, hg = 0, jnp.ones((1, n), F32)
    else:
        head_norm_tiles = head_gain.shape[0] // tn
        assert head_norm_tiles * tn == head_gain.shape[0] and tn % head_dim == 0
        hg = jnp.concatenate([head_gain, jnp.ones((n - head_gain.shape[0],), F32)]).reshape(1, n)
    scratch = [pltpu.VMEM((tm, d), BF16)]
    if layer is not None:
        scratch.append(pltpu.VMEM((d, tn), BF16))
    if head_norm_tiles:
        scratch.append(pltpu.VMEM((2, sub, tn), F32))
    return pl.pallas_call(
        functools.partial(_norm_mm_kernel, head_norm_tiles=head_norm_tiles, head_dim=head_dim,
                          sub=sub),
        grid=(t // tm, n // tn),
        in_specs=[
            pl.BlockSpec((tm, d), lambda i, j: (i, 0)),
            pl.BlockSpec((1, d), lambda i, j: (0, 0)),
            w_spec,
            pl.BlockSpec((1, tn), lambda i, j: (0, j)),
        ],
        out_specs=pl.BlockSpec((tm, tn), lambda i, j: (i, j)),
        out_shape=jax.ShapeDtypeStruct((t, n), out_dtype),
        scratch_shapes=scratch,
        compiler_params=_params("arbitrary", "arbitrary"),
        name="norm_matmul",
    )(x, g.reshape(1, d), w, hg)


def _retention_kernel(q_ref, k_ref, v_ref, g_ref, cos_ref, sin_ref, decay_ref, zeta_ref, xi_ref,
                      gch_ref, rn_ref, o_ref, qb_ref, kb_ref, kv_ref, st_ref):
    half = RET_DK // 2
    n_chunks = q_ref.shape[0] // RET_CHUNK
    chunk_rows = lambda n: slice(n * RET_CHUNK, (n + 1) * RET_CHUNK)

    def rotary(x, cos, sin):
        x1, x2 = x[:, :half], x[:, half:]
        return jnp.concatenate([x1 * cos - x2 * sin, x1 * sin + x2 * cos], axis=-1)

    for n in range(n_chunks):
        rows = chunk_rows(n)
        cos, sin = cos_ref[rows, :], sin_ref[rows, :]
        qb_ref[rows, :] = rotary(q_ref[rows, :].astype(F32), cos, sin).astype(BF16)
        kr = rotary(k_ref[rows, :].astype(F32), cos, sin) * (RET_DK ** -0.5)
        kb_ref[rows, :] = kr.astype(BF16)
        kz = (kr * zeta_ref[0]).astype(BF16)
        kv_ref[n] = lax.dot_general(kz, v_ref[rows, :], (((0,), (0,)), ((), ())),
                                    preferred_element_type=F32)

    state = jnp.zeros((RET_DK, RET_DV), F32)
    for n in range(n_chunks):
        st_ref[n] = state.astype(BF16)
        state = gch_ref[0] * state + kv_ref[n]

    for n in range(n_chunks):
        rows = chunk_rows(n)
        qb, vb = qb_ref[rows, :], v_ref[rows, :]
        scores = lax.dot_general(qb, kb_ref[rows, :], (((1,), (1,)), ((), ())),
                                 preferred_element_type=F32) * decay_ref[0]
        inner = jnp.dot(scores.astype(BF16), vb, preferred_element_type=F32)
        cross = jnp.dot(qb, st_ref[n], preferred_element_type=F32) * xi_ref[0]
        out = inner + cross
        out = out * _rms_scale(out) * rn_ref[0]
        g = g_ref[rows, :].astype(F32)
        o_ref[rows, :] = (out * (g * jax.nn.sigmoid(g))).astype(o_ref.dtype)


def retention(proj, tables, ret_norm, batch, side_w, side_layer):
    cos, sin, decay, zeta, xi, gch = tables
    t = proj.shape[0]
    hq = RET_W // RET_DK
    col = lambda off: pl.BlockSpec((SEQ, RET_DK), lambda b, h: (b, off + h))
    per_head = lambda shape: pl.BlockSpec((1,) + shape, lambda b, h: (h, 0, 0))
    full = pl.BlockSpec((SEQ, RET_DK // 2), lambda b, h: (0, 0))
    side = SideCast(side_w, side_layer, batch * RET_HEADS, lambda b, h: b * RET_HEADS + h)
    in_specs = [col(0), col(hq), col(2 * hq), col(3 * hq), full, full,
                per_head((RET_CHUNK, RET_CHUNK)), per_head((RET_CHUNK, RET_DK)),
                per_head((RET_CHUNK, RET_DV)), per_head((1, RET_DV)), per_head((1, RET_DV))]
    return pl.pallas_call(
        _with_side_cast(_retention_kernel, len(in_specs), 1),
        grid=(batch, RET_HEADS),
        in_specs=in_specs + [side.in_spec],
        out_specs=[pl.BlockSpec((SEQ, RET_DV), lambda b, h: (b, h)), side.out_spec],
        out_shape=[jax.ShapeDtypeStruct((t, RET_HEADS * RET_DV), BF16), side.out_shape],
        scratch_shapes=[pltpu.VMEM((SEQ, RET_DK), BF16), pltpu.VMEM((SEQ, RET_DK), BF16),
                        pltpu.VMEM((SEQ // RET_CHUNK, RET_DK, RET_DV), F32),
                        pltpu.VMEM((SEQ // RET_CHUNK, RET_DK, RET_DV), BF16)],
        compiler_params=_params("arbitrary", "arbitrary"),
        name="retention",
    )(proj, proj, proj, proj, cos, sin, decay, zeta, xi, gch,
      ret_norm.reshape(RET_HEADS, 1, RET_DV), side.w)


def retention_tables():
    half = RET_DK // 2
    pos = jnp.arange(SEQ)
    freqs = ROPE_BASE ** (-jnp.arange(half, dtype=F32) / half)
    ang = pos.astype(F32)[:, None] * freqs[None, :]
    cos, sin = jnp.cos(ang), jnp.sin(ang)
    log_g = jnp.log1p(-jnp.exp2(-5.0 - jnp.arange(RET_HEADS, dtype=F32)))
    i = jnp.arange(RET_CHUNK)
    diff = i[:, None] - i[None, :]
    decay = jnp.where(diff >= 0, jnp.exp(log_g[:, None, None] * jnp.maximum(diff, 0)), 0.0)
    zeta = jnp.exp(log_g[None, :] * (RET_CHUNK - 1 - i)[:, None])
    xi = jnp.exp(log_g[None, :] * (i + 1)[:, None])
    g_chunk = jnp.exp(log_g * RET_CHUNK)
    zeta_b = jnp.broadcast_to(zeta.T[:, :, None], (RET_HEADS, RET_CHUNK, RET_DK))
    xi_b = jnp.broadcast_to(xi.T[:, :, None], (RET_HEADS, RET_CHUNK, RET_DV))
    gch_b = jnp.broadcast_to(g_chunk[:, None, None], (RET_HEADS, 1, RET_DV))
    return cos, sin, decay, zeta_b, xi_b, gch_b


def _shift_rows(v, d, fill, row):
    if d % SUBLANES == 0:
        return jnp.concatenate([jnp.full((d, v.shape[1]), fill, v.dtype), v[:-d]], axis=0)
    return jnp.where(row >= d, pltpu.roll(v, d, axis=0), fill)


def _rglru_kernel(xl_ref, gl_ref, cw_ref, cb_ref, wai_ref, bai_ref, lam_ref, o_ref, a_ref, u_ref):
    s = xl_ref.shape[0]
    x = xl_ref[...].astype(F32)
    row = lax.broadcasted_iota(jnp.int32, x.shape, 0)
    xc = x * cw_ref[LRU_CONV - 1:LRU_CONV, :] + cb_ref[...]
    for d in range(1, LRU_CONV):
        xs = jnp.where(row >= d, pltpu.roll(x, d, axis=0), 0.0)
        xc = xc + xs * cw_ref[LRU_CONV - 1 - d:LRU_CONV - d, :]
    y = jnp.dot(xc.astype(BF16), wai_ref[0], preferred_element_type=F32) + bai_ref[...]
    r = jax.nn.sigmoid(y[:, :LRU_BW])
    gi = jax.nn.sigmoid(y[:, LRU_BW:])
    neg_lam = -lam_ref[...]
    softplus = jnp.maximum(neg_lam, 0.0) + jnp.log1p(jnp.exp(-jnp.abs(neg_lam)))
    log_a = (-LRU_C) * r * softplus
    a = jnp.exp(log_a)
    a_ref[...] = a
    u_ref[...] = jnp.sqrt(1.0 - a * a) * (gi * xc)

    c = LRU_SCAN_CHUNK
    crow = lax.broadcasted_iota(jnp.int32, (c, LRU_BW), 0)

    def chunk(n, h_prev):
        rows = pl.ds(pl.multiple_of(n * c, c), c)
        av, bv = a_ref[rows, :], u_ref[rows, :]
        d = 1
        while d < c:
            a_sh = _shift_rows(av, d, 1.0, crow)
            b_sh = _shift_rows(bv, d, 0.0, crow)
            bv = av * b_sh + bv
            av = av * a_sh
            d *= 2
        h = bv + av * h_prev
        o_ref[rows, :] = (h * _gelu_tanh(gl_ref[rows, :].astype(F32))).astype(o_ref.dtype)
        return h[c - 1:c, :]

    lax.fori_loop(0, s // c, chunk, jnp.zeros((1, LRU_BW), F32))


def rglru(proj, conv_w, conv_b, w_ai, b_ai, lam, batch):
    t = proj.shape[0]
    x_off = 4 * RET_W // LRU_BW
    g_off = x_off + LRU_W // LRU_BW
    vec = lambda rows: pl.BlockSpec((rows, LRU_BW), lambda b, g: (0, g))
    return pl.pallas_call(
        _rglru_kernel,
        grid=(batch, LRU_BLOCKS),
        in_specs=[
            pl.BlockSpec((SEQ, LRU_BW), lambda b, g: (b, x_off + g)),
            pl.BlockSpec((SEQ, LRU_BW), lambda b, g: (b, g_off + g)),
            vec(LRU_CONV), vec(1),
            pl.BlockSpec((1, LRU_BW, 2 * LRU_BW), lambda b, g: (g, 0, 0)),
            pl.BlockSpec((1, 2 * LRU_BW), lambda b, g: (0, g)),
            vec(1),
        ],
        out_specs=pl.BlockSpec((SEQ, LRU_BW), lambda b, g: (b, g)),
        out_shape=jax.ShapeDtypeStruct((t, LRU_W), BF16),
        scratch_shapes=[pltpu.VMEM((SEQ, LRU_BW), F32), pltpu.VMEM((SEQ, LRU_BW), F32)],
        compiler_params=_params("arbitrary", "arbitrary"),
        name="rglru",
    )(proj, proj, conv_w, conv_b.reshape(1, LRU_W), w_ai, b_ai, lam.reshape(1, LRU_W))


def _proj_res_kernel(*refs, splits, sub):
    part_refs = refs[:len(splits)]
    w_ref, x_ref, g_ref, o_ref, hn_ref, acc_ref = refs[len(splits):]

    def mm(r):
        rows = slice(r * sub, (r + 1) * sub)
        acc, off = None, 0
        for a_ref, k in zip(part_refs, splits):
            part = jnp.dot(a_ref[rows, :], w_ref[off:off + k, :], preferred_element_type=F32)
            acc = part if acc is None else acc + part
            off += k
        acc_ref[r % 2] = acc

    def epilogue(r):
        rows = slice(r * sub, (r + 1) * sub)
        y = x_ref[rows, :] + acc_ref[r % 2]
        o_ref[rows, :] = y
        hn_ref[rows, :] = (y * _rms_scale(y) * g_ref[...]).astype(BF16)

    n_sub = x_ref.shape[0] // sub
    mm(0)
    for r in range(n_sub):
        if r + 1 < n_sub:
            mm(r + 1)
        epilogue(r)


def proj_residual(parts, w, x, g, *, tm, sub=256):
    t, n = x.shape
    splits = tuple(p.shape[1] for p in parts)
    row = lambda i: (i, 0)
    return pl.pallas_call(
        functools.partial(_proj_res_kernel, splits=splits, sub=sub),
        grid=(t // tm,),
        in_specs=[pl.BlockSpec((tm, k), row) for k in splits] + [
            pl.BlockSpec(w.shape, lambda i: (0, 0)),
            pl.BlockSpec((tm, n), row),
            pl.BlockSpec((1, n), lambda i: (0, 0)),
        ],
        out_specs=[pl.BlockSpec((tm, n), row), pl.BlockSpec((tm, n), row)],
        out_shape=[jax.ShapeDtypeStruct((t, n), F32), jax.ShapeDtypeStruct((t, n), BF16)],
        scratch_shapes=[pltpu.VMEM((2, sub, n), F32)],
        compiler_params=_params("arbitrary"),
        name="proj_residual",
    )(*parts, w, x, g.reshape(1, n))


def _ffn_up_kernel(hn_ref, wg_ref, wv_ref, cwg_ref, cwv_ref, cbg_ref, cbv_ref, o_ref,
                   wb_ref, u_ref, carry_ref, *, tiles_per_seq, sub):
    i = pl.program_id(1)
    tm = hn_ref.shape[0]
    tn = wg_ref.shape[1]

    @pl.when(i == 0)
    def _():
        wb_ref[:, :tn] = wg_ref[...].astype(BF16)
        wb_ref[:, tn:] = wv_ref[...].astype(BF16)

    def up(r):
        u_ref[r % 2, SUBLANES:, :] = jnp.dot(hn_ref[r * sub:(r + 1) * sub, :], wb_ref[...],
                                             preferred_element_type=F32)

    def conv(slot, cols, cw_ref, cb_ref):
        out = u_ref[slot, SUBLANES:, cols] * cw_ref[FFN_CONV - 1:FFN_CONV, :] + cb_ref[...]
        for d in range(1, FFN_CONV):
            prev = u_ref[slot, SUBLANES - d:SUBLANES - d + sub, cols]
            out = out + prev * cw_ref[FFN_CONV - 1 - d:FFN_CONV - d, :]
        return out

    seq_start = (i % tiles_per_seq) == 0
    u_ref[0, :SUBLANES, :] = jnp.where(seq_start, 0.0, carry_ref[...])
    up(0)
    for r in range(tm // sub):
        slot = r % 2
        u_ref[1 - slot, :SUBLANES, :] = u_ref[slot, sub:, :]
        if (r + 1) * sub < tm:
            up(r + 1)
        gate = conv(slot, slice(0, tn), cwg_ref, cbg_ref)
        val = conv(slot, slice(tn, 2 * tn), cwv_ref, cbv_ref)
        o_ref[r * sub:(r + 1) * sub, :] = (_gelu_tanh(gate) * val).astype(o_ref.dtype)
    carry_ref[...] = u_ref[(tm // sub) % 2, :SUBLANES, :]


def ffn_up(hn, w_up, conv_w, conv_b, layer, side_w, side_layer, *, tm, tn, sub):
    t, d = hn.shape
    nj, ni = D_FF // tn, t // tm
    cb = conv_b.reshape(conv_b.shape[0], 1, 2 * D_FF)
    per_col = lambda rows, off: pl.BlockSpec((None, rows, tn), lambda j, i: (layer, 0, off + j))
    side = SideCast(side_w, side_layer, nj * ni, lambda j, i: j * ni + i)
    in_specs = [pl.BlockSpec((tm, d), lambda j, i: (i, 0)),
                per_col(d, 0), per_col(d, nj),
                per_col(FFN_CONV, 0), per_col(FFN_CONV, nj),
                per_col(1, 0), per_col(1, nj)]
    kern = functools.partial(_ffn_up_kernel, tiles_per_seq=SEQ // tm, sub=sub)
    return pl.pallas_call(
        _with_side_cast(kern, len(in_specs), 1),
        grid=(nj, ni),
        in_specs=in_specs + [side.in_spec],
        out_specs=[pl.BlockSpec((tm, tn), lambda j, i: (i, j)), side.out_spec],
        out_shape=[jax.ShapeDtypeStruct((t, D_FF), BF16), side.out_shape],
        scratch_shapes=[pltpu.VMEM((d, 2 * tn), BF16),
                        pltpu.VMEM((2, SUBLANES + sub, 2 * tn), F32),
                        pltpu.VMEM((SUBLANES, 2 * tn), F32)],
        compiler_params=_params("arbitrary", "arbitrary"),
        name="ffn_up",
    )(hn, w_up, w_up, conv_w, conv_w, cb, cb, side.w)


def _mm_res_kernel(a_ref, w_ref, x_ref, o_ref):
    o_ref[...] = x_ref[...] + jnp.dot(a_ref[...], w_ref[...], preferred_element_type=F32)


def matmul_residual(a, w, x, *, tm, tn, side_w=None, side_layer=None):
    t, k = a.shape
    n = w.shape[1]
    nj, ni = n // tn, t // tm
    in_specs = [pl.BlockSpec((tm, k), lambda j, i: (i, 0)),
                pl.BlockSpec((k, tn), lambda j, i: (0, j)),
                pl.BlockSpec((tm, tn), lambda j, i: (i, j))]
    out_specs = [pl.BlockSpec((tm, tn), lambda j, i: (i, j))]
    out_shape = [jax.ShapeDtypeStruct((t, n), F32)]
    operands = [a, w, x]
    kern = _mm_res_kernel
    if side_w is not None:
        side = SideCast(side_w, side_layer, nj * ni, lambda j, i: j * ni + i)
        kern = _with_side_cast(kern, len(in_specs), 1)
        in_specs.append(side.in_spec)
        out_specs.append(side.out_spec)
        out_shape.append(side.out_shape)
        operands.append(side.w)
    outs = pl.pallas_call(
        kern,
        grid=(nj, ni),
        in_specs=in_specs,
        out_specs=out_specs,
        out_shape=out_shape,
        compiler_params=_params("arbitrary", "arbitrary"),
        name="matmul_residual",
    )(*operands)
    return outs if side_w is not None else outs[0]


def _moba_kernel(relb_ref, bkt_ref, q_ref, k_ref, v_ref, o_ref,
                 bias_ref, qt_ref, ks_ref, vt_ref, s_ref, p_ref):
    h, b = pl.program_id(0), pl.program_id(1)
    nb = SEQ // MOBA_BLOCK
    blk = MOBA_BLOCK
    dh = MOBA_DH

    @pl.when((h == 0) & (b == 0))
    def _():
        key_blk = lax.broadcasted_iota(jnp.int32, (SEQ, dh), 0) // blk
        lane = lax.broadcasted_iota(jnp.int32, (SEQ, dh), 1)
        onehot = (lane < 2 * nb) & ((lane % nb) == key_blk)
        ks_ref[:, dh:] = jnp.where(onehot, 1.0, 0.0).astype(BF16)
        qt_ref[dh:, :] = jnp.zeros((dh, SEQ), BF16)
        vt_ref[dh:, :] = jnp.ones((vt_ref.shape[0] - dh, SEQ), BF16)

    @pl.when(b == 0)
    def _():
        for m in range(2):
            bk = bkt_ref[m]
            acc = jnp.full(bk.shape, NEG_INF, F32)
            for t in range(REL_BUCKETS):
                acc = jnp.where(bk == t, relb_ref[t, h] * LOG2E, acc)
            bias_ref[m] = acc

    far_bias = relb_ref[REL_BUCKETS - 1, h] * LOG2E

    qt_ref[:dh, :] = q_ref[...].astype(F32).T.astype(BF16)
    ks_ref[:, :dh] = k_ref[...]
    vt_ref[:dh, :] = v_ref[...].astype(F32).T.astype(BF16)

    blk_id = lax.broadcasted_iota(jnp.int32, (nb, SEQ), 0)
    q_blk = lax.broadcasted_iota(jnp.int32, (nb, SEQ), 1) // blk
    averager = jnp.where(blk_id == q_blk, 1.0 / blk, 0.0).astype(BF16)
    kmean = jnp.dot(averager, k_ref[...], preferred_element_type=F32)
    gate = jnp.dot(kmean.astype(BF16), qt_ref[:dh, :], preferred_element_type=F32)
    valid = blk_id < q_blk
    gm = jnp.where(valid, gate, NEG_INF)
    cnt = jnp.zeros(gate.shape, F32)
    for s in range(1, nb):
        gs = pltpu.roll(gm, s, axis=0)
        tie = jnp.where(((blk_id - s) % nb) < blk_id, 1.0, 0.0)
        cnt = cnt + jnp.where(gs > gm, 1.0, jnp.where(gs == gm, tie, 0.0))
    picked = jnp.where(blk_id < q_blk - 1, far_bias, 0.0)
    row = jnp.where(valid, jnp.where(cnt < MOBA_TOPK, picked, MASK_VALUE), 0.0)
    row_hi = row.astype(BF16).astype(F32)
    row_lo = jnp.where(row > 0.5 * MASK_VALUE, row - row_hi, 0.0)
    qt_ref[dh:dh + 2 * nb, :] = jnp.concatenate([row_hi, row_lo], axis=0).astype(BF16)

    def scores(pos, i):
        s_ref[pos % 2, :(i + 1) * blk, :] = jnp.dot(
            ks_ref[:(i + 1) * blk, :], qt_ref[:, i * blk:(i + 1) * blk],
            preferred_element_type=F32)

    def softmax(pos, i):
        slot = pos % 2
        maxes = []
        for jb in range(i + 1):
            rows = slice(jb * blk, (jb + 1) * blk)
            s_t = s_ref[slot, rows, :]
            if jb >= i - 1:
                s_t = s_t + bias_ref[i - jb]
                s_ref[slot, rows, :] = s_t
            maxes.append(jnp.max(s_t, axis=0, keepdims=True))
        m = functools.reduce(jnp.maximum, maxes)
        for jb in range(i + 1):
            rows = slice(jb * blk, (jb + 1) * blk)
            p_ref[slot, rows, :] = jnp.exp2(s_ref[slot, rows, :] - m).astype(BF16)

    def attend(pos, i):
        acc = jnp.dot(vt_ref[:, :(i + 1) * blk], p_ref[pos % 2, :(i + 1) * blk, :],
                      preferred_element_type=F32)
        out = acc[:dh, :] / acc[dh:dh + 1, :]
        o_ref[i * blk:(i + 1) * blk, :] = out.T.astype(o_ref.dtype)

    order = list(range(nb - 1, -1, -1))
    scores(0, order[0])
    for pos, i in enumerate(order):
        if pos + 1 < nb:
            scores(pos + 1, order[pos + 1])
        softmax(pos, i)
        if pos > 0:
            attend(pos - 1, order[pos - 1])
    attend(nb - 1, order[-1])


def moba_bucket_tables():
    r = jnp.arange(MOBA_BLOCK)
    dist_own = r[None, :] - r[:, None]

    def bucket(dist):
        n = jnp.maximum(dist, 0)
        max_exact = REL_BUCKETS // 2
        nf = jnp.maximum(n, 1).astype(F32)
        large = max_exact + (jnp.log(nf / max_exact) / math.log(REL_MAX_DIST / max_exact)
                             * (REL_BUCKETS - max_exact)).astype(jnp.int32)
        large = jnp.minimum(large, REL_BUCKETS - 1)
        return jnp.where(n < max_exact, n, large)

    own = jnp.where(dist_own >= 0, bucket(dist_own), -1)
    adj = bucket(dist_own + MOBA_BLOCK)
    return jnp.stack([own, adj]).astype(jnp.int32)


def moba(proj, rel_bias, batch, side_w, side_layer):
    t = proj.shape[0]
    col = lambda off: pl.BlockSpec((SEQ, MOBA_DH), lambda h, b: (b, off + h))
    side = SideCast(side_w, side_layer, MOBA_HEADS * batch, lambda h, b: h * batch + b)
    in_specs = [pl.BlockSpec(memory_space=pltpu.SMEM),
                pl.BlockSpec((2, MOBA_BLOCK, MOBA_BLOCK), lambda h, b: (0, 0, 0)),
                col(0), col(MOBA_HEADS), col(2 * MOBA_HEADS)]
    return pl.pallas_call(
        _with_side_cast(_moba_kernel, len(in_specs), 1),
        grid=(MOBA_HEADS, batch),
        in_specs=in_specs + [side.in_spec],
        out_specs=[pl.BlockSpec((SEQ, MOBA_DH), lambda h, b: (b, h)), side.out_spec],
        out_shape=[jax.ShapeDtypeStruct((t, MOBA_HEADS * MOBA_DH), BF16), side.out_shape],
        scratch_shapes=[pltpu.VMEM((2, MOBA_BLOCK, MOBA_BLOCK), F32),
                        pltpu.VMEM((2 * MOBA_DH, SEQ), BF16),
                        pltpu.VMEM((SEQ, 2 * MOBA_DH), BF16),
                        pltpu.VMEM((MOBA_DH + 2 * SUBLANES, SEQ), BF16),
                        pltpu.VMEM((2, SEQ, MOBA_BLOCK), F32),
                        pltpu.VMEM((2, SEQ, MOBA_BLOCK), BF16)],
        compiler_params=_params("arbitrary", "arbitrary"),
        name="moba",
    )(rel_bias, moba_bucket_tables(), proj, proj, proj, side.w)


def moba_head_gain(q_norm, k_norm):
    qg = jnp.tile(q_norm * (MOBA_DH ** -0.5 * LOG2E), MOBA_HEADS)
    return jnp.concatenate([qg, jnp.tile(k_norm, MOBA_HEADS)])


def kernel(x, norm_mix, norm_ffn, ev_w_in, ret_norm, lru_conv_w, lru_conv_b, lru_w_a, lru_b_a,
           lru_w_i, lru_b_i, lru_lambda, ev_w_out, od_w_in, q_norm, k_norm, od_w_out, rel_bias,
           ffn_w_up, ffn_conv_w, ffn_conv_b, ffn_w_down):
    batch, seq, d = x.shape
    assert (seq, d) == (SEQ, D_MODEL)
    xf = x.reshape(batch * seq, d)


    proj = norm_matmul(xf, norm_mix[0], ev_w_in, 0, tm=1024, tn=1024, out_dtype=BF16)
    ret, w_out_b = retention(proj, retention_tables(), ret_norm[0], batch, ev_w_out, 0)
    w_ai = jnp.concatenate([lru_w_a[0], lru_w_i[0]], axis=-1).astype(BF16)
    b_ai = jnp.concatenate([lru_b_a[0].reshape(LRU_BLOCKS, LRU_BW),
                            lru_b_i[0].reshape(LRU_BLOCKS, LRU_BW)], axis=-1).reshape(1, 2 * LRU_W)
    lru = rglru(proj, lru_conv_w[0], lru_conv_b[0], w_ai, b_ai, lru_lambda[0], batch)
    xf, hn = proj_residual([ret, lru], w_out_b, xf, norm_ffn[0], tm=512)
    act, w_down_b = ffn_up(hn, ffn_w_up, ffn_conv_w, ffn_conv_b, 0, ffn_w_down, 0,
                           tm=2048, tn=512, sub=256)
    xf, w_in_b = matmul_residual(act, w_down_b, xf, tm=512, tn=1024, side_w=od_w_in, side_layer=0)

    proj = norm_matmul(xf, norm_mix[1], w_in_b, None, tm=1024, tn=2048, out_dtype=BF16,
                       head_gain=moba_head_gain(q_norm[0], k_norm[0]), head_dim=MOBA_DH)
    att, w_out_b = moba(proj, rel_bias, batch, od_w_out, 0)
    xf, hn = proj_residual([att], w_out_b, xf, norm_ffn[1], tm=512)
    act, w_down_b = ffn_up(hn, ffn_w_up, ffn_conv_w, ffn_conv_b, 1, ffn_w_down, 1,
                           tm=2048, tn=512, sub=256)
    xf = matmul_residual(act, w_down_b, xf, tm=512, tn=1024)
    return xf.reshape(batch, seq, d)
```

```python
import functools
import math

import jax
import jax.numpy as jnp
from jax import lax
from jax.experimental import pallas as pl
from jax.experimental.pallas import tpu as pltpu

D_MODEL = 2048
SEQ = 2048
RET_HEADS = 4
RET_DK = 256
RET_DV = 256
RET_W = RET_HEADS * RET_DK
RET_CHUNK = 128
ROPE_BASE = 10000.0
LRU_W = 1024
LRU_BLOCKS = 8
LRU_BW = 128
LRU_CONV = 4
LRU_C = 8.0
LRU_SCAN_CHUNK = 64
MOBA_HEADS = 16
MOBA_DH = 128
MOBA_W = MOBA_HEADS * MOBA_DH
MOBA_BLOCK = 256
MOBA_TOPK = 3
REL_BUCKETS = 32
REL_MAX_DIST = 128
D_FF = 5632
FFN_CONV = 3
EPS = 1e-6

VMEM_LIMIT_BYTES = 56 * 1024 * 1024
SUBLANES = 8

BF16 = jnp.bfloat16
F32 = jnp.float32
NEG_INF = float("-inf")
MASK_VALUE = -1e30
LOG2E = math.log2(math.e)


def _params(*sem):
    return pltpu.CompilerParams(dimension_semantics=sem, vmem_limit_bytes=VMEM_LIMIT_BYTES)


def _gelu_tanh(x):
    c = math.sqrt(2.0 / math.pi)
    return 0.5 * x * (1.0 + jnp.tanh(c * (x + 0.044715 * (x * x * x))))


def _rms_scale(x):
    return lax.rsqrt(jnp.mean(x * x, axis=-1, keepdims=True) + EPS)


class SideCast:
    def __init__(self, w, layer, n_steps, step_of):
        k, n = w.shape[1:]
        rows = k // n_steps
        assert rows * n_steps == k and rows % (2 * SUBLANES) == 0
        self.w = w
        self.in_spec = pl.BlockSpec((None, rows, n), lambda *ids: (layer, step_of(*ids), 0))
        self.out_spec = pl.BlockSpec((rows, n), lambda *ids: (step_of(*ids), 0))
        self.out_shape = jax.ShapeDtypeStruct((k, n), BF16)


def _with_side_cast(body, n_in, n_out):
    def wrapped(*refs):
        side_in, side_out = refs[n_in], refs[n_in + 1 + n_out]
        side_out[...] = side_in[...].astype(BF16)
        body(*refs[:n_in], *refs[n_in + 1:n_in + 1 + n_out], *refs[n_in + 2 + n_out:])
    return wrapped


def _norm_mm_kernel(*refs, extra_names, col_epilogues, sub, tiles_per_seq):
    n_extra = len(extra_names)
    x_ref, g_ref, w_ref = refs[:3]
    ex = dict(zip(extra_names, refs[3:3 + n_extra]))
    o_ref, hn_ref = refs[3 + n_extra:5 + n_extra]
    scratch = list(refs[5 + n_extra:])
    w_is_bf16 = w_ref.dtype == BF16
    wb_ref = w_ref if w_is_bf16 else scratch.pop(0)
    acc_ref = scratch.pop(0) if any(col_epilogues) else None
    halo_ref = scratch.pop(0) if ("conv",) in col_epilogues else None
    i, j = pl.program_id(0), pl.program_id(1)
    tm, tn = o_ref.shape
    n_sub = tm // sub

    def apply_epilogue(kind, a, r):
        rows = slice(r * sub, (r + 1) * sub)
        if kind[0] == "head_norm":
            heads = [a[:, s:s + kind[1]] for s in range(0, tn, kind[1])]
            return jnp.concatenate([h * _rms_scale(h) for h in heads], axis=-1) * ex["hg"][...]
        if kind[0] == "rotary":
            cos, sin = ex["cos"][rows, :], ex["sin"][rows, :]
            if kind[1] != 1.0:
                cos, sin = cos * kind[1], sin * kind[1]
            half = RET_DK // 2
            out = []
            for s in range(0, tn, RET_DK):
                x1, x2 = a[:, s:s + half], a[:, s + half:s + RET_DK]
                out += [x1 * cos - x2 * sin, x1 * sin + x2 * cos]
            return jnp.concatenate(out, axis=-1)
        if kind[0] == "silu":
            return a * jax.nn.sigmoid(a)
        if kind[0] == "gelu":
            return _gelu_tanh(a)
        if kind[0] == "conv":
            taps = ex["cw"].shape[0]
            fresh = (i % tiles_per_seq == 0) if r == 0 else False
            halo = jnp.where(fresh, 0.0, halo_ref[...])
            halo_ref[...] = a[sub - SUBLANES:, :]
            ext = jnp.concatenate([halo, a], axis=0)
            out = a * ex["cw"][taps - 1:taps, :] + ex["cb"][...]
            for d in range(1, taps):
                prev = pltpu.roll(ext, d, axis=0)[SUBLANES:, :]
                out = out + prev * ex["cw"][taps - 1 - d:taps - d, :]
            return out
        raise ValueError(kind)

    def plain():
        o_ref[...] = jnp.dot(hn_ref[...], w_ref[...].astype(BF16),
                             preferred_element_type=F32).astype(o_ref.dtype)

    def staged(norm_rows, kind):
        if not w_is_bf16:
            wb_ref[...] = w_ref[...].astype(BF16)

        def norm(r):
            x = x_ref[r * sub:(r + 1) * sub, :]
            hn_ref[r * sub:(r + 1) * sub, :] = (x * _rms_scale(x) * g_ref[...]).astype(BF16)

        def mm(r):
            acc = jnp.dot(hn_ref[r * sub:(r + 1) * sub, :], wb_ref[...],
                          preferred_element_type=F32)
            if kind:
                acc_ref[r % 2] = acc
            else:
                o_ref[r * sub:(r + 1) * sub, :] = acc.astype(o_ref.dtype)

        def epilogue(r):
            o_ref[r * sub:(r + 1) * sub, :] = apply_epilogue(
                kind, acc_ref[r % 2], r).astype(o_ref.dtype)

        if norm_rows:
            norm(0)
        for r in range(n_sub):
            if norm_rows and r + 1 < n_sub:
                norm(r + 1)
            mm(r)
            if kind and r > 0:
                epilogue(r - 1)
        if kind:
            epilogue(n_sub - 1)

    for col, kind in enumerate(col_epilogues):
        if col == 0 or kind:
            pl.when(j == col)(functools.partial(staged, col == 0, kind))
        else:
            pl.when(j == col)(plain)


def norm_matmul(x, g, w, layer, *, tm, tn, out_dtype, col_epilogues, extras=None, sub=256):
    t, d = x.shape
    n = w.shape[-1]
    assert len(col_epilogues) == n // tn
    if layer is None:
        assert w.dtype == BF16
        w_spec = pl.BlockSpec((d, tn), lambda i, j: (0, j))
    else:
        w_spec = pl.BlockSpec((None, d, tn), lambda i, j: (layer, 0, j))
    extras = extras or {}
    scratch = [pltpu.VMEM((tm, d), BF16)]
    if layer is not None:
        scratch.append(pltpu.VMEM((d, tn), BF16))
    if any(col_epilogues):
        scratch.append(pltpu.VMEM((2, sub, tn), F32))
    if ("conv",) in col_epilogues:
        scratch.append(pltpu.VMEM((SUBLANES, tn), F32))
    return pl.pallas_call(
        functools.partial(_norm_mm_kernel, extra_names=tuple(extras),
                          col_epilogues=tuple(col_epilogues), sub=sub, tiles_per_seq=SEQ // tm),
        grid=(t // tm, n // tn),
        in_specs=[
            pl.BlockSpec((tm, d), lambda i, j: (i, 0)),
            pl.BlockSpec((1, d), lambda i, j: (0, 0)),
            w_spec,
        ] + [pl.BlockSpec(block, index_map) for _, block, index_map in extras.values()],
        out_specs=pl.BlockSpec((tm, tn), lambda i, j: (i, j)),
        out_shape=jax.ShapeDtypeStruct((t, n), out_dtype),
        scratch_shapes=scratch,
        compiler_params=_params("arbitrary", "arbitrary"),
        name="norm_matmul",
    )(x, g.reshape(1, d), w, *[arr for arr, _, _ in extras.values()])


def _retention_kernel(q_ref, k_ref, v_ref, g_ref, decay_ref, zeta_ref, xi_ref,
                      gch_ref, rn_ref, o_ref, kv_ref, st_ref):
    n_chunks = q_ref.shape[0] // RET_CHUNK
    chunk_rows = lambda n: slice(n * RET_CHUNK, (n + 1) * RET_CHUNK)

    for n in range(n_chunks):
        rows = chunk_rows(n)
        kz = (k_ref[rows, :].astype(F32) * zeta_ref[0]).astype(BF16)
        kv_ref[n] = lax.dot_general(kz, v_ref[rows, :], (((0,), (0,)), ((), ())),
                                    preferred_element_type=F32)

    state = jnp.zeros((RET_DK, RET_DV), F32)
    for n in range(n_chunks):
        st_ref[n] = state.astype(BF16)
        state = gch_ref[0] * state + kv_ref[n]

    for n in range(n_chunks):
        rows = chunk_rows(n)
        qb, vb = q_ref[rows, :], v_ref[rows, :]
        scores = lax.dot_general(qb, k_ref[rows, :], (((1,), (1,)), ((), ())),
                                 preferred_element_type=F32) * decay_ref[0]
        inner = jnp.dot(scores.astype(BF16), vb, preferred_element_type=F32)
        cross = jnp.dot(qb, st_ref[n], preferred_element_type=F32) * xi_ref[0]
        out = inner + cross
        out = out * _rms_scale(out) * rn_ref[0]
        o_ref[rows, :] = (out * g_ref[rows, :].astype(F32)).astype(o_ref.dtype)


def retention(proj, tables, ret_norm, batch, side_w, side_layer):
    decay, zeta, xi, gch = tables
    t = proj.shape[0]
    hq = RET_W // RET_DK
    col = lambda off: pl.BlockSpec((SEQ, RET_DK), lambda b, h: (b, off + h))
    per_head = lambda shape: pl.BlockSpec((1,) + shape, lambda b, h: (h, 0, 0))
    side = SideCast(side_w, side_layer, batch * RET_HEADS, lambda b, h: b * RET_HEADS + h)
    in_specs = [col(0), col(hq), col(2 * hq), col(3 * hq),
                per_head((RET_CHUNK, RET_CHUNK)), per_head((RET_CHUNK, RET_DK)),
                per_head((RET_CHUNK, RET_DV)), per_head((1, RET_DV)), per_head((1, RET_DV))]
    return pl.pallas_call(
        _with_side_cast(_retention_kernel, len(in_specs), 1),
        grid=(batch, RET_HEADS),
        in_specs=in_specs + [side.in_spec],
        out_specs=[pl.BlockSpec((SEQ, RET_DV), lambda b, h: (b, h)), side.out_spec],
        out_shape=[jax.ShapeDtypeStruct((t, RET_HEADS * RET_DV), BF16), side.out_shape],
        scratch_shapes=[pltpu.VMEM((SEQ // RET_CHUNK, RET_DK, RET_DV), F32),
                        pltpu.VMEM((SEQ // RET_CHUNK, RET_DK, RET_DV), BF16)],
        compiler_params=_params("arbitrary", "arbitrary"),
        name="retention",
    )(proj, proj, proj, proj, decay, zeta, xi, gch,
      ret_norm.reshape(RET_HEADS, 1, RET_DV), side.w)


def rotary_tables():
    half = RET_DK // 2
    pos = jnp.arange(SEQ)
    freqs = ROPE_BASE ** (-jnp.arange(half, dtype=F32) / half)
    ang = pos.astype(F32)[:, None] * freqs[None, :]
    return jnp.cos(ang), jnp.sin(ang)


def retention_tables():
    log_g = jnp.log1p(-jnp.exp2(-5.0 - jnp.arange(RET_HEADS, dtype=F32)))
    i = jnp.arange(RET_CHUNK)
    diff = i[:, None] - i[None, :]
    decay = jnp.where(diff >= 0, jnp.exp(log_g[:, None, None] * jnp.maximum(diff, 0)), 0.0)
    zeta = jnp.exp(log_g[None, :] * (RET_CHUNK - 1 - i)[:, None])
    xi = jnp.exp(log_g[None, :] * (i + 1)[:, None])
    g_chunk = jnp.exp(log_g * RET_CHUNK)
    zeta_b = jnp.broadcast_to(zeta.T[:, :, None], (RET_HEADS, RET_CHUNK, RET_DK))
    xi_b = jnp.broadcast_to(xi.T[:, :, None], (RET_HEADS, RET_CHUNK, RET_DV))
    gch_b = jnp.broadcast_to(g_chunk[:, None, None], (RET_HEADS, 1, RET_DV))
    return decay, zeta_b, xi_b, gch_b


def _shift_rows(v, d, fill, row):
    if d % SUBLANES == 0:
        return jnp.concatenate([jnp.full((d, v.shape[1]), fill, v.dtype), v[:-d]], axis=0)
    return jnp.where(row >= d, pltpu.roll(v, d, axis=0), fill)


def _rglru_kernel(xc_ref, gg_ref, wai_ref, bai_ref, lam_ref, o_ref, a_ref, u_ref):
    s = xc_ref.shape[0]
    xc = xc_ref[...].astype(F32)
    y = jnp.dot(xc_ref[...], wai_ref[0], preferred_element_type=F32) + bai_ref[...]
    r = jax.nn.sigmoid(y[:, :LRU_BW])
    gi = jax.nn.sigmoid(y[:, LRU_BW:])
    neg_lam = -lam_ref[...]
    softplus = jnp.maximum(neg_lam, 0.0) + jnp.log1p(jnp.exp(-jnp.abs(neg_lam)))
    log_a = (-LRU_C) * r * softplus
    a = jnp.exp(log_a)
    a_ref[...] = a
    u_ref[...] = jnp.sqrt(1.0 - a * a) * (gi * xc)

    c = LRU_SCAN_CHUNK
    crow = lax.broadcasted_iota(jnp.int32, (c, LRU_BW), 0)

    def chunk(n, h_prev):
        rows = pl.ds(pl.multiple_of(n * c, c), c)
        av, bv = a_ref[rows, :], u_ref[rows, :]
        d = 1
        while d < c:
            a_sh = _shift_rows(av, d, 1.0, crow)
            b_sh = _shift_rows(bv, d, 0.0, crow)
            bv = av * b_sh + bv
            av = av * a_sh
            d *= 2
        h = bv + av * h_prev
        o_ref[rows, :] = (h * gg_ref[rows, :].astype(F32)).astype(o_ref.dtype)
        return h[c - 1:c, :]

    lax.fori_loop(0, s // c, chunk, jnp.zeros((1, LRU_BW), F32))


def rglru(proj, w_ai, b_ai, lam, batch):
    t = proj.shape[0]
    x_off = 4 * RET_W // LRU_BW
    g_off = x_off + LRU_W // LRU_BW
    vec = lambda rows: pl.BlockSpec((rows, LRU_BW), lambda b, g: (0, g))
    return pl.pallas_call(
        _rglru_kernel,
        grid=(batch, LRU_BLOCKS),
        in_specs=[
            pl.BlockSpec((SEQ, LRU_BW), lambda b, g: (b, x_off + g)),
            pl.BlockSpec((SEQ, LRU_BW), lambda b, g: (b, g_off + g)),
            pl.BlockSpec((1, LRU_BW, 2 * LRU_BW), lambda b, g: (g, 0, 0)),
            pl.BlockSpec((1, 2 * LRU_BW), lambda b, g: (0, g)),
            vec(1),
        ],
        out_specs=pl.BlockSpec((SEQ, LRU_BW), lambda b, g: (b, g)),
        out_shape=jax.ShapeDtypeStruct((t, LRU_W), BF16),
        scratch_shapes=[pltpu.VMEM((SEQ, LRU_BW), F32), pltpu.VMEM((SEQ, LRU_BW), F32)],
        compiler_params=_params("arbitrary", "arbitrary"),
        name="rglru",
    )(proj, proj, w_ai, b_ai, lam.reshape(1, LRU_W))


def _proj_res_kernel(*refs, splits, sub):
    part_refs = refs[:len(splits)]
    w_ref, x_ref, g_ref, o_ref, hn_ref, acc_ref = refs[len(splits):]

    def mm(r):
        rows = slice(r * sub, (r + 1) * sub)
        acc, off = None, 0
        for a_ref, k in zip(part_refs, splits):
            part = jnp.dot(a_ref[rows, :], w_ref[off:off + k, :], preferred_element_type=F32)
            acc = part if acc is None else acc + part
            off += k
        acc_ref[r % 2] = acc

    def epilogue(r):
        rows = slice(r * sub, (r + 1) * sub)
        y = x_ref[rows, :] + acc_ref[r % 2]
        o_ref[rows, :] = y
        hn_ref[rows, :] = (y * _rms_scale(y) * g_ref[...]).astype(BF16)

    n_sub = x_ref.shape[0] // sub
    mm(0)
    for r in range(n_sub):
        if r + 1 < n_sub:
            mm(r + 1)
        epilogue(r)


def proj_residual(parts, w, x, g, *, tm, sub=256):
    t, n = x.shape
    splits = tuple(p.shape[1] for p in parts)
    row = lambda i: (i, 0)
    return pl.pallas_call(
        functools.partial(_proj_res_kernel, splits=splits, sub=sub),
        grid=(t // tm,),
        in_specs=[pl.BlockSpec((tm, k), row) for k in splits] + [
            pl.BlockSpec(w.shape, lambda i: (0, 0)),
            pl.BlockSpec((tm, n), row),
            pl.BlockSpec((1, n), lambda i: (0, 0)),
        ],
        out_specs=[pl.BlockSpec((tm, n), row), pl.BlockSpec((tm, n), row)],
        out_shape=[jax.ShapeDtypeStruct((t, n), F32), jax.ShapeDtypeStruct((t, n), BF16)],
        scratch_shapes=[pltpu.VMEM((2, sub, n), F32)],
        compiler_params=_params("arbitrary"),
        name="proj_residual",
    )(*parts, w, x, g.reshape(1, n))


def _ffn_up_kernel(hn_ref, wg_ref, wv_ref, cwg_ref, cwv_ref, cbg_ref, cbv_ref, o_ref,
                   wb_ref, u_ref, carry_ref, *, tiles_per_seq, sub):
    i = pl.program_id(1)
    tm = hn_ref.shape[0]
    tn = wg_ref.shape[1]

    @pl.when(i == 0)
    def _():
        wb_ref[:, :tn] = wg_ref[...].astype(BF16)
        wb_ref[:, tn:] = wv_ref[...].astype(BF16)

    def up(r):
        u_ref[r % 2, SUBLANES:, :] = jnp.dot(hn_ref[r * sub:(r + 1) * sub, :], wb_ref[...],
                                             preferred_element_type=F32)

    def conv(slot, cols, cw_ref, cb_ref):
        out = u_ref[slot, SUBLANES:, cols] * cw_ref[FFN_CONV - 1:FFN_CONV, :] + cb_ref[...]
        for d in range(1, FFN_CONV):
            prev = u_ref[slot, SUBLANES - d:SUBLANES - d + sub, cols]
            out = out + prev * cw_ref[FFN_CONV - 1 - d:FFN_CONV - d, :]
        return out

    seq_start = (i % tiles_per_seq) == 0
    u_ref[0, :SUBLANES, :] = jnp.where(seq_start, 0.0, carry_ref[...])
    up(0)
    for r in range(tm // sub):
        slot = r % 2
        u_ref[1 - slot, :SUBLANES, :] = u_ref[slot, sub:, :]
        if (r + 1) * sub < tm:
            up(r + 1)
        gate = conv(slot, slice(0, tn), cwg_ref, cbg_ref)
        val = conv(slot, slice(tn, 2 * tn), cwv_ref, cbv_ref)
        o_ref[r * sub:(r + 1) * sub, :] = (_gelu_tanh(gate) * val).astype(o_ref.dtype)
    carry_ref[...] = u_ref[(tm // sub) % 2, :SUBLANES, :]


def ffn_up(hn, w_up, conv_w, conv_b, layer, side_w, side_layer, *, tm, tn, sub):
    t, d = hn.shape
    nj, ni = D_FF // tn, t // tm
    cb = conv_b.reshape(conv_b.shape[0], 1, 2 * D_FF)
    per_col = lambda rows, off: pl.BlockSpec((None, rows, tn), lambda j, i: (layer, 0, off + j))
    side = SideCast(side_w, side_layer, nj * ni, lambda j, i: j * ni + i)
    in_specs = [pl.BlockSpec((tm, d), lambda j, i: (i, 0)),
                per_col(d, 0), per_col(d, nj),
                per_col(FFN_CONV, 0), per_col(FFN_CONV, nj),
                per_col(1, 0), per_col(1, nj)]
    kern = functools.partial(_ffn_up_kernel, tiles_per_seq=SEQ // tm, sub=sub)
    return pl.pallas_call(
        _with_side_cast(kern, len(in_specs), 1),
        grid=(nj, ni),
        in_specs=in_specs + [side.in_spec],
        out_specs=[pl.BlockSpec((tm, tn), lambda j, i: (i, j)), side.out_spec],
        out_shape=[jax.ShapeDtypeStruct((t, D_FF), BF16), side.out_shape],
        scratch_shapes=[pltpu.VMEM((d, 2 * tn), BF16),
                        pltpu.VMEM((2, SUBLANES + sub, 2 * tn), F32),
                        pltpu.VMEM((SUBLANES, 2 * tn), F32)],
        compiler_params=_params("arbitrary", "arbitrary"),
        name="ffn_up",
    )(hn, w_up, w_up, conv_w, conv_w, cb, cb, side.w)


def _mm_res_kernel(a_ref, w_ref, x_ref, o_ref):
    o_ref[...] = x_ref[...] + jnp.dot(a_ref[...], w_ref[...], preferred_element_type=F32)


def matmul_residual(a, w, x, *, tm, tn, side_w=None, side_layer=None):
    t, k = a.shape
    n = w.shape[1]
    nj, ni = n // tn, t // tm
    in_specs = [pl.BlockSpec((tm, k), lambda j, i: (i, 0)),
                pl.BlockSpec((k, tn), lambda j, i: (0, j)),
                pl.BlockSpec((tm, tn), lambda j, i: (i, j))]
    out_specs = [pl.BlockSpec((tm, tn), lambda j, i: (i, j))]
    out_shape = [jax.ShapeDtypeStruct((t, n), F32)]
    operands = [a, w, x]
    kern = _mm_res_kernel
    if side_w is not None:
        side = SideCast(side_w, side_layer, nj * ni, lambda j, i: j * ni + i)
        kern = _with_side_cast(kern, len(in_specs), 1)
        in_specs.append(side.in_spec)
        out_specs.append(side.out_spec)
        out_shape.append(side.out_shape)
        operands.append(side.w)
    outs = pl.pallas_call(
        kern,
        grid=(nj, ni),
        in_specs=in_specs,
        out_specs=out_specs,
        out_shape=out_shape,
        compiler_params=_params("arbitrary", "arbitrary"),
        name="matmul_residual",
    )(*operands)
    return outs if side_w is not None else outs[0]


def _moba_kernel(relb_ref, bkt_ref, q_ref, k_ref, v_ref, o_ref,
                 bias_ref, qt_ref, ks_ref, vt_ref, s_ref, p_ref):
    h, b = pl.program_id(0), pl.program_id(1)
    nb = SEQ // MOBA_BLOCK
    blk = MOBA_BLOCK
    dh = MOBA_DH

    @pl.when((h == 0) & (b == 0))
    def _():
        key_blk = lax.broadcasted_iota(jnp.int32, (SEQ, dh), 0) // blk
        lane = lax.broadcasted_iota(jnp.int32, (SEQ, dh), 1)
        onehot = (lane < 2 * nb) & ((lane % nb) == key_blk)
        ks_ref[:, dh:] = jnp.where(onehot, 1.0, 0.0).astype(BF16)
        qt_ref[dh:, :] = jnp.zeros((dh, SEQ), BF16)
        vt_ref[dh:, :] = jnp.ones((vt_ref.shape[0] - dh, SEQ), BF16)

    @pl.when(b == 0)
    def _():
        for m in range(2):
            bk = bkt_ref[m]
            acc = jnp.full(bk.shape, NEG_INF, F32)
            for t in range(REL_BUCKETS):
                acc = jnp.where(bk == t, relb_ref[t, h] * LOG2E, acc)
            bias_ref[m] = acc

    far_bias = relb_ref[REL_BUCKETS - 1, h] * LOG2E

    qt_ref[:dh, :] = q_ref[...].astype(F32).T.astype(BF16)
    ks_ref[:, :dh] = k_ref[...]
    vt_ref[:dh, :] = v_ref[...].astype(F32).T.astype(BF16)

    blk_id = lax.broadcasted_iota(jnp.int32, (nb, SEQ), 0)
    q_blk = lax.broadcasted_iota(jnp.int32, (nb, SEQ), 1) // blk
    averager = jnp.where(blk_id == q_blk, 1.0 / blk, 0.0).astype(BF16)
    kmean = jnp.dot(averager, k_ref[...], preferred_element_type=F32)
    gate = jnp.dot(kmean.astype(BF16), qt_ref[:dh, :], preferred_element_type=F32)
    valid = blk_id < q_blk
    gm = jnp.where(valid, gate, NEG_INF)
    cnt = jnp.zeros(gate.shape, F32)
    for s in range(1, nb):
        gs = pltpu.roll(gm, s, axis=0)
        tie = jnp.where(((blk_id - s) % nb) < blk_id, 1.0, 0.0)
        cnt = cnt + jnp.where(gs > gm, 1.0, jnp.where(gs == gm, tie, 0.0))
    picked = jnp.where(blk_id < q_blk - 1, far_bias, 0.0)
    row = jnp.where(valid, jnp.where(cnt < MOBA_TOPK, picked, MASK_VALUE), 0.0)
    row_hi = row.astype(BF16).astype(F32)
    row_lo = jnp.where(row > 0.5 * MASK_VALUE, row - row_hi, 0.0)
    qt_ref[dh:dh + 2 * nb, :] = jnp.concatenate([row_hi, row_lo], axis=0).astype(BF16)

    def scores(pos, i):
        s_ref[pos % 2, :(i + 1) * blk, :] = jnp.dot(
            ks_ref[:(i + 1) * blk, :], qt_ref[:, i * blk:(i + 1) * blk],
            preferred_element_type=F32)

    def softmax(pos, i):
        slot = pos % 2
        maxes = []
        for jb in range(i + 1):
            rows = slice(jb * blk, (jb + 1) * blk)
            s_t = s_ref[slot, rows, :]
            if jb >= i - 1:
                s_t = s_t + bias_ref[i - jb]
                s_ref[slot, rows, :] = s_t
            maxes.append(jnp.max(s_t, axis=0, keepdims=True))
        m = functools.reduce(jnp.maximum, maxes)
        for jb in range(i + 1):
            rows = slice(jb * blk, (jb + 1) * blk)
            p_ref[slot, rows, :] = jnp.exp2(s_ref[slot, rows, :] - m).astype(BF16)

    def attend(pos, i):
        acc = jnp.dot(vt_ref[:, :(i + 1) * blk], p_ref[pos % 2, :(i + 1) * blk, :],
                      preferred_element_type=F32)
        out = acc[:dh, :] / acc[dh:dh + 1, :]
        o_ref[i * blk:(i + 1) * blk, :] = out.T.astype(o_ref.dtype)

    order = list(range(nb - 1, -1, -1))
    scores(0, order[0])
    for pos, i in enumerate(order):
        if pos + 1 < nb:
            scores(pos + 1, order[pos + 1])
        softmax(pos, i)
        if pos > 0:
            attend(pos - 1, order[pos - 1])
    attend(nb - 1, order[-1])


def moba_bucket_tables():
    r = jnp.arange(MOBA_BLOCK)
    dist_own = r[None, :] - r[:, None]

    def bucket(dist):
        n = jnp.maximum(dist, 0)
        max_exact = REL_BUCKETS // 2
        nf = jnp.maximum(n, 1).astype(F32)
        large = max_exact + (jnp.log(nf / max_exact) / math.log(REL_MAX_DIST / max_exact)
                             * (REL_BUCKETS - max_exact)).astype(jnp.int32)
        large = jnp.minimum(large, REL_BUCKETS - 1)
        return jnp.where(n < max_exact, n, large)

    own = jnp.where(dist_own >= 0, bucket(dist_own), -1)
    adj = bucket(dist_own + MOBA_BLOCK)
    return jnp.stack([own, adj]).astype(jnp.int32)


def moba(proj, rel_bias, batch, side_w, side_layer):
    t = proj.shape[0]
    col = lambda off: pl.BlockSpec((SEQ, MOBA_DH), lambda h, b: (b, off + h))
    side = SideCast(side_w, side_layer, MOBA_HEADS * batch, lambda h, b: h * batch + b)
    in_specs = [pl.BlockSpec(memory_space=pltpu.SMEM),
                pl.BlockSpec((2, MOBA_BLOCK, MOBA_BLOCK), lambda h, b: (0, 0, 0)),
                col(0), col(MOBA_HEADS), col(2 * MOBA_HEADS)]
    return pl.pallas_call(
        _with_side_cast(_moba_kernel, len(in_specs), 1),
        grid=(MOBA_HEADS, batch),
        in_specs=in_specs + [side.in_spec],
        out_specs=[pl.BlockSpec((SEQ, MOBA_DH), lambda h, b: (b, h)), side.out_spec],
        out_shape=[jax.ShapeDtypeStruct((t, MOBA_HEADS * MOBA_DH), BF16), side.out_shape],
        scratch_shapes=[pltpu.VMEM((2, MOBA_BLOCK, MOBA_BLOCK), F32),
                        pltpu.VMEM((2 * MOBA_DH, SEQ), BF16),
                        pltpu.VMEM((SEQ, 2 * MOBA_DH), BF16),
                        pltpu.VMEM((MOBA_DH + 2 * SUBLANES, SEQ), BF16),
                        pltpu.VMEM((2, SEQ, MOBA_BLOCK), F32),
                        pltpu.VMEM((2, SEQ, MOBA_BLOCK), BF16)],
        compiler_params=_params("arbitrary", "arbitrary"),
        name="moba",
    )(rel_bias, moba_bucket_tables(), proj, proj, proj, side.w)


def moba_head_gain(q_norm, k_norm):
    qg = jnp.tile(q_norm * (MOBA_DH ** -0.5 * LOG2E), MOBA_HEADS)
    return jnp.concatenate([qg, jnp.tile(k_norm, MOBA_HEADS)])


def kernel(x, norm_mix, norm_ffn, ev_w_in, ret_norm, lru_conv_w, lru_conv_b, lru_w_a, lru_b_a,
           lru_w_i, lru_b_i, lru_lambda, ev_w_out, od_w_in, q_norm, k_norm, od_w_out, rel_bias,
           ffn_w_up, ffn_conv_w, ffn_conv_b, ffn_w_down):
    batch, seq, d = x.shape
    assert (seq, d) == (SEQ, D_MODEL)
    xf = x.reshape(batch * seq, d)


    tm_in = 1024
    cos, sin = rotary_tables()
    pos_rows = lambda i, j: (i % (SEQ // tm_in), 0)
    fixed = lambda i, j: (0, 0)
    proj = norm_matmul(
        xf, norm_mix[0], ev_w_in, 0, tm=tm_in, tn=RET_W, out_dtype=BF16,
        col_epilogues=[("rotary", 1.0), ("rotary", RET_DK ** -0.5), None, ("silu",), ("conv",),
                       ("gelu",)],
        extras={"cos": (cos, (tm_in, RET_DK // 2), pos_rows),
                "sin": (sin, (tm_in, RET_DK // 2), pos_rows),
                "cw": (lru_conv_w[0], (LRU_CONV, LRU_W), fixed),
                "cb": (lru_conv_b[0].reshape(1, LRU_W), (1, LRU_W), fixed)})
    ret, w_out_b = retention(proj, retention_tables(), ret_norm[0], batch, ev_w_out, 0)
    w_ai = jnp.concatenate([lru_w_a[0], lru_w_i[0]], axis=-1).astype(BF16)
    b_ai = jnp.concatenate([lru_b_a[0].reshape(LRU_BLOCKS, LRU_BW),
                            lru_b_i[0].reshape(LRU_BLOCKS, LRU_BW)], axis=-1).reshape(1, 2 * LRU_W)
    lru = rglru(proj, w_ai, b_ai, lru_lambda[0], batch)
    xf, hn = proj_residual([ret, lru], w_out_b, xf, norm_ffn[0], tm=512)
    act, w_down_b = ffn_up(hn, ffn_w_up, ffn_conv_w, ffn_conv_b, 0, ffn_w_down, 0,
                           tm=2048, tn=512, sub=256)
    xf, w_in_b = matmul_residual(act, w_down_b, xf, tm=512, tn=1024, side_w=od_w_in, side_layer=0)

    head_norm = ("head_norm", MOBA_DH)
    proj = norm_matmul(
        xf, norm_mix[1], w_in_b, None, tm=1024, tn=MOBA_W, out_dtype=BF16,
        col_epilogues=[head_norm, head_norm, None],
        extras={"hg": (moba_head_gain(q_norm[0], k_norm[0]).reshape(1, 2 * MOBA_W),
                       (1, MOBA_W), lambda i, j: (0, jnp.minimum(j, 1)))})
    att, w_out_b = moba(proj, rel_bias, batch, od_w_out, 0)
    xf, hn = proj_residual([att], w_out_b, xf, norm_ffn[1], tm=512)
    act, w_down_b = ffn_up(hn, ffn_w_up, ffn_conv_w, ffn_conv_b, 1, ffn_w_down, 1,
                           tm=2048, tn=512, sub=256)
    xf = matmul_residual(act, w_down_b, xf, tm=512, tn=1024)
    return xf.reshape(batch, seq, d)
```

```python
import functools
import math

import jax
import jax.numpy as jnp
from jax import lax
from jax.experimental import pallas as pl
from jax.experimental.pallas import tpu as pltpu

D_MODEL = 2048
SEQ = 2048
RET_HEADS = 4
RET_DK = 256
RET_DV = 256
RET_W = RET_HEADS * RET_DK
RET_CHUNK = 128
ROPE_BASE = 10000.0
LRU_W = 1024
LRU_BLOCKS = 8
LRU_BW = 128
LRU_CONV = 4
LRU_C = 8.0
LRU_SCAN_CHUNK = 64
MOBA_HEADS = 16
MOBA_DH = 128
MOBA_W = MOBA_HEADS * MOBA_DH
MOBA_HEADS_PER_STEP = 2
MOBA_BLOCK = 256
MOBA_TOPK = 3
REL_BUCKETS = 32
REL_MAX_DIST = 128
D_FF = 5632
FFN_CONV = 3
EPS = 1e-6

VMEM_LIMIT_BYTES = 56 * 1024 * 1024
SUBLANES = 8

BF16 = jnp.bfloat16
F32 = jnp.float32
NEG_INF = float("-inf")
MASK_VALUE = -1e30
LOG2E = math.log2(math.e)


def _params(*sem):
    return pltpu.CompilerParams(dimension_semantics=sem, vmem_limit_bytes=VMEM_LIMIT_BYTES)


def _gelu_tanh(x):
    c = math.sqrt(2.0 / math.pi)
    return 0.5 * x * (1.0 + jnp.tanh(c * (x + 0.044715 * (x * x * x))))


def _rms_scale(x):
    return lax.rsqrt(jnp.mean(x * x, axis=-1, keepdims=True) + EPS)


class SideCast:
    def __init__(self, w, layer, n_steps, step_of):
        k, n = w.shape[1:]
        rows = k // n_steps
        assert rows * n_steps == k and rows % (2 * SUBLANES) == 0
        self.w = w
        self.in_spec = pl.BlockSpec((None, rows, n), lambda *ids: (layer, step_of(*ids), 0))
        self.out_spec = pl.BlockSpec((rows, n), lambda *ids: (step_of(*ids), 0))
        self.out_shape = jax.ShapeDtypeStruct((k, n), BF16)


def _with_side_cast(body, n_in, n_out):
    def wrapped(*refs):
        side_in, side_out = refs[n_in], refs[n_in + 1 + n_out]
        side_out[...] = side_in[...].astype(BF16)
        body(*refs[:n_in], *refs[n_in + 1:n_in + 1 + n_out], *refs[n_in + 2 + n_out:])
    return wrapped


def _norm_mm_kernel(*refs, extra_names, col_epilogues, sub, tiles_per_seq):
    n_extra = len(extra_names)
    x_ref, g_ref, w_ref = refs[:3]
    ex = dict(zip(extra_names, refs[3:3 + n_extra]))
    o_ref, hn_ref = refs[3 + n_extra:5 + n_extra]
    scratch = list(refs[5 + n_extra:])
    w_is_bf16 = w_ref.dtype == BF16
    wb_ref = w_ref if w_is_bf16 else scratch.pop(0)
    acc_ref = scratch.pop(0) if any(col_epilogues) else None
    halo_ref = scratch.pop(0) if ("conv",) in col_epilogues else None
    i, j = pl.program_id(0), pl.program_id(1)
    tm, tn = o_ref.shape
    n_sub = tm // sub

    def apply_epilogue(kind, a, r):
        rows = slice(r * sub, (r + 1) * sub)
        if kind[0] == "head_norm":
            heads = [a[:, s:s + kind[1]] for s in range(0, tn, kind[1])]
            return jnp.concatenate([h * _rms_scale(h) for h in heads], axis=-1) * ex["hg"][...]
        if kind[0] == "rotary":
            cos, sin = ex["cos"][rows, :], ex["sin"][rows, :]
            if kind[1] != 1.0:
                cos, sin = cos * kind[1], sin * kind[1]
            half = RET_DK // 2
            out = []
            for s in range(0, tn, RET_DK):
                x1, x2 = a[:, s:s + half], a[:, s + half:s + RET_DK]
                out += [x1 * cos - x2 * sin, x1 * sin + x2 * cos]
            return jnp.concatenate(out, axis=-1)
        if kind[0] == "silu":
            return a * jax.nn.sigmoid(a)
        if kind[0] == "gelu":
            return _gelu_tanh(a)
        if kind[0] == "conv":
            taps = ex["cw"].shape[0]
            fresh = (i % tiles_per_seq == 0) if r == 0 else False
            halo = jnp.where(fresh, 0.0, halo_ref[...])
            halo_ref[...] = a[sub - SUBLANES:, :]
            ext = jnp.concatenate([halo, a], axis=0)
            out = a * ex["cw"][taps - 1:taps, :] + ex["cb"][...]
            for d in range(1, taps):
                prev = pltpu.roll(ext, d, axis=0)[SUBLANES:, :]
                out = out + prev * ex["cw"][taps - 1 - d:taps - d, :]
            return out
        raise ValueError(kind)

    def plain():
        o_ref[...] = jnp.dot(hn_ref[...], w_ref[...].astype(BF16),
                             preferred_element_type=F32).astype(o_ref.dtype)

    def staged(norm_rows, kind):
        if not w_is_bf16:
            wb_ref[...] = w_ref[...].astype(BF16)

        def norm(r):
            x = x_ref[r * sub:(r + 1) * sub, :]
            hn_ref[r * sub:(r + 1) * sub, :] = (x * _rms_scale(x) * g_ref[...]).astype(BF16)

        def mm(r):
            acc = jnp.dot(hn_ref[r * sub:(r + 1) * sub, :], wb_ref[...],
                          preferred_element_type=F32)
            if kind:
                acc_ref[r % 2] = acc
            else:
                o_ref[r * sub:(r + 1) * sub, :] = acc.astype(o_ref.dtype)

        def epilogue(r):
            o_ref[r * sub:(r + 1) * sub, :] = apply_epilogue(
                kind, acc_ref[r % 2], r).astype(o_ref.dtype)

        if norm_rows:
            norm(0)
        for r in range(n_sub):
            if norm_rows and r + 1 < n_sub:
                norm(r + 1)
            mm(r)
            if kind and r > 0:
                epilogue(r - 1)
        if kind:
            epilogue(n_sub - 1)

    for col, kind in enumerate(col_epilogues):
        if col == 0 or kind:
            pl.when(j == col)(functools.partial(staged, col == 0, kind))
        else:
            pl.when(j == col)(plain)


def norm_matmul(x, g, w, layer, *, tm, tn, out_dtype, col_epilogues, extras=None, sub=256):
    t, d = x.shape
    n = w.shape[-1]
    assert len(col_epilogues) == n // tn
    if layer is None:
        assert w.dtype == BF16
        w_spec = pl.BlockSpec((d, tn), lambda i, j: (0, j))
    else:
        w_spec = pl.BlockSpec((None, d, tn), lambda i, j: (layer, 0, j))
    extras = extras or {}
    scratch = [pltpu.VMEM((tm, d), BF16)]
    if layer is not None:
        scratch.append(pltpu.VMEM((d, tn), BF16))
    if any(col_epilogues):
        scratch.append(pltpu.VMEM((2, sub, tn), F32))
    if ("conv",) in col_epilogues:
        scratch.append(pltpu.VMEM((SUBLANES, tn), F32))
    return pl.pallas_call(
        functools.partial(_norm_mm_kernel, extra_names=tuple(extras),
                          col_epilogues=tuple(col_epilogues), sub=sub, tiles_per_seq=SEQ // tm),
        grid=(t // tm, n // tn),
        in_specs=[
            pl.BlockSpec((tm, d), lambda i, j: (i, 0)),
            pl.BlockSpec((1, d), lambda i, j: (0, 0)),
            w_spec,
        ] + [pl.BlockSpec(block, index_map) for _, block, index_map in extras.values()],
        out_specs=pl.BlockSpec((tm, tn), lambda i, j: (i, j)),
        out_shape=jax.ShapeDtypeStruct((t, n), out_dtype),
        scratch_shapes=scratch,
        compiler_params=_params("arbitrary", "arbitrary"),
        name="norm_matmul",
    )(x, g.reshape(1, d), w, *[arr for arr, _, _ in extras.values()])


def _retention_kernel(q_ref, k_ref, v_ref, g_ref, decay_ref, zeta_ref, xi_ref,
                      gch_ref, rn_ref, o_ref, kv_ref, st_ref):
    n_chunks = q_ref.shape[0] // RET_CHUNK
    chunk_rows = lambda n: slice(n * RET_CHUNK, (n + 1) * RET_CHUNK)

    for n in range(n_chunks):
        rows = chunk_rows(n)
        kz = (k_ref[rows, :].astype(F32) * zeta_ref[0]).astype(BF16)
        kv_ref[n] = lax.dot_general(kz, v_ref[rows, :], (((0,), (0,)), ((), ())),
                                    preferred_element_type=F32)

    state = jnp.zeros((RET_DK, RET_DV), F32)
    for n in range(n_chunks):
        st_ref[n] = state.astype(BF16)
        state = gch_ref[0] * state + kv_ref[n]

    for n in range(n_chunks):
        rows = chunk_rows(n)
        qb, vb = q_ref[rows, :], v_ref[rows, :]
        scores = lax.dot_general(qb, k_ref[rows, :], (((1,), (1,)), ((), ())),
                                 preferred_element_type=F32) * decay_ref[0]
        inner = jnp.dot(scores.astype(BF16), vb, preferred_element_type=F32)
        cross = jnp.dot(qb, st_ref[n], preferred_element_type=F32) * xi_ref[0]
        out = inner + cross
        out = out * _rms_scale(out) * rn_ref[0]
        o_ref[rows, :] = (out * g_ref[rows, :].astype(F32)).astype(o_ref.dtype)


def retention(proj, tables, ret_norm, batch, side_w, side_layer):
    decay, zeta, xi, gch = tables
    t = proj.shape[0]
    hq = RET_W // RET_DK
    col = lambda off: pl.BlockSpec((SEQ, RET_DK), lambda b, h: (b, off + h))
    per_head = lambda shape: pl.BlockSpec((1,) + shape, lambda b, h: (h, 0, 0))
    side = SideCast(side_w, side_layer, batch * RET_HEADS, lambda b, h: b * RET_HEADS + h)
    in_specs = [col(0), col(hq), col(2 * hq), col(3 * hq),
                per_head((RET_CHUNK, RET_CHUNK)), per_head((RET_CHUNK, RET_DK)),
                per_head((RET_CHUNK, RET_DV)), per_head((1, RET_DV)), per_head((1, RET_DV))]
    return pl.pallas_call(
        _with_side_cast(_retention_kernel, len(in_specs), 1),
        grid=(batch, RET_HEADS),
        in_specs=in_specs + [side.in_spec],
        out_specs=[pl.BlockSpec((SEQ, RET_DV), lambda b, h: (b, h)), side.out_spec],
        out_shape=[jax.ShapeDtypeStruct((t, RET_HEADS * RET_DV), BF16), side.out_shape],
        scratch_shapes=[pltpu.VMEM((SEQ // RET_CHUNK, RET_DK, RET_DV), F32),
                        pltpu.VMEM((SEQ // RET_CHUNK, RET_DK, RET_DV), BF16)],
        compiler_params=_params("arbitrary", "arbitrary"),
        name="retention",
    )(proj, proj, proj, proj, decay, zeta, xi, gch,
      ret_norm.reshape(RET_HEADS, 1, RET_DV), side.w)


def rotary_tables():
    half = RET_DK // 2
    pos = jnp.arange(SEQ)
    freqs = ROPE_BASE ** (-jnp.arange(half, dtype=F32) / half)
    ang = pos.astype(F32)[:, None] * freqs[None, :]
    return jnp.cos(ang), jnp.sin(ang)


def retention_tables():
    log_g = jnp.log1p(-jnp.exp2(-5.0 - jnp.arange(RET_HEADS, dtype=F32)))
    i = jnp.arange(RET_CHUNK)
    diff = i[:, None] - i[None, :]
    decay = jnp.where(diff >= 0, jnp.exp(log_g[:, None, None] * jnp.maximum(diff, 0)), 0.0)
    zeta = jnp.exp(log_g[None, :] * (RET_CHUNK - 1 - i)[:, None])
    xi = jnp.exp(log_g[None, :] * (i + 1)[:, None])
    g_chunk = jnp.exp(log_g * RET_CHUNK)
    zeta_b = jnp.broadcast_to(zeta.T[:, :, None], (RET_HEADS, RET_CHUNK, RET_DK))
    xi_b = jnp.broadcast_to(xi.T[:, :, None], (RET_HEADS, RET_CHUNK, RET_DV))
    gch_b = jnp.broadcast_to(g_chunk[:, None, None], (RET_HEADS, 1, RET_DV))
    return decay, zeta_b, xi_b, gch_b


def _shift_rows(v, d, fill, row):
    if d % SUBLANES == 0:
        return jnp.concatenate([jnp.full((d, v.shape[1]), fill, v.dtype), v[:-d]], axis=0)
    return jnp.where(row >= d, pltpu.roll(v, d, axis=0), fill)


def _rglru_kernel(xc_ref, gg_ref, wai_ref, bai_ref, lam_ref, o_ref, a_ref, u_ref):
    s = xc_ref.shape[0]
    xc = xc_ref[...].astype(F32)
    y = jnp.dot(xc_ref[...], wai_ref[0], preferred_element_type=F32) + bai_ref[...]
    r = jax.nn.sigmoid(y[:, :LRU_BW])
    gi = jax.nn.sigmoid(y[:, LRU_BW:])
    neg_lam = -lam_ref[...]
    softplus = jnp.maximum(neg_lam, 0.0) + jnp.log1p(jnp.exp(-jnp.abs(neg_lam)))
    log_a = (-LRU_C) * r * softplus
    a = jnp.exp(log_a)
    a_ref[...] = a
    u_ref[...] = jnp.sqrt(1.0 - a * a) * (gi * xc)

    c = LRU_SCAN_CHUNK
    crow = lax.broadcasted_iota(jnp.int32, (c, LRU_BW), 0)

    def chunk(n, h_prev):
        rows = pl.ds(pl.multiple_of(n * c, c), c)
        av, bv = a_ref[rows, :], u_ref[rows, :]
        d = 1
        while d < c:
            a_sh = _shift_rows(av, d, 1.0, crow)
            b_sh = _shift_rows(bv, d, 0.0, crow)
            bv = av * b_sh + bv
            av = av * a_sh
            d *= 2
        h = bv + av * h_prev
        o_ref[rows, :] = (h * gg_ref[rows, :].astype(F32)).astype(o_ref.dtype)
        return h[c - 1:c, :]

    lax.fori_loop(0, s // c, chunk, jnp.zeros((1, LRU_BW), F32))


def rglru(proj, w_ai, b_ai, lam, batch):
    t = proj.shape[0]
    x_off = 4 * RET_W // LRU_BW
    g_off = x_off + LRU_W // LRU_BW
    vec = lambda rows: pl.BlockSpec((rows, LRU_BW), lambda b, g: (0, g))
    return pl.pallas_call(
        _rglru_kernel,
        grid=(batch, LRU_BLOCKS),
        in_specs=[
            pl.BlockSpec((SEQ, LRU_BW), lambda b, g: (b, x_off + g)),
            pl.BlockSpec((SEQ, LRU_BW), lambda b, g: (b, g_off + g)),
            pl.BlockSpec((1, LRU_BW, 2 * LRU_BW), lambda b, g: (g, 0, 0)),
            pl.BlockSpec((1, 2 * LRU_BW), lambda b, g: (0, g)),
            vec(1),
        ],
        out_specs=pl.BlockSpec((SEQ, LRU_BW), lambda b, g: (b, g)),
        out_shape=jax.ShapeDtypeStruct((t, LRU_W), BF16),
        scratch_shapes=[pltpu.VMEM((SEQ, LRU_BW), F32), pltpu.VMEM((SEQ, LRU_BW), F32)],
        compiler_params=_params("arbitrary", "arbitrary"),
        name="rglru",
    )(proj, proj, w_ai, b_ai, lam.reshape(1, LRU_W))


def _proj_res_kernel(*refs, splits, sub):
    part_refs = refs[:len(splits)]
    w_ref, x_ref, g_ref, o_ref, hn_ref, acc_ref = refs[len(splits):]

    def mm(r):
        rows = slice(r * sub, (r + 1) * sub)
        acc, off = None, 0
        for a_ref, k in zip(part_refs, splits):
            part = jnp.dot(a_ref[rows, :], w_ref[off:off + k, :], preferred_element_type=F32)
            acc = part if acc is None else acc + part
            off += k
        acc_ref[r % 2] = acc

    def epilogue(r):
        rows = slice(r * sub, (r + 1) * sub)
        y = x_ref[rows, :] + acc_ref[r % 2]
        o_ref[rows, :] = y
        hn_ref[rows, :] = (y * _rms_scale(y) * g_ref[...]).astype(BF16)

    n_sub = x_ref.shape[0] // sub
    mm(0)
    for r in range(n_sub):
        if r + 1 < n_sub:
            mm(r + 1)
        epilogue(r)


def proj_residual(parts, w, x, g, *, tm, sub=256):
    t, n = x.shape
    splits = tuple(p.shape[1] for p in parts)
    row = lambda i: (i, 0)
    return pl.pallas_call(
        functools.partial(_proj_res_kernel, splits=splits, sub=sub),
        grid=(t // tm,),
        in_specs=[pl.BlockSpec((tm, k), row) for k in splits] + [
            pl.BlockSpec(w.shape, lambda i: (0, 0)),
            pl.BlockSpec((tm, n), row),
            pl.BlockSpec((1, n), lambda i: (0, 0)),
        ],
        out_specs=[pl.BlockSpec((tm, n), row), pl.BlockSpec((tm, n), row)],
        out_shape=[jax.ShapeDtypeStruct((t, n), F32), jax.ShapeDtypeStruct((t, n), BF16)],
        scratch_shapes=[pltpu.VMEM((2, sub, n), F32)],
        compiler_params=_params("arbitrary"),
        name="proj_residual",
    )(*parts, w, x, g.reshape(1, n))


def _ffn_up_kernel(hn_ref, wg_ref, wv_ref, cwg_ref, cwv_ref, cbg_ref, cbv_ref, o_ref,
                   wb_ref, u_ref, carry_ref, *, tiles_per_seq, sub):
    i = pl.program_id(1)
    tm = hn_ref.shape[0]
    tn = wg_ref.shape[1]

    @pl.when(i == 0)
    def _():
        wb_ref[:, :tn] = wg_ref[...].astype(BF16)
        wb_ref[:, tn:] = wv_ref[...].astype(BF16)

    def up(r):
        u_ref[r % 2, SUBLANES:, :] = jnp.dot(hn_ref[r * sub:(r + 1) * sub, :], wb_ref[...],
                                             preferred_element_type=F32)

    def conv(slot, cols, cw_ref, cb_ref):
        out = u_ref[slot, SUBLANES:, cols] * cw_ref[FFN_CONV - 1:FFN_CONV, :] + cb_ref[...]
        for d in range(1, FFN_CONV):
            prev = u_ref[slot, SUBLANES - d:SUBLANES - d + sub, cols]
            out = out + prev * cw_ref[FFN_CONV - 1 - d:FFN_CONV - d, :]
        return out

    seq_start = (i % tiles_per_seq) == 0
    u_ref[0, :SUBLANES, :] = jnp.where(seq_start, 0.0, carry_ref[...])
    up(0)
    for r in range(tm // sub):
        slot = r % 2
        u_ref[1 - slot, :SUBLANES, :] = u_ref[slot, sub:, :]
        if (r + 1) * sub < tm:
            up(r + 1)
        gate = conv(slot, slice(0, tn), cwg_ref, cbg_ref)
        val = conv(slot, slice(tn, 2 * tn), cwv_ref, cbv_ref)
        o_ref[r * sub:(r + 1) * sub, :] = (_gelu_tanh(gate) * val).astype(o_ref.dtype)
    carry_ref[...] = u_ref[(tm // sub) % 2, :SUBLANES, :]


def ffn_up(hn, w_up, conv_w, conv_b, layer, side_w, side_layer, *, tm, tn, sub):
    t, d = hn.shape
    nj, ni = D_FF // tn, t // tm
    cb = conv_b.reshape(conv_b.shape[0], 1, 2 * D_FF)
    per_col = lambda rows, off: pl.BlockSpec((None, rows, tn), lambda j, i: (layer, 0, off + j))
    side = SideCast(side_w, side_layer, nj * ni, lambda j, i: j * ni + i)
    in_specs = [pl.BlockSpec((tm, d), lambda j, i: (i, 0)),
                per_col(d, 0), per_col(d, nj),
                per_col(FFN_CONV, 0), per_col(FFN_CONV, nj),
                per_col(1, 0), per_col(1, nj)]
    kern = functools.partial(_ffn_up_kernel, tiles_per_seq=SEQ // tm, sub=sub)
    return pl.pallas_call(
        _with_side_cast(kern, len(in_specs), 1),
        grid=(nj, ni),
        in_specs=in_specs + [side.in_spec],
        out_specs=[pl.BlockSpec((tm, tn), lambda j, i: (i, j)), side.out_spec],
        out_shape=[jax.ShapeDtypeStruct((t, D_FF), BF16), side.out_shape],
        scratch_shapes=[pltpu.VMEM((d, 2 * tn), BF16),
                        pltpu.VMEM((2, SUBLANES + sub, 2 * tn), F32),
                        pltpu.VMEM((SUBLANES, 2 * tn), F32)],
        compiler_params=_params("arbitrary", "arbitrary"),
        name="ffn_up",
    )(hn, w_up, w_up, conv_w, conv_w, cb, cb, side.w)


def _mm_res_kernel(a_ref, w_ref, x_ref, o_ref):
    o_ref[...] = x_ref[...] + jnp.dot(a_ref[...], w_ref[...], preferred_element_type=F32)


def matmul_residual(a, w, x, *, tm, tn, side_w=None, side_layer=None):
    t, k = a.shape
    n = w.shape[1]
    nj, ni = n // tn, t // tm
    in_specs = [pl.BlockSpec((tm, k), lambda j, i: (i, 0)),
                pl.BlockSpec((k, tn), lambda j, i: (0, j)),
                pl.BlockSpec((tm, tn), lambda j, i: (i, j))]
    out_specs = [pl.BlockSpec((tm, tn), lambda j, i: (i, j))]
    out_shape = [jax.ShapeDtypeStruct((t, n), F32)]
    operands = [a, w, x]
    kern = _mm_res_kernel
    if side_w is not None:
        side = SideCast(side_w, side_layer, nj * ni, lambda j, i: j * ni + i)
        kern = _with_side_cast(kern, len(in_specs), 1)
        in_specs.append(side.in_spec)
        out_specs.append(side.out_spec)
        out_shape.append(side.out_shape)
        operands.append(side.w)
    outs = pl.pallas_call(
        kern,
        grid=(nj, ni),
        in_specs=in_specs,
        out_specs=out_specs,
        out_shape=out_shape,
        compiler_params=_params("arbitrary", "arbitrary"),
        name="matmul_residual",
    )(*operands)
    return outs if side_w is not None else outs[0]


def _moba_kernel(relb_ref, bkt_ref, q_ref, k_ref, v_ref, o_ref,
                 bias_ref, qt_ref, ks_ref, vt_ref, s_ref, p_ref):
    hg, b = pl.program_id(0), pl.program_id(1)
    nb = SEQ // MOBA_BLOCK
    blk = MOBA_BLOCK
    dh = MOBA_DH
    heads = range(qt_ref.shape[0])

    @pl.when((hg == 0) & (b == 0))
    def _():
        key_blk = lax.broadcasted_iota(jnp.int32, (SEQ, dh), 0) // blk
        lane = lax.broadcasted_iota(jnp.int32, (SEQ, dh), 1)
        onehot = (lane < 2 * nb) & ((lane % nb) == key_blk)
        for hh in heads:
            ks_ref[hh, :, dh:] = jnp.where(onehot, 1.0, 0.0).astype(BF16)
            qt_ref[hh, dh:, :] = jnp.zeros((dh, SEQ), BF16)
            vt_ref[hh, dh:, :] = jnp.ones((vt_ref.shape[1] - dh, SEQ), BF16)

    @pl.when(b == 0)
    def _():
        for hh in heads:
            for m in range(2):
                bk = bkt_ref[m]
                acc = jnp.full(bk.shape, NEG_INF, F32)
                for t in range(REL_BUCKETS):
                    acc = jnp.where(bk == t, relb_ref[t, hg * len(heads) + hh] * LOG2E, acc)
                bias_ref[hh, m] = acc

    blk_id = lax.broadcasted_iota(jnp.int32, (nb, SEQ), 0)
    q_blk = lax.broadcasted_iota(jnp.int32, (nb, SEQ), 1) // blk
    averager = jnp.where(blk_id == q_blk, 1.0 / blk, 0.0).astype(BF16)
    valid = blk_id < q_blk

    def prepare(hh):
        lanes = slice(hh * dh, (hh + 1) * dh)
        far_bias = relb_ref[REL_BUCKETS - 1, hg * len(heads) + hh] * LOG2E
        qt_ref[hh, :dh, :] = q_ref[:, lanes].astype(F32).T.astype(BF16)
        ks_ref[hh, :, :dh] = k_ref[:, lanes]
        vt_ref[hh, :dh, :] = v_ref[:, lanes].astype(F32).T.astype(BF16)

        kmean = jnp.dot(averager, k_ref[:, lanes], preferred_element_type=F32)
        gate = jnp.dot(kmean.astype(BF16), qt_ref[hh, :dh, :], preferred_element_type=F32)
        gm = jnp.where(valid, gate, NEG_INF)
        cnt = jnp.zeros(gate.shape, F32)
        for s in range(1, nb):
            gs = pltpu.roll(gm, s, axis=0)
            tie = jnp.where(((blk_id - s) % nb) < blk_id, 1.0, 0.0)
            cnt = cnt + jnp.where(gs > gm, 1.0, jnp.where(gs == gm, tie, 0.0))
        picked = jnp.where(blk_id < q_blk - 1, far_bias, 0.0)
        row = jnp.where(valid, jnp.where(cnt < MOBA_TOPK, picked, MASK_VALUE), 0.0)
        row_hi = row.astype(BF16).astype(F32)
        row_lo = jnp.where(row > 0.5 * MASK_VALUE, row - row_hi, 0.0)
        qt_ref[hh, dh:dh + 2 * nb, :] = jnp.concatenate([row_hi, row_lo], axis=0).astype(BF16)

    def scores(hh, pos, i):
        for jb in range(i + 1):
            rows = slice(jb * blk, (jb + 1) * blk)
            s_ref[hh, pos % 2, rows, :] = jnp.dot(
                ks_ref[hh, rows, :], qt_ref[hh, :, i * blk:(i + 1) * blk],
                preferred_element_type=F32)

    def softmax(hh, pos, i):
        slot = pos % 2
        maxes = []
        for jb in range(i + 1):
            rows = slice(jb * blk, (jb + 1) * blk)
            s_t = s_ref[hh, slot, rows, :]
            if jb >= i - 1:
                s_t = s_t + bias_ref[hh, i - jb]
                s_ref[hh, slot, rows, :] = s_t
            maxes.append(jnp.max(s_t, axis=0, keepdims=True))
        m = functools.reduce(jnp.maximum, maxes)
        for jb in range(i + 1):
            rows = slice(jb * blk, (jb + 1) * blk)
            p_ref[hh, slot, rows, :] = jnp.exp2(s_ref[hh, slot, rows, :] - m).astype(BF16)

    def attend(hh, pos, i):
        acc = jnp.dot(vt_ref[hh, :, :(i + 1) * blk], p_ref[hh, pos % 2, :(i + 1) * blk, :],
                      preferred_element_type=F32)
        out = acc[:dh, :] / acc[dh:dh + 1, :]
        o_ref[i * blk:(i + 1) * blk, hh * dh:(hh + 1) * dh] = out.T.astype(o_ref.dtype)

    for hh in heads:
        prepare(hh)

    order = list(range(nb - 1, -1, -1))
    for hh in heads:
        scores(hh, 0, order[0])
    for pos, i in enumerate(order):
        for hh in heads:
            if pos + 1 < nb:
                scores(hh, pos + 1, order[pos + 1])
            softmax(hh, pos, i)
            if pos > 0:
                attend(hh, pos - 1, order[pos - 1])
    for hh in heads:
        attend(hh, nb - 1, order[-1])


def moba_bucket_tables():
    r = jnp.arange(MOBA_BLOCK)
    dist_own = r[None, :] - r[:, None]

    def bucket(dist):
        n = jnp.maximum(dist, 0)
        max_exact = REL_BUCKETS // 2
        nf = jnp.maximum(n, 1).astype(F32)
        large = max_exact + (jnp.log(nf / max_exact) / math.log(REL_MAX_DIST / max_exact)
                             * (REL_BUCKETS - max_exact)).astype(jnp.int32)
        large = jnp.minimum(large, REL_BUCKETS - 1)
        return jnp.where(n < max_exact, n, large)

    own = jnp.where(dist_own >= 0, bucket(dist_own), -1)
    adj = bucket(dist_own + MOBA_BLOCK)
    return jnp.stack([own, adj]).astype(jnp.int32)


def moba(proj, rel_bias, batch, side_w, side_layer):
    t = proj.shape[0]
    hps = MOBA_HEADS_PER_STEP
    groups = MOBA_HEADS // hps
    width = hps * MOBA_DH
    col = lambda off: pl.BlockSpec((SEQ, width), lambda hg, b: (b, off + hg))
    side = SideCast(side_w, side_layer, groups * batch, lambda hg, b: hg * batch + b)
    in_specs = [pl.BlockSpec(memory_space=pltpu.SMEM),
                pl.BlockSpec((2, MOBA_BLOCK, MOBA_BLOCK), lambda hg, b: (0, 0, 0)),
                col(0), col(groups), col(2 * groups)]
    return pl.pallas_call(
        _with_side_cast(_moba_kernel, len(in_specs), 1),
        grid=(groups, batch),
        in_specs=in_specs + [side.in_spec],
        out_specs=[pl.BlockSpec((SEQ, width), lambda hg, b: (b, hg)), side.out_spec],
        out_shape=[jax.ShapeDtypeStruct((t, MOBA_HEADS * MOBA_DH), BF16), side.out_shape],
        scratch_shapes=[pltpu.VMEM((hps, 2, MOBA_BLOCK, MOBA_BLOCK), F32),
                        pltpu.VMEM((hps, 2 * MOBA_DH, SEQ), BF16),
                        pltpu.VMEM((hps, SEQ, 2 * MOBA_DH), BF16),
                        pltpu.VMEM((hps, MOBA_DH + 2 * SUBLANES, SEQ), BF16),
                        pltpu.VMEM((hps, 2, SEQ, MOBA_BLOCK), F32),
                        pltpu.VMEM((hps, 2, SEQ, MOBA_BLOCK), BF16)],
        compiler_params=_params("arbitrary", "arbitrary"),
        name="moba",
    )(rel_bias, moba_bucket_tables(), proj, proj, proj, side.w)


def moba_head_gain(q_norm, k_norm):
    qg = jnp.tile(q_norm * (MOBA_DH ** -0.5 * LOG2E), MOBA_HEADS)
    return jnp.concatenate([qg, jnp.tile(k_norm, MOBA_HEADS)])


def kernel(x, norm_mix, norm_ffn, ev_w_in, ret_norm, lru_conv_w, lru_conv_b, lru_w_a, lru_b_a,
           lru_w_i, lru_b_i, lru_lambda, ev_w_out, od_w_in, q_norm, k_norm, od_w_out, rel_bias,
           ffn_w_up, ffn_conv_w, ffn_conv_b, ffn_w_down):
    batch, seq, d = x.shape
    assert (seq, d) == (SEQ, D_MODEL)
    xf = x.reshape(batch * seq, d)


    tm_in = 1024
    cos, sin = rotary_tables()
    pos_rows = lambda i, j: (i % (SEQ // tm_in), 0)
    fixed = lambda i, j: (0, 0)
    proj = norm_matmul(
        xf, norm_mix[0], ev_w_in, 0, tm=tm_in, tn=RET_W, out_dtype=BF16,
        col_epilogues=[("rotary", 1.0), ("rotary", RET_DK ** -0.5), None, ("silu",), ("conv",),
                       ("gelu",)],
        extras={"cos": (cos, (tm_in, RET_DK // 2), pos_rows),
                "sin": (sin, (tm_in, RET_DK // 2), pos_rows),
                "cw": (lru_conv_w[0], (LRU_CONV, LRU_W), fixed),
                "cb": (lru_conv_b[0].reshape(1, LRU_W), (1, LRU_W), fixed)})
    ret, w_out_b = retention(proj, retention_tables(), ret_norm[0], batch, ev_w_out, 0)
    w_ai = jnp.concatenate([lru_w_a[0], lru_w_i[0]], axis=-1).astype(BF16)
    b_ai = jnp.concatenate([lru_b_a[0].reshape(LRU_BLOCKS, LRU_BW),
                            lru_b_i[0].reshape(LRU_BLOCKS, LRU_BW)], axis=-1).reshape(1, 2 * LRU_W)
    lru = rglru(proj, w_ai, b_ai, lru_lambda[0], batch)
    xf, hn = proj_residual([ret, lru], w_out_b, xf, norm_ffn[0], tm=512)
    act, w_down_b = ffn_up(hn, ffn_w_up, ffn_conv_w, ffn_conv_b, 0, ffn_w_down, 0,
                           tm=2048, tn=512, sub=256)
    xf, w_in_b = matmul_residual(act, w_down_b, xf, tm=512, tn=1024, side_w=od_w_in, side_layer=0)

    head_norm = ("head_norm", MOBA_DH)
    proj = norm_matmul(
        xf, norm_mix[1], w_in_b, None, tm=1024, tn=MOBA_W, out_dtype=BF16,
        col_epilogues=[head_norm, head_norm, None],
        extras={"hg": (moba_head_gain(q_norm[0], k_norm[0]).reshape(1, 2 * MOBA_W),
                       (1, MOBA_W), lambda i, j: (0, jnp.minimum(j, 1)))})
    att, w_out_b = moba(proj, rel_bias, batch, od_w_out, 0)
    xf, hn = proj_residual([att], w_out_b, xf, norm_ffn[1], tm=512)
    act, w_down_b = ffn_up(hn, ffn_w_up, ffn_conv_w, ffn_conv_b, 1, ffn_w_down, 1,
                           tm=2048, tn=512, sub=256)
    xf = matmul_residual(act, w_down_b, xf, tm=512, tn=1024)
    return xf.reshape(batch, seq, d)
```

```python
import functools
import math

import jax
import jax.numpy as jnp
import numpy as np
from jax import lax
from jax.experimental import pallas as pl
from jax.experimental.pallas import tpu as pltpu

D_MODEL = 2048
SEQ = 2048
RET_HEADS = 4
RET_DK = 256
RET_DV = 256
RET_W = RET_HEADS * RET_DK
RET_CHUNK = 128
ROPE_BASE = 10000.0
LRU_W = 1024
LRU_BLOCKS = 8
LRU_BW = 128
LRU_CONV = 4
LRU_C = 8.0
LRU_SCAN_CHUNK = 64
LRU_BLOCKS_PER_STEP = 2
MOBA_HEADS = 16
MOBA_DH = 128
MOBA_W = MOBA_HEADS * MOBA_DH
MOBA_HEADS_PER_STEP = 2
MOBA_BLOCK = 256
MOBA_TOPK = 3
REL_BUCKETS = 32
REL_MAX_DIST = 128
D_FF = 5632
FFN_CONV = 3
EPS = 1e-6

VMEM_LIMIT_BYTES = 56 * 1024 * 1024
SUBLANES = 8

BF16 = jnp.bfloat16
F32 = jnp.float32
NEG_INF = float("-inf")
MASK_VALUE = -1e30
LOG2E = math.log2(math.e)


def _params(*sem):
    return pltpu.CompilerParams(dimension_semantics=sem, vmem_limit_bytes=VMEM_LIMIT_BYTES)


def _gelu_tanh(x):
    c = math.sqrt(2.0 / math.pi)
    return 0.5 * x * (1.0 + jnp.tanh(c * (x + 0.044715 * (x * x * x))))


def _rms_scale(x):
    return lax.rsqrt(jnp.mean(x * x, axis=-1, keepdims=True) + EPS)


class SideCast:
    def __init__(self, w, layer, n_steps, step_of):
        k, n = w.shape[1:]
        rows = k // n_steps
        assert rows * n_steps == k and rows % (2 * SUBLANES) == 0
        self.w = w
        self.in_spec = pl.BlockSpec((None, rows, n), lambda *ids: (layer, step_of(*ids), 0))
        self.out_spec = pl.BlockSpec((rows, n), lambda *ids: (step_of(*ids), 0))
        self.out_shape = jax.ShapeDtypeStruct((k, n), BF16)


def _with_side_cast(body, n_in, n_out):
    def wrapped(*refs):
        side_in, side_out = refs[n_in], refs[n_in + 1 + n_out]
        side_out[...] = side_in[...].astype(BF16)
        body(*refs[:n_in], *refs[n_in + 1:n_in + 1 + n_out], *refs[n_in + 2 + n_out:])
    return wrapped


def _norm_mm_kernel(*refs, extra_names, col_epilogues, sub, tiles_per_seq):
    n_extra = len(extra_names)
    x_ref, g_ref, w_ref = refs[:3]
    ex = dict(zip(extra_names, refs[3:3 + n_extra]))
    o_ref, hn_ref = refs[3 + n_extra:5 + n_extra]
    scratch = list(refs[5 + n_extra:])
    w_is_bf16 = w_ref.dtype == BF16
    wb_ref = w_ref if w_is_bf16 else scratch.pop(0)
    acc_ref = scratch.pop(0) if any(col_epilogues) else None
    halo_ref = scratch.pop(0) if ("conv",) in col_epilogues else None
    i, j = pl.program_id(0), pl.program_id(1)
    tm, tn = o_ref.shape
    n_sub = tm // sub

    def apply_epilogue(kind, a, r):
        rows = slice(r * sub, (r + 1) * sub)
        if kind[0] == "head_norm":
            heads = [a[:, s:s + kind[1]] for s in range(0, tn, kind[1])]
            return jnp.concatenate([h * _rms_scale(h) for h in heads], axis=-1) * ex["hg"][...]
        if kind[0] == "rotary":
            cos, sin = ex["cos"][rows, :], ex["sin"][rows, :]
            if kind[1] != 1.0:
                cos, sin = cos * kind[1], sin * kind[1]
            half = RET_DK // 2
            out = []
            for s in range(0, tn, RET_DK):
                x1, x2 = a[:, s:s + half], a[:, s + half:s + RET_DK]
                out += [x1 * cos - x2 * sin, x1 * sin + x2 * cos]
            return jnp.concatenate(out, axis=-1)
        if kind[0] == "silu":
            return a * jax.nn.sigmoid(a)
        if kind[0] == "gelu":
            return _gelu_tanh(a)
        if kind[0] == "conv":
            taps = ex["cw"].shape[0]
            fresh = (i % tiles_per_seq == 0) if r == 0 else False
            halo = jnp.where(fresh, 0.0, halo_ref[...])
            halo_ref[...] = a[sub - SUBLANES:, :]
            ext = jnp.concatenate([halo, a], axis=0)
            out = a * ex["cw"][taps - 1:taps, :] + ex["cb"][...]
            for d in range(1, taps):
                prev = pltpu.roll(ext, d, axis=0)[SUBLANES:, :]
                out = out + prev * ex["cw"][taps - 1 - d:taps - d, :]
            return out
        raise ValueError(kind)

    def plain():
        o_ref[...] = jnp.dot(hn_ref[...], w_ref[...].astype(BF16),
                             preferred_element_type=F32).astype(o_ref.dtype)

    def staged(norm_rows, kind):
        if not w_is_bf16:
            wb_ref[...] = w_ref[...].astype(BF16)

        def norm(r):
            x = x_ref[r * sub:(r + 1) * sub, :]
            hn_ref[r * sub:(r + 1) * sub, :] = (x * _rms_scale(x) * g_ref[...]).astype(BF16)

        def mm(r):
            acc = jnp.dot(hn_ref[r * sub:(r + 1) * sub, :], wb_ref[...],
                          preferred_element_type=F32)
            if kind:
                acc_ref[r % 2] = acc
            else:
                o_ref[r * sub:(r + 1) * sub, :] = acc.astype(o_ref.dtype)

        def epilogue(r):
            o_ref[r * sub:(r + 1) * sub, :] = apply_epilogue(
                kind, acc_ref[r % 2], r).astype(o_ref.dtype)

        if norm_rows:
            norm(0)
        for r in range(n_sub):
            if norm_rows and r + 1 < n_sub:
                norm(r + 1)
            mm(r)
            if kind and r > 0:
                epilogue(r - 1)
        if kind:
            epilogue(n_sub - 1)

    for col, kind in enumerate(col_epilogues):
        if col == 0 or kind:
            pl.when(j == col)(functools.partial(staged, col == 0, kind))
        else:
            pl.when(j == col)(plain)


def norm_matmul(x, g, w, layer, *, tm, tn, out_dtype, col_epilogues, extras=None, sub=256):
    t, d = x.shape
    n = w.shape[-1]
    assert len(col_epilogues) == n // tn
    if layer is None:
        assert w.dtype == BF16
        w_spec = pl.BlockSpec((d, tn), lambda i, j: (0, j))
    else:
        w_spec = pl.BlockSpec((None, d, tn), lambda i, j: (layer, 0, j))
    extras = extras or {}
    scratch = [pltpu.VMEM((tm, d), BF16)]
    if layer is not None:
        scratch.append(pltpu.VMEM((d, tn), BF16))
    if any(col_epilogues):
        scratch.append(pltpu.VMEM((2, sub, tn), F32))
    if ("conv",) in col_epilogues:
        scratch.append(pltpu.VMEM((SUBLANES, tn), F32))
    return pl.pallas_call(
        functools.partial(_norm_mm_kernel, extra_names=tuple(extras),
                          col_epilogues=tuple(col_epilogues), sub=sub, tiles_per_seq=SEQ // tm),
        grid=(t // tm, n // tn),
        in_specs=[
            pl.BlockSpec((tm, d), lambda i, j: (i, 0)),
            pl.BlockSpec((1, d), lambda i, j: (0, 0)),
            w_spec,
        ] + [pl.BlockSpec(block, index_map) for _, block, index_map in extras.values()],
        out_specs=pl.BlockSpec((tm, tn), lambda i, j: (i, j)),
        out_shape=jax.ShapeDtypeStruct((t, n), out_dtype),
        scratch_shapes=scratch,
        compiler_params=_params("arbitrary", "arbitrary"),
        name="norm_matmul",
    )(x, g.reshape(1, d), w, *[arr for arr, _, _ in extras.values()])


def _retention_kernel(q_ref, k_ref, v_ref, g_ref, decay_ref, zeta_ref, xi_ref,
                      gch_ref, rn_ref, o_ref, kv_ref, st_ref):
    n_chunks = q_ref.shape[0] // RET_CHUNK
    chunk_rows = lambda n: slice(n * RET_CHUNK, (n + 1) * RET_CHUNK)

    for n in range(n_chunks):
        rows = chunk_rows(n)
        kz = (k_ref[rows, :].astype(F32) * zeta_ref[0]).astype(BF16)
        kv_ref[n] = lax.dot_general(kz, v_ref[rows, :], (((0,), (0,)), ((), ())),
                                    preferred_element_type=F32)

    state = jnp.zeros((RET_DK, RET_DV), F32)
    for n in range(n_chunks):
        st_ref[n] = state.astype(BF16)
        state = gch_ref[0] * state + kv_ref[n]

    for n in range(n_chunks):
        rows = chunk_rows(n)
        qb, vb = q_ref[rows, :], v_ref[rows, :]
        scores = lax.dot_general(qb, k_ref[rows, :], (((1,), (1,)), ((), ())),
                                 preferred_element_type=F32) * decay_ref[0]
        inner = jnp.dot(scores.astype(BF16), vb, preferred_element_type=F32)
        cross = jnp.dot(qb, st_ref[n], preferred_element_type=F32) * xi_ref[0]
        out = inner + cross
        out = out * _rms_scale(out) * rn_ref[0]
        o_ref[rows, :] = (out * g_ref[rows, :].astype(F32)).astype(o_ref.dtype)


def retention(proj, tables, ret_norm, batch, side_w, side_layer):
    decay, zeta, xi, gch = tables
    t = proj.shape[0]
    hq = RET_W // RET_DK
    col = lambda off: pl.BlockSpec((SEQ, RET_DK), lambda b, h: (b, off + h))
    per_head = lambda shape: pl.BlockSpec((1,) + shape, lambda b, h: (h, 0, 0))
    side = SideCast(side_w, side_layer, batch * RET_HEADS, lambda b, h: b * RET_HEADS + h)
    in_specs = [col(0), col(hq), col(2 * hq), col(3 * hq),
                per_head((RET_CHUNK, RET_CHUNK)), per_head((RET_CHUNK, RET_DK)),
                per_head((RET_CHUNK, RET_DV)), per_head((1, RET_DV)), per_head((1, RET_DV))]
    return pl.pallas_call(
        _with_side_cast(_retention_kernel, len(in_specs), 1),
        grid=(batch, RET_HEADS),
        in_specs=in_specs + [side.in_spec],
        out_specs=[pl.BlockSpec((SEQ, RET_DV), lambda b, h: (b, h)), side.out_spec],
        out_shape=[jax.ShapeDtypeStruct((t, RET_HEADS * RET_DV), BF16), side.out_shape],
        scratch_shapes=[pltpu.VMEM((SEQ // RET_CHUNK, RET_DK, RET_DV), F32),
                        pltpu.VMEM((SEQ // RET_CHUNK, RET_DK, RET_DV), BF16)],
        compiler_params=_params("arbitrary", "arbitrary"),
        name="retention",
    )(proj, proj, proj, proj, decay, zeta, xi, gch,
      ret_norm.reshape(RET_HEADS, 1, RET_DV), side.w)


def rotary_tables():
    f32 = np.float32
    half = RET_DK // 2
    freqs = f32(ROPE_BASE) ** (-np.arange(half, dtype=f32) / f32(half))
    ang = np.arange(SEQ).astype(f32)[:, None] * freqs[None, :]
    return jnp.asarray(np.cos(ang)), jnp.asarray(np.sin(ang))


def retention_tables():
    f32 = np.float32
    log_g = np.log1p(-np.exp2(f32(-5.0) - np.arange(RET_HEADS, dtype=f32)))
    i = np.arange(RET_CHUNK)
    diff = i[:, None] - i[None, :]
    decay = np.where(diff >= 0, np.exp(log_g[:, None, None] * np.maximum(diff, 0).astype(f32)),
                     f32(0.0))
    zeta = np.exp(log_g[None, :] * (RET_CHUNK - 1 - i).astype(f32)[:, None])
    xi = np.exp(log_g[None, :] * (i + 1).astype(f32)[:, None])
    g_chunk = np.exp(log_g * f32(RET_CHUNK))
    zeta_b = np.broadcast_to(zeta.T[:, :, None], (RET_HEADS, RET_CHUNK, RET_DK))
    xi_b = np.broadcast_to(xi.T[:, :, None], (RET_HEADS, RET_CHUNK, RET_DV))
    gch_b = np.broadcast_to(g_chunk[:, None, None], (RET_HEADS, 1, RET_DV))
    return tuple(jnp.asarray(a.astype(f32)) for a in (decay, zeta_b, xi_b, gch_b))


def _shift_rows(v, d, fill, row):
    if d % SUBLANES == 0:
        return jnp.concatenate([jnp.full((d, v.shape[1]), fill, v.dtype), v[:-d]], axis=0)
    return jnp.where(row >= d, pltpu.roll(v, d, axis=0), fill)


def _rglru_kernel(xc_ref, gg_ref, wai_ref, bai_ref, lam_ref, o_ref, a_ref, u_ref):
    s, width = xc_ref.shape
    for blk in range(width // LRU_BW):
        lanes = slice(blk * LRU_BW, (blk + 1) * LRU_BW)
        gates = slice(2 * blk * LRU_BW, 2 * (blk + 1) * LRU_BW)
        xc = xc_ref[:, lanes].astype(F32)
        y = jnp.dot(xc_ref[:, lanes], wai_ref[blk], preferred_element_type=F32) + bai_ref[:, gates]
        r = jax.nn.sigmoid(y[:, :LRU_BW])
        gi = jax.nn.sigmoid(y[:, LRU_BW:])
        neg_lam = -lam_ref[:, lanes]
        softplus = jnp.maximum(neg_lam, 0.0) + jnp.log1p(jnp.exp(-jnp.abs(neg_lam)))
        log_a = (-LRU_C) * r * softplus
        a = jnp.exp(log_a)
        a_ref[:, lanes] = a
        u_ref[:, lanes] = jnp.sqrt(1.0 - a * a) * (gi * xc)

    c = LRU_SCAN_CHUNK
    crow = lax.broadcasted_iota(jnp.int32, (c, width), 0)

    def chunk(n, h_prev):
        rows = pl.ds(pl.multiple_of(n * c, c), c)
        av, bv = a_ref[rows, :], u_ref[rows, :]
        d = 1
        while d < c:
            a_sh = _shift_rows(av, d, 1.0, crow)
            b_sh = _shift_rows(bv, d, 0.0, crow)
            bv = av * b_sh + bv
            av = av * a_sh
            d *= 2
        h = bv + av * h_prev
        o_ref[rows, :] = (h * gg_ref[rows, :].astype(F32)).astype(o_ref.dtype)
        return h[c - 1:c, :]

    lax.fori_loop(0, s // c, chunk, jnp.zeros((1, width), F32))


def rglru(proj, w_ai, b_ai, lam, batch):
    t = proj.shape[0]
    bps = LRU_BLOCKS_PER_STEP
    width = bps * LRU_BW
    x_off = 4 * RET_W // width
    g_off = x_off + LRU_W // width
    return pl.pallas_call(
        _rglru_kernel,
        grid=(batch, LRU_BLOCKS // bps),
        in_specs=[
            pl.BlockSpec((SEQ, width), lambda b, g: (b, x_off + g)),
            pl.BlockSpec((SEQ, width), lambda b, g: (b, g_off + g)),
            pl.BlockSpec((bps, LRU_BW, 2 * LRU_BW), lambda b, g: (g, 0, 0)),
            pl.BlockSpec((1, 2 * width), lambda b, g: (0, g)),
            pl.BlockSpec((1, width), lambda b, g: (0, g)),
        ],
        out_specs=pl.BlockSpec((SEQ, width), lambda b, g: (b, g)),
        out_shape=jax.ShapeDtypeStruct((t, LRU_W), BF16),
        scratch_shapes=[pltpu.VMEM((SEQ, width), F32), pltpu.VMEM((SEQ, width), F32)],
        compiler_params=_params("arbitrary", "arbitrary"),
        name="rglru",
    )(proj, proj, w_ai, b_ai, lam.reshape(1, LRU_W))


def _proj_res_kernel(*refs, splits, sub):
    part_refs = refs[:len(splits)]
    w_ref, x_ref, g_ref, o_ref, hn_ref, acc_ref = refs[len(splits):]

    def mm(r):
        rows = slice(r * sub, (r + 1) * sub)
        acc, off = None, 0
        for a_ref, k in zip(part_refs, splits):
            part = jnp.dot(a_ref[rows, :], w_ref[off:off + k, :], preferred_element_type=F32)
            acc = part if acc is None else acc + part
            off += k
        acc_ref[r % 2] = acc

    def epilogue(r):
        rows = slice(r * sub, (r + 1) * sub)
        y = x_ref[rows, :] + acc_ref[r % 2]
        o_ref[rows, :] = y
        hn_ref[rows, :] = (y * _rms_scale(y) * g_ref[...]).astype(BF16)

    n_sub = x_ref.shape[0] // sub
    mm(0)
    for r in range(n_sub):
        if r + 1 < n_sub:
            mm(r + 1)
        epilogue(r)


def proj_residual(parts, w, x, g, *, tm, sub=256):
    t, n = x.shape
    splits = tuple(p.shape[1] for p in parts)
    row = lambda i: (i, 0)
    return pl.pallas_call(
        functools.partial(_proj_res_kernel, splits=splits, sub=sub),
        grid=(t // tm,),
        in_specs=[pl.BlockSpec((tm, k), row) for k in splits] + [
            pl.BlockSpec(w.shape, lambda i: (0, 0)),
            pl.BlockSpec((tm, n), row),
            pl.BlockSpec((1, n), lambda i: (0, 0)),
        ],
        out_specs=[pl.BlockSpec((tm, n), row), pl.BlockSpec((tm, n), row)],
        out_shape=[jax.ShapeDtypeStruct((t, n), F32), jax.ShapeDtypeStruct((t, n), BF16)],
        scratch_shapes=[pltpu.VMEM((2, sub, n), F32)],
        compiler_params=_params("arbitrary"),
        name="proj_residual",
    )(*parts, w, x, g.reshape(1, n))


def _ffn_up_kernel(hn_ref, wg_ref, wv_ref, cwg_ref, cwv_ref, cbg_ref, cbv_ref, o_ref,
                   wb_ref, u_ref, carry_ref, *, tiles_per_seq, sub):
    i = pl.program_id(1)
    tm = hn_ref.shape[0]
    tn = wg_ref.shape[1]

    @pl.when(i == 0)
    def _():
        wb_ref[:, :tn] = wg_ref[...].astype(BF16)
        wb_ref[:, tn:] = wv_ref[...].astype(BF16)

    def up(r):
        u_ref[r % 2, SUBLANES:, :] = jnp.dot(hn_ref[r * sub:(r + 1) * sub, :], wb_ref[...],
                                             preferred_element_type=F32)

    def conv(slot, cols, cw_ref, cb_ref):
        out = u_ref[slot, SUBLANES:, cols] * cw_ref[FFN_CONV - 1:FFN_CONV, :] + cb_ref[...]
        for d in range(1, FFN_CONV):
            prev = u_ref[slot, SUBLANES - d:SUBLANES - d + sub, cols]
            out = out + prev * cw_ref[FFN_CONV - 1 - d:FFN_CONV - d, :]
        return out

    seq_start = (i % tiles_per_seq) == 0
    u_ref[0, :SUBLANES, :] = jnp.where(seq_start, 0.0, carry_ref[...])
    up(0)
    for r in range(tm // sub):
        slot = r % 2
        u_ref[1 - slot, :SUBLANES, :] = u_ref[slot, sub:, :]
        if (r + 1) * sub < tm:
            up(r + 1)
        gate = conv(slot, slice(0, tn), cwg_ref, cbg_ref)
        val = conv(slot, slice(tn, 2 * tn), cwv_ref, cbv_ref)
        o_ref[r * sub:(r + 1) * sub, :] = (_gelu_tanh(gate) * val).astype(o_ref.dtype)
    carry_ref[...] = u_ref[(tm // sub) % 2, :SUBLANES, :]


def ffn_up(hn, w_up, conv_w, conv_b, layer, side_w, side_layer, *, tm, tn, sub):
    t, d = hn.shape
    nj, ni = D_FF // tn, t // tm
    cb = conv_b.reshape(conv_b.shape[0], 1, 2 * D_FF)
    per_col = lambda rows, off: pl.BlockSpec((None, rows, tn), lambda j, i: (layer, 0, off + j))
    side = SideCast(side_w, side_layer, nj * ni, lambda j, i: j * ni + i)
    in_specs = [pl.BlockSpec((tm, d), lambda j, i: (i, 0)),
                per_col(d, 0), per_col(d, nj),
                per_col(FFN_CONV, 0), per_col(FFN_CONV, nj),
                per_col(1, 0), per_col(1, nj)]
    kern = functools.partial(_ffn_up_kernel, tiles_per_seq=SEQ // tm, sub=sub)
    return pl.pallas_call(
        _with_side_cast(kern, len(in_specs), 1),
        grid=(nj, ni),
        in_specs=in_specs + [side.in_spec],
        out_specs=[pl.BlockSpec((tm, tn), lambda j, i: (i, j)), side.out_spec],
        out_shape=[jax.ShapeDtypeStruct((t, D_FF), BF16), side.out_shape],
        scratch_shapes=[pltpu.VMEM((d, 2 * tn), BF16),
                        pltpu.VMEM((2, SUBLANES + sub, 2 * tn), F32),
                        pltpu.VMEM((SUBLANES, 2 * tn), F32)],
        compiler_params=_params("arbitrary", "arbitrary"),
        name="ffn_up",
    )(hn, w_up, w_up, conv_w, conv_w, cb, cb, side.w)


def _mm_res_kernel(a_ref, w_ref, x_ref, o_ref):
    o_ref[...] = x_ref[...] + jnp.dot(a_ref[...], w_ref[...], preferred_element_type=F32)


def matmul_residual(a, w, x, *, tm, tn, side_w=None, side_layer=None):
    t, k = a.shape
    n = w.shape[1]
    nj, ni = n // tn, t // tm
    in_specs = [pl.BlockSpec((tm, k), lambda j, i: (i, 0)),
                pl.BlockSpec((k, tn), lambda j, i: (0, j)),
                pl.BlockSpec((tm, tn), lambda j, i: (i, j))]
    out_specs = [pl.BlockSpec((tm, tn), lambda j, i: (i, j))]
    out_shape = [jax.ShapeDtypeStruct((t, n), F32)]
    operands = [a, w, x]
    kern = _mm_res_kernel
    if side_w is not None:
        side = SideCast(side_w, side_layer, nj * ni, lambda j, i: j * ni + i)
        kern = _with_side_cast(kern, len(in_specs), 1)
        in_specs.append(side.in_spec)
        out_specs.append(side.out_spec)
        out_shape.append(side.out_shape)
        operands.append(side.w)
    outs = pl.pallas_call(
        kern,
        grid=(nj, ni),
        in_specs=in_specs,
        out_specs=out_specs,
        out_shape=out_shape,
        compiler_params=_params("arbitrary", "arbitrary"),
        name="matmul_residual",
    )(*operands)
    return outs if side_w is not None else outs[0]


def _moba_kernel(relb_ref, bkt_ref, q_ref, k_ref, v_ref, o_ref,
                 bias_ref, qt_ref, ks_ref, vt_ref, s_ref, p_ref):
    hg, b = pl.program_id(0), pl.program_id(1)
    nb = SEQ // MOBA_BLOCK
    blk = MOBA_BLOCK
    dh = MOBA_DH
    heads = range(qt_ref.shape[0])

    @pl.when((hg == 0) & (b == 0))
    def _():
        key_blk = lax.broadcasted_iota(jnp.int32, (SEQ, dh), 0) // blk
        lane = lax.broadcasted_iota(jnp.int32, (SEQ, dh), 1)
        onehot = (lane < 2 * nb) & ((lane % nb) == key_blk)
        for hh in heads:
            ks_ref[hh, :, dh:] = jnp.where(onehot, 1.0, 0.0).astype(BF16)
            qt_ref[hh, dh:, :] = jnp.zeros((dh, SEQ), BF16)
            vt_ref[hh, dh:, :] = jnp.ones((vt_ref.shape[1] - dh, SEQ), BF16)

    @pl.when(b == 0)
    def _():
        for hh in heads:
            for m in range(2):
                bk = bkt_ref[m]
                strip = jnp.full(bk.shape, NEG_INF, F32)
                for t in range(REL_BUCKETS):
                    strip = jnp.where(bk == t, relb_ref[t, hg * len(heads) + hh] * LOG2E, strip)
                tiled = jnp.concatenate([strip] * (blk // SUBLANES), axis=0)
                toeplitz = pltpu.roll(tiled, 0, axis=1, stride=1, stride_axis=0)
                bias_ref[hh, m] = toeplitz[:, :blk]

    blk_id = lax.broadcasted_iota(jnp.int32, (nb, SEQ), 0)
    q_blk = lax.broadcasted_iota(jnp.int32, (nb, SEQ), 1) // blk
    averager = jnp.where(blk_id == q_blk, 1.0 / blk, 0.0).astype(BF16)
    valid = blk_id < q_blk

    def prepare(hh):
        lanes = slice(hh * dh, (hh + 1) * dh)
        far_bias = relb_ref[REL_BUCKETS - 1, hg * len(heads) + hh] * LOG2E
        qt_ref[hh, :dh, :] = q_ref[:, lanes].astype(F32).T.astype(BF16)
        ks_ref[hh, :, :dh] = k_ref[:, lanes]
        vt_ref[hh, :dh, :] = v_ref[:, lanes].astype(F32).T.astype(BF16)

        kmean = jnp.dot(averager, k_ref[:, lanes], preferred_element_type=F32)
        gate = jnp.dot(kmean.astype(BF16), qt_ref[hh, :dh, :], preferred_element_type=F32)
        gm = jnp.where(valid, gate, NEG_INF)
        cnt = jnp.zeros(gate.shape, F32)
        for s in range(1, nb):
            gs = pltpu.roll(gm, s, axis=0)
            tie = jnp.where(((blk_id - s) % nb) < blk_id, 1.0, 0.0)
            cnt = cnt + jnp.where(gs > gm, 1.0, jnp.where(gs == gm, tie, 0.0))
        picked = jnp.where(blk_id < q_blk - 1, far_bias, 0.0)
        row = jnp.where(valid, jnp.where(cnt < MOBA_TOPK, picked, MASK_VALUE), 0.0)
        row_hi = row.astype(BF16).astype(F32)
        row_lo = jnp.where(row > 0.5 * MASK_VALUE, row - row_hi, 0.0)
        qt_ref[hh, dh:dh + 2 * nb, :] = jnp.concatenate([row_hi, row_lo], axis=0).astype(BF16)

    def scores(hh, pos, i):
        for jb in range(i + 1):
            rows = slice(jb * blk, (jb + 1) * blk)
            s_ref[hh, pos % 2, rows, :] = jnp.dot(
                ks_ref[hh, rows, :], qt_ref[hh, :, i * blk:(i + 1) * blk],
                preferred_element_type=F32)

    def softmax(hh, pos, i):
        slot = pos % 2
        maxes = []
        for jb in range(i + 1):
            rows = slice(jb * blk, (jb + 1) * blk)
            s_t = s_ref[hh, slot, rows, :]
            if jb >= i - 1:
                s_t = s_t + bias_ref[hh, i - jb]
                s_ref[hh, slot, rows, :] = s_t
            maxes.append(jnp.max(s_t, axis=0, keepdims=True))
        m = functools.reduce(jnp.maximum, maxes)
        for jb in range(i + 1):
            rows = slice(jb * blk, (jb + 1) * blk)
            p_ref[hh, slot, rows, :] = jnp.exp2(s_ref[hh, slot, rows, :] - m).astype(BF16)

    def attend(hh, pos, i):
        acc = jnp.dot(vt_ref[hh, :, :(i + 1) * blk], p_ref[hh, pos % 2, :(i + 1) * blk, :],
                      preferred_element_type=F32)
        out = acc[:dh, :] / acc[dh:dh + 1, :]
        o_ref[i * blk:(i + 1) * blk, hh * dh:(hh + 1) * dh] = out.T.astype(o_ref.dtype)

    for hh in heads:
        prepare(hh)

    order = list(range(nb - 1, -1, -1))
    for hh in heads:
        scores(hh, 0, order[0])
    for pos, i in enumerate(order):
        for hh in heads:
            if pos + 1 < nb:
                scores(hh, pos + 1, order[pos + 1])
            softmax(hh, pos, i)
            if pos > 0:
                attend(hh, pos - 1, order[pos - 1])
    for hh in heads:
        attend(hh, nb - 1, order[-1])


def moba_bucket_tables():
    f32 = np.float32

    def bucket(dist):
        n = np.maximum(dist, 0)
        max_exact = REL_BUCKETS // 2
        nf = np.maximum(n, 1).astype(f32)
        large = max_exact + (np.log(nf / f32(max_exact)) / f32(math.log(REL_MAX_DIST / max_exact))
                             * f32(REL_BUCKETS - max_exact)).astype(np.int32)
        large = np.minimum(large, REL_BUCKETS - 1)
        return np.where(n < max_exact, n, large)

    lane = np.arange(2 * MOBA_BLOCK)
    own = np.where(lane < MOBA_BLOCK, bucket(lane), -1)
    adj = bucket((lane + MOBA_BLOCK) % (2 * MOBA_BLOCK))
    strips = np.stack([own, adj]).astype(np.int32)
    return jnp.asarray(np.broadcast_to(strips[:, None, :], (2, SUBLANES, 2 * MOBA_BLOCK)))


def moba(proj, rel_bias, batch, side_w, side_layer):
    t = proj.shape[0]
    hps = MOBA_HEADS_PER_STEP
    groups = MOBA_HEADS // hps
    width = hps * MOBA_DH
    col = lambda off: pl.BlockSpec((SEQ, width), lambda hg, b: (b, off + hg))
    side = SideCast(side_w, side_layer, groups * batch, lambda hg, b: hg * batch + b)
    in_specs = [pl.BlockSpec(memory_space=pltpu.SMEM),
                pl.BlockSpec((2, SUBLANES, 2 * MOBA_BLOCK), lambda hg, b: (0, 0, 0)),
                col(0), col(groups), col(2 * groups)]
    return pl.pallas_call(
        _with_side_cast(_moba_kernel, len(in_specs), 1),
        grid=(groups, batch),
        in_specs=in_specs + [side.in_spec],
        out_specs=[pl.BlockSpec((SEQ, width), lambda hg, b: (b, hg)), side.out_spec],
        out_shape=[jax.ShapeDtypeStruct((t, MOBA_HEADS * MOBA_DH), BF16), side.out_shape],
        scratch_shapes=[pltpu.VMEM((hps, 2, MOBA_BLOCK, MOBA_BLOCK), F32),
                        pltpu.VMEM((hps, 2 * MOBA_DH, SEQ), BF16),
                        pltpu.VMEM((hps, SEQ, 2 * MOBA_DH), BF16),
                        pltpu.VMEM((hps, MOBA_DH + 2 * SUBLANES, SEQ), BF16),
                        pltpu.VMEM((hps, 2, SEQ, MOBA_BLOCK), F32),
                        pltpu.VMEM((hps, 2, SEQ, MOBA_BLOCK), BF16)],
        compiler_params=_params("arbitrary", "arbitrary"),
        name="moba",
    )(rel_bias, moba_bucket_tables(), proj, proj, proj, side.w)


def moba_head_gain(q_norm, k_norm):
    qg = jnp.tile(q_norm * (MOBA_DH ** -0.5 * LOG2E), MOBA_HEADS)
    return jnp.concatenate([qg, jnp.tile(k_norm, MOBA_HEADS)])


def kernel(x, norm_mix, norm_ffn, ev_w_in, ret_norm, lru_conv_w, lru_conv_b, lru_w_a, lru_b_a,
           lru_w_i, lru_b_i, lru_lambda, ev_w_out, od_w_in, q_norm, k_norm, od_w_out, rel_bias,
           ffn_w_up, ffn_conv_w, ffn_conv_b, ffn_w_down):
    batch, seq, d = x.shape
    assert (seq, d) == (SEQ, D_MODEL)
    xf = x.reshape(batch * seq, d)


    tm_in = 1024
    cos, sin = rotary_tables()
    pos_rows = lambda i, j: (i % (SEQ // tm_in), 0)
    fixed = lambda i, j: (0, 0)
    proj = norm_matmul(
        xf, norm_mix[0], ev_w_in, 0, tm=tm_in, tn=RET_W, out_dtype=BF16,
        col_epilogues=[("rotary", 1.0), ("rotary", RET_DK ** -0.5), None, ("silu",), ("conv",),
                       ("gelu",)],
        extras={"cos": (cos, (tm_in, RET_DK // 2), pos_rows),
                "sin": (sin, (tm_in, RET_DK // 2), pos_rows),
                "cw": (lru_conv_w[0], (LRU_CONV, LRU_W), fixed),
                "cb": (lru_conv_b[0].reshape(1, LRU_W), (1, LRU_W), fixed)})
    ret, w_out_b = retention(proj, retention_tables(), ret_norm[0], batch, ev_w_out, 0)
    w_ai = jnp.concatenate([lru_w_a[0], lru_w_i[0]], axis=-1).astype(BF16)
    b_ai = jnp.concatenate([lru_b_a[0].reshape(LRU_BLOCKS, LRU_BW),
                            lru_b_i[0].reshape(LRU_BLOCKS, LRU_BW)], axis=-1).reshape(1, 2 * LRU_W)
    lru = rglru(proj, w_ai, b_ai, lru_lambda[0], batch)
    xf, hn = proj_residual([ret, lru], w_out_b, xf, norm_ffn[0], tm=512)
    act, w_down_b = ffn_up(hn, ffn_w_up, ffn_conv_w, ffn_conv_b, 0, ffn_w_down, 0,
                           tm=2048, tn=512, sub=256)
    xf, w_in_b = matmul_residual(act, w_down_b, xf, tm=512, tn=1024, side_w=od_w_in, side_layer=0)

    head_norm = ("head_norm", MOBA_DH)
    proj = norm_matmul(
        xf, norm_mix[1], w_in_b, None, tm=1024, tn=MOBA_W, out_dtype=BF16,
        col_epilogues=[head_norm, head_norm, None],
        extras={"hg": (moba_head_gain(q_norm[0], k_norm[0]).reshape(1, 2 * MOBA_W),
                       (1, MOBA_W), lambda i, j: (0, jnp.minimum(j, 1)))})
    att, w_out_b = moba(proj, rel_bias, batch, od_w_out, 0)
    xf, hn = proj_residual([att], w_out_b, xf, norm_ffn[1], tm=512)
    act, w_down_b = ffn_up(hn, ffn_w_up, ffn_conv_w, ffn_conv_b, 1, ffn_w_down, 1,
                           tm=2048, tn=512, sub=256)
    xf = matmul_residual(act, w_down_b, xf, tm=512, tn=1024)
    return xf.reshape(batch, seq, d)
```

```python
import functools
import math

import jax
import jax.numpy as jnp
import numpy as np
from jax import lax
from jax.experimental import pallas as pl
from jax.experimental.pallas import tpu as pltpu

D_MODEL = 2048
SEQ = 2048
RET_HEADS = 4
RET_DK = 256
RET_DV = 256
RET_W = RET_HEADS * RET_DK
RET_CHUNK = 128
RET_HEADS_PER_STEP = 2
ROPE_BASE = 10000.0
LRU_W = 1024
LRU_BLOCKS = 8
LRU_BW = 128
LRU_CONV = 4
LRU_C = 8.0
LRU_SCAN_CHUNK = 64
LRU_BLOCKS_PER_STEP = 2
MOBA_HEADS = 16
MOBA_DH = 128
MOBA_W = MOBA_HEADS * MOBA_DH
MOBA_HEADS_PER_STEP = 4
MOBA_BLOCK = 256
MOBA_TOPK = 3
REL_BUCKETS = 32
REL_MAX_DIST = 128
D_FF = 5632
FFN_CONV = 3
EPS = 1e-6

VMEM_LIMIT_BYTES = 56 * 1024 * 1024
SUBLANES = 8

BF16 = jnp.bfloat16
F32 = jnp.float32
NEG_INF = float("-inf")
MASK_VALUE = -1e30
LOG2E = math.log2(math.e)


def _params(*sem):
    return pltpu.CompilerParams(dimension_semantics=sem, vmem_limit_bytes=VMEM_LIMIT_BYTES)


def _gelu_tanh(x):
    c = math.sqrt(2.0 / math.pi)
    return 0.5 * x * (1.0 + jnp.tanh(c * (x + 0.044715 * (x * x * x))))


def _rms_scale(x):
    return lax.rsqrt(jnp.mean(x * x, axis=-1, keepdims=True) + EPS)


class SideCast:
    def __init__(self, w, layer, n_steps, step_of):
        k, n = w.shape[1:]
        rows = k // n_steps
        assert rows * n_steps == k and rows % (2 * SUBLANES) == 0
        self.w = w
        self.in_spec = pl.BlockSpec((None, rows, n), lambda *ids: (layer, step_of(*ids), 0))
        self.out_spec = pl.BlockSpec((rows, n), lambda *ids: (step_of(*ids), 0))
        self.out_shape = jax.ShapeDtypeStruct((k, n), BF16)


def _with_side_cast(body, n_in, n_out):
    def wrapped(*refs):
        side_in, side_out = refs[n_in], refs[n_in + 1 + n_out]
        side_out[...] = side_in[...].astype(BF16)
        body(*refs[:n_in], *refs[n_in + 1:n_in + 1 + n_out], *refs[n_in + 2 + n_out:])
    return wrapped


def _norm_mm_kernel(*refs, extra_names, col_epilogues, sub, tiles_per_seq):
    n_extra = len(extra_names)
    x_ref, g_ref, w_ref = refs[:3]
    ex = dict(zip(extra_names, refs[3:3 + n_extra]))
    o_ref, hn_ref = refs[3 + n_extra:5 + n_extra]
    scratch = list(refs[5 + n_extra:])
    w_is_bf16 = w_ref.dtype == BF16
    wb_ref = w_ref if w_is_bf16 else scratch.pop(0)
    acc_ref = scratch.pop(0) if any(col_epilogues) else None
    halo_ref = scratch.pop(0) if ("conv",) in col_epilogues else None
    i, j = pl.program_id(0), pl.program_id(1)
    tm, tn = o_ref.shape
    n_sub = tm // sub

    def apply_epilogue(kind, a, r):
        rows = slice(r * sub, (r + 1) * sub)
        if kind[0] == "head_norm":
            heads = [a[:, s:s + kind[1]] for s in range(0, tn, kind[1])]
            return jnp.concatenate([h * _rms_scale(h) for h in heads], axis=-1) * ex["hg"][...]
        if kind[0] == "rotary":
            cos, sin = ex["cos"][rows, :], ex["sin"][rows, :]
            if kind[1] != 1.0:
                cos, sin = cos * kind[1], sin * kind[1]
            half = RET_DK // 2
            out = []
            for s in range(0, tn, RET_DK):
                x1, x2 = a[:, s:s + half], a[:, s + half:s + RET_DK]
                out += [x1 * cos - x2 * sin, x1 * sin + x2 * cos]
            return jnp.concatenate(out, axis=-1)
        if kind[0] == "silu":
            return a * jax.nn.sigmoid(a)
        if kind[0] == "gelu":
            return _gelu_tanh(a)
        if kind[0] == "conv":
            taps = ex["cw"].shape[0]
            fresh = (i % tiles_per_seq == 0) if r == 0 else False
            halo = jnp.where(fresh, 0.0, halo_ref[...])
            halo_ref[...] = a[sub - SUBLANES:, :]
            ext = jnp.concatenate([halo, a], axis=0)
            out = a * ex["cw"][taps - 1:taps, :] + ex["cb"][...]
            for d in range(1, taps):
                prev = pltpu.roll(ext, d, axis=0)[SUBLANES:, :]
                out = out + prev * ex["cw"][taps - 1 - d:taps - d, :]
            return out
        raise ValueError(kind)

    def plain():
        o_ref[...] = jnp.dot(hn_ref[...], w_ref[...].astype(BF16),
                             preferred_element_type=F32).astype(o_ref.dtype)

    def staged(norm_rows, kind):
        if not w_is_bf16:
            wb_ref[...] = w_ref[...].astype(BF16)

        def norm(r):
            x = x_ref[r * sub:(r + 1) * sub, :]
            hn_ref[r * sub:(r + 1) * sub, :] = (x * _rms_scale(x) * g_ref[...]).astype(BF16)

        def mm(r):
            acc = jnp.dot(hn_ref[r * sub:(r + 1) * sub, :], wb_ref[...],
                          preferred_element_type=F32)
            if kind:
                acc_ref[r % 2] = acc
            else:
                o_ref[r * sub:(r + 1) * sub, :] = acc.astype(o_ref.dtype)

        def epilogue(r):
            o_ref[r * sub:(r + 1) * sub, :] = apply_epilogue(
                kind, acc_ref[r % 2], r).astype(o_ref.dtype)

        if norm_rows:
            norm(0)
        for r in range(n_sub):
            if norm_rows and r + 1 < n_sub:
                norm(r + 1)
            mm(r)
            if kind and r > 0:
                epilogue(r - 1)
        if kind:
            epilogue(n_sub - 1)

    for col, kind in enumerate(col_epilogues):
        if col == 0 or kind:
            pl.when(j == col)(functools.partial(staged, col == 0, kind))
        else:
            pl.when(j == col)(plain)


def norm_matmul(x, g, w, layer, *, tm, tn, out_dtype, col_epilogues, extras=None, sub=256):
    t, d = x.shape
    n = w.shape[-1]
    assert len(col_epilogues) == n // tn
    if layer is None:
        assert w.dtype == BF16
        w_spec = pl.BlockSpec((d, tn), lambda i, j: (0, j))
    else:
        w_spec = pl.BlockSpec((None, d, tn), lambda i, j: (layer, 0, j))
    extras = extras or {}
    scratch = [pltpu.VMEM((tm, d), BF16)]
    if layer is not None:
        scratch.append(pltpu.VMEM((d, tn), BF16))
    if any(col_epilogues):
        scratch.append(pltpu.VMEM((2, sub, tn), F32))
    if ("conv",) in col_epilogues:
        scratch.append(pltpu.VMEM((SUBLANES, tn), F32))
    return pl.pallas_call(
        functools.partial(_norm_mm_kernel, extra_names=tuple(extras),
                          col_epilogues=tuple(col_epilogues), sub=sub, tiles_per_seq=SEQ // tm),
        grid=(t // tm, n // tn),
        in_specs=[
            pl.BlockSpec((tm, d), lambda i, j: (i, 0)),
            pl.BlockSpec((1, d), lambda i, j: (0, 0)),
            w_spec,
        ] + [pl.BlockSpec(block, index_map) for _, block, index_map in extras.values()],
        out_specs=pl.BlockSpec((tm, tn), lambda i, j: (i, j)),
        out_shape=jax.ShapeDtypeStruct((t, n), out_dtype),
        scratch_shapes=scratch,
        compiler_params=_params("arbitrary", "arbitrary"),
        name="norm_matmul",
    )(x, g.reshape(1, d), w, *[arr for arr, _, _ in extras.values()])


def _retention_kernel(q_ref, k_ref, v_ref, g_ref, decay_ref, zeta_ref, xi_ref,
                      gch_ref, rn_ref, o_ref, kv_ref, st_ref):
    n_chunks = q_ref.shape[0] // RET_CHUNK
    heads = range(kv_ref.shape[0])
    chunk_rows = lambda n: slice(n * RET_CHUNK, (n + 1) * RET_CHUNK)
    head_cols = lambda hh: slice(hh * RET_DK, (hh + 1) * RET_DK)

    for n in range(n_chunks):
        rows = chunk_rows(n)
        for hh in heads:
            cols = head_cols(hh)
            kz = (k_ref[rows, cols].astype(F32) * zeta_ref[hh]).astype(BF16)
            kv_ref[hh, n] = lax.dot_general(kz, v_ref[rows, cols], (((0,), (0,)), ((), ())),
                                            preferred_element_type=F32)

    states = [jnp.zeros((RET_DK, RET_DV), F32) for _ in heads]
    for n in range(n_chunks):
        for hh in heads:
            st_ref[hh, n] = states[hh].astype(BF16)
            states[hh] = gch_ref[hh] * states[hh] + kv_ref[hh, n]

    for n in range(n_chunks):
        rows = chunk_rows(n)
        for hh in heads:
            cols = head_cols(hh)
            qb, vb = q_ref[rows, cols], v_ref[rows, cols]
            scores = lax.dot_general(qb, k_ref[rows, cols], (((1,), (1,)), ((), ())),
                                     preferred_element_type=F32) * decay_ref[hh]
            inner = jnp.dot(scores.astype(BF16), vb, preferred_element_type=F32)
            cross = jnp.dot(qb, st_ref[hh, n], preferred_element_type=F32) * xi_ref[hh]
            out = inner + cross
            out = out * _rms_scale(out) * rn_ref[hh]
            o_ref[rows, cols] = (out * g_ref[rows, cols].astype(F32)).astype(o_ref.dtype)


def retention(proj, tables, ret_norm, batch, side_w, side_layer):
    decay, zeta, xi, gch = tables
    t = proj.shape[0]
    hps = RET_HEADS_PER_STEP
    groups = RET_HEADS // hps
    col = lambda off: pl.BlockSpec((SEQ, hps * RET_DK), lambda b, hg: (b, off + hg))
    per_head = lambda shape: pl.BlockSpec((hps,) + shape, lambda b, hg: (hg, 0, 0))
    side = SideCast(side_w, side_layer, batch * groups, lambda b, hg: b * groups + hg)
    in_specs = [col(0), col(groups), col(2 * groups), col(3 * groups),
                per_head((RET_CHUNK, RET_CHUNK)), per_head((RET_CHUNK, RET_DK)),
                per_head((RET_CHUNK, RET_DV)), per_head((1, RET_DV)), per_head((1, RET_DV))]
    return pl.pallas_call(
        _with_side_cast(_retention_kernel, len(in_specs), 1),
        grid=(batch, groups),
        in_specs=in_specs + [side.in_spec],
        out_specs=[pl.BlockSpec((SEQ, hps * RET_DV), lambda b, hg: (b, hg)), side.out_spec],
        out_shape=[jax.ShapeDtypeStruct((t, RET_HEADS * RET_DV), BF16), side.out_shape],
        scratch_shapes=[pltpu.VMEM((hps, SEQ // RET_CHUNK, RET_DK, RET_DV), F32),
                        pltpu.VMEM((hps, SEQ // RET_CHUNK, RET_DK, RET_DV), BF16)],
        compiler_params=_params("arbitrary", "arbitrary"),
        name="retention",
    )(proj, proj, proj, proj, decay, zeta, xi, gch,
      ret_norm.reshape(RET_HEADS, 1, RET_DV), side.w)


def rotary_tables():
    f32 = np.float32
    half = RET_DK // 2
    freqs = f32(ROPE_BASE) ** (-np.arange(half, dtype=f32) / f32(half))
    ang = np.arange(SEQ).astype(f32)[:, None] * freqs[None, :]
    return jnp.asarray(np.cos(ang)), jnp.asarray(np.sin(ang))


def retention_tables():
    f32 = np.float32
    log_g = np.log1p(-np.exp2(f32(-5.0) - np.arange(RET_HEADS, dtype=f32)))
    i = np.arange(RET_CHUNK)
    diff = i[:, None] - i[None, :]
    decay = np.where(diff >= 0, np.exp(log_g[:, None, None] * np.maximum(diff, 0).astype(f32)),
                     f32(0.0))
    zeta = np.exp(log_g[None, :] * (RET_CHUNK - 1 - i).astype(f32)[:, None])
    xi = np.exp(log_g[None, :] * (i + 1).astype(f32)[:, None])
    g_chunk = np.exp(log_g * f32(RET_CHUNK))
    zeta_b = np.broadcast_to(zeta.T[:, :, None], (RET_HEADS, RET_CHUNK, RET_DK))
    xi_b = np.broadcast_to(xi.T[:, :, None], (RET_HEADS, RET_CHUNK, RET_DV))
    gch_b = np.broadcast_to(g_chunk[:, None, None], (RET_HEADS, 1, RET_DV))
    return tuple(jnp.asarray(a.astype(f32)) for a in (decay, zeta_b, xi_b, gch_b))


def _shift_rows(v, d, fill, row):
    if d % SUBLANES == 0:
        return jnp.concatenate([jnp.full((d, v.shape[1]), fill, v.dtype), v[:-d]], axis=0)
    return jnp.where(row >= d, pltpu.roll(v, d, axis=0), fill)


def _rglru_kernel(xc_ref, gg_ref, wai_ref, bai_ref, lam_ref, o_ref, a_ref, u_ref):
    s, width = xc_ref.shape
    for blk in range(width // LRU_BW):
        lanes = slice(blk * LRU_BW, (blk + 1) * LRU_BW)
        gates = slice(2 * blk * LRU_BW, 2 * (blk + 1) * LRU_BW)
        xc = xc_ref[:, lanes].astype(F32)
        y = jnp.dot(xc_ref[:, lanes], wai_ref[blk], preferred_element_type=F32) + bai_ref[:, gates]
        r = jax.nn.sigmoid(y[:, :LRU_BW])
        gi = jax.nn.sigmoid(y[:, LRU_BW:])
        neg_lam = -lam_ref[:, lanes]
        softplus = jnp.maximum(neg_lam, 0.0) + jnp.log1p(jnp.exp(-jnp.abs(neg_lam)))
        log_a = (-LRU_C) * r * softplus
        a = jnp.exp(log_a)
        a_ref[:, lanes] = a
        u_ref[:, lanes] = jnp.sqrt(1.0 - a * a) * (gi * xc)

    c = LRU_SCAN_CHUNK
    crow = lax.broadcasted_iota(jnp.int32, (c, width), 0)

    def chunk(n, h_prev):
        rows = pl.ds(pl.multiple_of(n * c, c), c)
        av, bv = a_ref[rows, :], u_ref[rows, :]
        d = 1
        while d < c:
            a_sh = _shift_rows(av, d, 1.0, crow)
            b_sh = _shift_rows(bv, d, 0.0, crow)
            bv = av * b_sh + bv
            av = av * a_sh
            d *= 2
        h = bv + av * h_prev
        o_ref[rows, :] = (h * gg_ref[rows, :].astype(F32)).astype(o_ref.dtype)
        return h[c - 1:c, :]

    lax.fori_loop(0, s // c, chunk, jnp.zeros((1, width), F32))


def rglru(proj, w_ai, b_ai, lam, batch):
    t = proj.shape[0]
    bps = LRU_BLOCKS_PER_STEP
    width = bps * LRU_BW
    x_off = 4 * RET_W // width
    g_off = x_off + LRU_W // width
    return pl.pallas_call(
        _rglru_kernel,
        grid=(batch, LRU_BLOCKS // bps),
        in_specs=[
            pl.BlockSpec((SEQ, width), lambda b, g: (b, x_off + g)),
            pl.BlockSpec((SEQ, width), lambda b, g: (b, g_off + g)),
            pl.BlockSpec((bps, LRU_BW, 2 * LRU_BW), lambda b, g: (g, 0, 0)),
            pl.BlockSpec((1, 2 * width), lambda b, g: (0, g)),
            pl.BlockSpec((1, width), lambda b, g: (0, g)),
        ],
        out_specs=pl.BlockSpec((SEQ, width), lambda b, g: (b, g)),
        out_shape=jax.ShapeDtypeStruct((t, LRU_W), BF16),
        scratch_shapes=[pltpu.VMEM((SEQ, width), F32), pltpu.VMEM((SEQ, width), F32)],
        compiler_params=_params("arbitrary", "arbitrary"),
        name="rglru",
    )(proj, proj, w_ai, b_ai, lam.reshape(1, LRU_W))


def _proj_res_kernel(*refs, splits, sub):
    part_refs = refs[:len(splits)]
    w_ref, x_ref, g_ref, o_ref, hn_ref, acc_ref = refs[len(splits):]

    def mm(r):
        rows = slice(r * sub, (r + 1) * sub)
        acc, off = None, 0
        for a_ref, k in zip(part_refs, splits):
            part = jnp.dot(a_ref[rows, :], w_ref[off:off + k, :], preferred_element_type=F32)
            acc = part if acc is None else acc + part
            off += k
        acc_ref[r % 2] = acc

    def epilogue(r):
        rows = slice(r * sub, (r + 1) * sub)
        y = x_ref[rows, :] + acc_ref[r % 2]
        o_ref[rows, :] = y
        hn_ref[rows, :] = (y * _rms_scale(y) * g_ref[...]).astype(BF16)

    n_sub = x_ref.shape[0] // sub
    mm(0)
    for r in range(n_sub):
        if r + 1 < n_sub:
            mm(r + 1)
        epilogue(r)


def proj_residual(parts, w, x, g, *, tm, sub=256):
    t, n = x.shape
    splits = tuple(p.shape[1] for p in parts)
    row = lambda i: (i, 0)
    return pl.pallas_call(
        functools.partial(_proj_res_kernel, splits=splits, sub=sub),
        grid=(t // tm,),
        in_specs=[pl.BlockSpec((tm, k), row) for k in splits] + [
            pl.BlockSpec(w.shape, lambda i: (0, 0)),
            pl.BlockSpec((tm, n), row),
            pl.BlockSpec((1, n), lambda i: (0, 0)),
        ],
        out_specs=[pl.BlockSpec((tm, n), row), pl.BlockSpec((tm, n), row)],
        out_shape=[jax.ShapeDtypeStruct((t, n), F32), jax.ShapeDtypeStruct((t, n), BF16)],
        scratch_shapes=[pltpu.VMEM((2, sub, n), F32)],
        compiler_params=_params("arbitrary"),
        name="proj_residual",
    )(*parts, w, x, g.reshape(1, n))


def _ffn_up_kernel(hn_ref, wg_ref, wv_ref, cwg_ref, cwv_ref, cbg_ref, cbv_ref, o_ref,
                   wb_ref, u_ref, carry_ref, *, tiles_per_seq, sub):
    i = pl.program_id(1)
    tm = hn_ref.shape[0]
    tn = wg_ref.shape[1]

    @pl.when(i == 0)
    def _():
        wb_ref[:, :tn] = wg_ref[...].astype(BF16)
        wb_ref[:, tn:] = wv_ref[...].astype(BF16)

    def up(r):
        u_ref[r % 2, SUBLANES:, :] = jnp.dot(hn_ref[r * sub:(r + 1) * sub, :], wb_ref[...],
                                             preferred_element_type=F32)

    def conv(slot, cols, cw_ref, cb_ref):
        out = u_ref[slot, SUBLANES:, cols] * cw_ref[FFN_CONV - 1:FFN_CONV, :] + cb_ref[...]
        for d in range(1, FFN_CONV):
            prev = u_ref[slot, SUBLANES - d:SUBLANES - d + sub, cols]
            out = out + prev * cw_ref[FFN_CONV - 1 - d:FFN_CONV - d, :]
        return out

    seq_start = (i % tiles_per_seq) == 0
    u_ref[0, :SUBLANES, :] = jnp.where(seq_start, 0.0, carry_ref[...])
    up(0)
    for r in range(tm // sub):
        slot = r % 2
        u_ref[1 - slot, :SUBLANES, :] = u_ref[slot, sub:, :]
        if (r + 1) * sub < tm:
            up(r + 1)
        gate = conv(slot, slice(0, tn), cwg_ref, cbg_ref)
        val = conv(slot, slice(tn, 2 * tn), cwv_ref, cbv_ref)
        o_ref[r * sub:(r + 1) * sub, :] = (_gelu_tanh(gate) * val).astype(o_ref.dtype)
    carry_ref[...] = u_ref[(tm // sub) % 2, :SUBLANES, :]


def ffn_up(hn, w_up, conv_w, conv_b, layer, side_w, side_layer, *, tm, tn, sub):
    t, d = hn.shape
    nj, ni = D_FF // tn, t // tm
    cb = conv_b.reshape(conv_b.shape[0], 1, 2 * D_FF)
    per_col = lambda rows, off: pl.BlockSpec((None, rows, tn), lambda j, i: (layer, 0, off + j))
    side = SideCast(side_w, side_layer, nj * ni, lambda j, i: j * ni + i)
    in_specs = [pl.BlockSpec((tm, d), lambda j, i: (i, 0)),
                per_col(d, 0), per_col(d, nj),
                per_col(FFN_CONV, 0), per_col(FFN_CONV, nj),
                per_col(1, 0), per_col(1, nj)]
    kern = functools.partial(_ffn_up_kernel, tiles_per_seq=SEQ // tm, sub=sub)
    return pl.pallas_call(
        _with_side_cast(kern, len(in_specs), 1),
        grid=(nj, ni),
        in_specs=in_specs + [side.in_spec],
        out_specs=[pl.BlockSpec((tm, tn), lambda j, i: (i, j)), side.out_spec],
        out_shape=[jax.ShapeDtypeStruct((t, D_FF), BF16), side.out_shape],
        scratch_shapes=[pltpu.VMEM((d, 2 * tn), BF16),
                        pltpu.VMEM((2, SUBLANES + sub, 2 * tn), F32),
                        pltpu.VMEM((SUBLANES, 2 * tn), F32)],
        compiler_params=_params("arbitrary", "arbitrary"),
        name="ffn_up",
    )(hn, w_up, w_up, conv_w, conv_w, cb, cb, side.w)


def _mm_res_kernel(a_ref, w_ref, x_ref, o_ref):
    o_ref[...] = x_ref[...] + jnp.dot(a_ref[...], w_ref[...], preferred_element_type=F32)


def matmul_residual(a, w, x, *, tm, tn, side_w=None, side_layer=None):
    t, k = a.shape
    n = w.shape[1]
    nj, ni = n // tn, t // tm
    in_specs = [pl.BlockSpec((tm, k), lambda j, i: (i, 0)),
                pl.BlockSpec((k, tn), lambda j, i: (0, j)),
                pl.BlockSpec((tm, tn), lambda j, i: (i, j))]
    out_specs = [pl.BlockSpec((tm, tn), lambda j, i: (i, j))]
    out_shape = [jax.ShapeDtypeStruct((t, n), F32)]
    operands = [a, w, x]
    kern = _mm_res_kernel
    if side_w is not None:
        side = SideCast(side_w, side_layer, nj * ni, lambda j, i: j * ni + i)
        kern = _with_side_cast(kern, len(in_specs), 1)
        in_specs.append(side.in_spec)
        out_specs.append(side.out_spec)
        out_shape.append(side.out_shape)
        operands.append(side.w)
    outs = pl.pallas_call(
        kern,
        grid=(nj, ni),
        in_specs=in_specs,
        out_specs=out_specs,
        out_shape=out_shape,
        compiler_params=_params("arbitrary", "arbitrary"),
        name="matmul_residual",
    )(*operands)
    return outs if side_w is not None else outs[0]


def _moba_kernel(relb_ref, bkt_ref, q_ref, k_ref, v_ref, o_ref,
                 bias_ref, qt_ref, ks_ref, vt_ref, s_ref, p_ref):
    hg, b = pl.program_id(0), pl.program_id(1)
    nb = SEQ // MOBA_BLOCK
    blk = MOBA_BLOCK
    dh = MOBA_DH
    heads = range(qt_ref.shape[0])

    @pl.when((hg == 0) & (b == 0))
    def _():
        key_blk = lax.broadcasted_iota(jnp.int32, (SEQ, dh), 0) // blk
        lane = lax.broadcasted_iota(jnp.int32, (SEQ, dh), 1)
        onehot = (lane < 2 * nb) & ((lane % nb) == key_blk)
        for hh in heads:
            ks_ref[hh, :, dh:] = jnp.where(onehot, 1.0, 0.0).astype(BF16)
            qt_ref[hh, dh:, :] = jnp.zeros((dh, SEQ), BF16)
            vt_ref[hh, dh:, :] = jnp.ones((vt_ref.shape[1] - dh, SEQ), BF16)

    @pl.when(b == 0)
    def _():
        for hh in heads:
            for m in range(2):
                bk = bkt_ref[m]
                strip = jnp.full(bk.shape, NEG_INF, F32)
                for t in range(REL_BUCKETS):
                    strip = jnp.where(bk == t, relb_ref[t, hg * len(heads) + hh] * LOG2E, strip)
                tiled = jnp.concatenate([strip] * (blk // SUBLANES), axis=0)
                toeplitz = pltpu.roll(tiled, 0, axis=1, stride=1, stride_axis=0)
                bias_ref[hh, m] = toeplitz[:, :blk]

    blk_id = lax.broadcasted_iota(jnp.int32, (nb, SEQ), 0)
    q_blk = lax.broadcasted_iota(jnp.int32, (nb, SEQ), 1) // blk
    averager = jnp.where(blk_id == q_blk, 1.0 / blk, 0.0).astype(BF16)
    valid = blk_id < q_blk

    def prepare(hh):
        lanes = slice(hh * dh, (hh + 1) * dh)
        far_bias = relb_ref[REL_BUCKETS - 1, hg * len(heads) + hh] * LOG2E
        qt_ref[hh, :dh, :] = q_ref[:, lanes].astype(F32).T.astype(BF16)
        ks_ref[hh, :, :dh] = k_ref[:, lanes]
        vt_ref[hh, :dh, :] = v_ref[:, lanes].astype(F32).T.astype(BF16)

        kmean = jnp.dot(averager, k_ref[:, lanes], preferred_element_type=F32)
        gate = jnp.dot(kmean.astype(BF16), qt_ref[hh, :dh, :], preferred_element_type=F32)
        gm = jnp.where(valid, gate, NEG_INF)
        cnt = jnp.zeros(gate.shape, F32)
        for s in range(1, nb):
            gs = pltpu.roll(gm, s, axis=0)
            tie = jnp.where(((blk_id - s) % nb) < blk_id, 1.0, 0.0)
            cnt = cnt + jnp.where(gs > gm, 1.0, jnp.where(gs == gm, tie, 0.0))
        picked = jnp.where(blk_id < q_blk - 1, far_bias, 0.0)
        row = jnp.where(valid, jnp.where(cnt < MOBA_TOPK, picked, MASK_VALUE), 0.0)
        row_hi = row.astype(BF16).astype(F32)
        row_lo = jnp.where(row > 0.5 * MASK_VALUE, row - row_hi, 0.0)
        qt_ref[hh, dh:dh + 2 * nb, :] = jnp.concatenate([row_hi, row_lo], axis=0).astype(BF16)

    def scores(hh, pos, i):
        for jb in range(i + 1):
            rows = slice(jb * blk, (jb + 1) * blk)
            s_ref[hh, pos % 2, rows, :] = jnp.dot(
                ks_ref[hh, rows, :], qt_ref[hh, :, i * blk:(i + 1) * blk],
                preferred_element_type=F32)

    def softmax(hh, pos, i):
        slot = pos % 2
        maxes = []
        for jb in range(i + 1):
            rows = slice(jb * blk, (jb + 1) * blk)
            s_t = s_ref[hh, slot, rows, :]
            if jb >= i - 1:
                s_t = s_t + bias_ref[hh, i - jb]
                s_ref[hh, slot, rows, :] = s_t
            maxes.append(jnp.max(s_t, axis=0, keepdims=True))
        m = functools.reduce(jnp.maximum, maxes)
        for jb in range(i + 1):
            rows = slice(jb * blk, (jb + 1) * blk)
            p_ref[hh, slot, rows, :] = jnp.exp2(s_ref[hh, slot, rows, :] - m).astype(BF16)

    def attend(hh, pos, i):
        acc = jnp.dot(vt_ref[hh, :, :(i + 1) * blk], p_ref[hh, pos % 2, :(i + 1) * blk, :],
                      preferred_element_type=F32)
        out = acc[:dh, :] / acc[dh:dh + 1, :]
        o_ref[i * blk:(i + 1) * blk, hh * dh:(hh + 1) * dh] = out.T.astype(o_ref.dtype)

    for hh in heads:
        prepare(hh)

    order = list(range(nb - 1, -1, -1))
    for hh in heads:
        scores(hh, 0, order[0])
    for pos, i in enumerate(order):
        for hh in heads:
            if pos + 1 < nb:
                scores(hh, pos + 1, order[pos + 1])
            softmax(hh, pos, i)
            if pos > 0:
                attend(hh, pos - 1, order[pos - 1])
    for hh in heads:
        attend(hh, nb - 1, order[-1])


def moba_bucket_tables():
    f32 = np.float32

    def bucket(dist):
        n = np.maximum(dist, 0)
        max_exact = REL_BUCKETS // 2
        nf = np.maximum(n, 1).astype(f32)
        large = max_exact + (np.log(nf / f32(max_exact)) / f32(math.log(REL_MAX_DIST / max_exact))
                             * f32(REL_BUCKETS - max_exact)).astype(np.int32)
        large = np.minimum(large, REL_BUCKETS - 1)
        return np.where(n < max_exact, n, large)

    lane = np.arange(2 * MOBA_BLOCK)
    own = np.where(lane < MOBA_BLOCK, bucket(lane), -1)
    adj = bucket((lane + MOBA_BLOCK) % (2 * MOBA_BLOCK))
    strips = np.stack([own, adj]).astype(np.int32)
    return jnp.asarray(np.broadcast_to(strips[:, None, :], (2, SUBLANES, 2 * MOBA_BLOCK)))


def moba(proj, rel_bias, batch, side_w, side_layer):
    t = proj.shape[0]
    hps = MOBA_HEADS_PER_STEP
    groups = MOBA_HEADS // hps
    width = hps * MOBA_DH
    col = lambda off: pl.BlockSpec((SEQ, width), lambda hg, b: (b, off + hg))
    side = SideCast(side_w, side_layer, groups * batch, lambda hg, b: hg * batch + b)
    in_specs = [pl.BlockSpec(memory_space=pltpu.SMEM),
                pl.BlockSpec((2, SUBLANES, 2 * MOBA_BLOCK), lambda hg, b: (0, 0, 0)),
                col(0), col(groups), col(2 * groups)]
    return pl.pallas_call(
        _with_side_cast(_moba_kernel, len(in_specs), 1),
        grid=(groups, batch),
        in_specs=in_specs + [side.in_spec],
        out_specs=[pl.BlockSpec((SEQ, width), lambda hg, b: (b, hg)), side.out_spec],
        out_shape=[jax.ShapeDtypeStruct((t, MOBA_HEADS * MOBA_DH), BF16), side.out_shape],
        scratch_shapes=[pltpu.VMEM((hps, 2, MOBA_BLOCK, MOBA_BLOCK), F32),
                        pltpu.VMEM((hps, 2 * MOBA_DH, SEQ), BF16),
                        pltpu.VMEM((hps, SEQ, 2 * MOBA_DH), BF16),
                        pltpu.VMEM((hps, MOBA_DH + 2 * SUBLANES, SEQ), BF16),
                        pltpu.VMEM((hps, 2, SEQ, MOBA_BLOCK), F32),
                        pltpu.VMEM((hps, 2, SEQ, MOBA_BLOCK), BF16)],
        compiler_params=_params("arbitrary", "arbitrary"),
        name="moba",
    )(rel_bias, moba_bucket_tables(), proj, proj, proj, side.w)


def moba_head_gain(q_norm, k_norm):
    qg = jnp.tile(q_norm * (MOBA_DH ** -0.5 * LOG2E), MOBA_HEADS)
    return jnp.concatenate([qg, jnp.tile(k_norm, MOBA_HEADS)])


def kernel(x, norm_mix, norm_ffn, ev_w_in, ret_norm, lru_conv_w, lru_conv_b, lru_w_a, lru_b_a,
           lru_w_i, lru_b_i, lru_lambda, ev_w_out, od_w_in, q_norm, k_norm, od_w_out, rel_bias,
           ffn_w_up, ffn_conv_w, ffn_conv_b, ffn_w_down):
    batch, seq, d = x.shape
    assert (seq, d) == (SEQ, D_MODEL)
    xf = x.reshape(batch * seq, d)


    tm_in = 1024
    cos, sin = rotary_tables()
    pos_rows = lambda i, j: (i % (SEQ // tm_in), 0)
    fixed = lambda i, j: (0, 0)
    proj = norm_matmul(
        xf, norm_mix[0], ev_w_in, 0, tm=tm_in, tn=RET_W, out_dtype=BF16,
        col_epilogues=[("rotary", 1.0), ("rotary", RET_DK ** -0.5), None, ("silu",), ("conv",),
                       ("gelu",)],
        extras={"cos": (cos, (tm_in, RET_DK // 2), pos_rows),
                "sin": (sin, (tm_in, RET_DK // 2), pos_rows),
                "cw": (lru_conv_w[0], (LRU_CONV, LRU_W), fixed),
                "cb": (lru_conv_b[0].reshape(1, LRU_W), (1, LRU_W), fixed)})
    ret, w_out_b = retention(proj, retention_tables(), ret_norm[0], batch, ev_w_out, 0)
    w_ai = jnp.concatenate([lru_w_a[0], lru_w_i[0]], axis=-1).astype(BF16)
    b_ai = jnp.concatenate([lru_b_a[0].reshape(LRU_BLOCKS, LRU_BW),
                            lru_b_i[0].reshape(LRU_BLOCKS, LRU_BW)], axis=-1).reshape(1, 2 * LRU_W)
    lru = rglru(proj, w_ai, b_ai, lru_lambda[0], batch)
    xf, hn = proj_residual([ret, lru], w_out_b, xf, norm_ffn[0], tm=512)
    act, w_down_b = ffn_up(hn, ffn_w_up, ffn_conv_w, ffn_conv_b, 0, ffn_w_down, 0,
                           tm=2048, tn=512, sub=256)
    xf, w_in_b = matmul_residual(act, w_down_b, xf, tm=512, tn=1024, side_w=od_w_in, side_layer=0)

    head_norm = ("head_norm", MOBA_DH)
    proj = norm_matmul(
        xf, norm_mix[1], w_in_b, None, tm=1024, tn=MOBA_W, out_dtype=BF16,
        col_epilogues=[head_norm, head_norm, None],
        extras={"hg": (moba_head_gain(q_norm[0], k_norm[0]).reshape(1, 2 * MOBA_W),
                       (1, MOBA_W), lambda i, j: (0, jnp.minimum(j, 1)))})
    att, w_out_b = moba(proj, rel_bias, batch, od_w_out, 0)
    xf, hn = proj_residual([att], w_out_b, xf, norm_ffn[1], tm=512)
    act, w_down_b = ffn_up(hn, ffn_w_up, ffn_conv_w, ffn_conv_b, 1, ffn_w_down, 1,
                           tm=2048, tn=512, sub=256)
    xf = matmul_residual(act, w_down_b, xf, tm=512, tn=1024)
    return xf.reshape(batch, seq, d)
```

```python
import functools
import math

import jax
import jax.numpy as jnp
import numpy as np
from jax import lax
from jax.experimental import pallas as pl
from jax.experimental.pallas import tpu as pltpu

D_MODEL = 2048
SEQ = 2048
RET_HEADS = 4
RET_DK = 256
RET_DV = 256
RET_W = RET_HEADS * RET_DK
RET_CHUNK = 128
RET_HEADS_PER_STEP = 2
ROPE_BASE = 10000.0
LRU_W = 1024
LRU_BLOCKS = 8
LRU_BW = 128
LRU_CONV = 4
LRU_C = 8.0
LRU_SCAN_CHUNK = 64
LRU_BLOCKS_PER_STEP = 2
MOBA_HEADS = 16
MOBA_DH = 128
MOBA_W = MOBA_HEADS * MOBA_DH
MOBA_HEADS_PER_STEP = 4
MOBA_BLOCK = 256
MOBA_TOPK = 3
REL_BUCKETS = 32
REL_MAX_DIST = 128
D_FF = 5632
FFN_CONV = 3
EPS = 1e-6

V7X_VMEM_BYTES = 64 * 1024 * 1024
VMEM_LIMIT_BYTES = V7X_VMEM_BYTES - 8 * 1024 * 1024
SUBLANES = 8

IN_PROJ_ROWS = 1024
OUT_PROJ_ROWS = 512
FFN_UP_TILE = (2048, 512)
FFN_DOWN_TILE = (512, 1024)
EPILOGUE_SUB_ROWS = 256

BF16 = jnp.bfloat16
F32 = jnp.float32
NEG_INF = float("-inf")
MASK_VALUE = -1e30
LOG2E = math.log2(math.e)


def _params(*sem):
    return pltpu.CompilerParams(dimension_semantics=sem, vmem_limit_bytes=VMEM_LIMIT_BYTES)


def _gelu_tanh(x):
    c = math.sqrt(2.0 / math.pi)
    return 0.5 * x * (1.0 + jnp.tanh(c * (x + 0.044715 * (x * x * x))))


def _rms_scale(x):
    return lax.rsqrt(jnp.mean(x * x, axis=-1, keepdims=True) + EPS)


class SideCast:
    def __init__(self, w, layer, n_steps, step_of):
        k, n = w.shape[1:]
        rows = k // n_steps
        assert rows * n_steps == k and rows % (2 * SUBLANES) == 0
        self.w = w
        self.in_spec = pl.BlockSpec((None, rows, n), lambda *ids: (layer, step_of(*ids), 0))
        self.out_spec = pl.BlockSpec((rows, n), lambda *ids: (step_of(*ids), 0))
        self.out_shape = jax.ShapeDtypeStruct((k, n), BF16)


def _with_side_cast(body, n_in, n_out):
    def wrapped(*refs):
        side_in, side_out = refs[n_in], refs[n_in + 1 + n_out]
        side_out[...] = side_in[...].astype(BF16)
        body(*refs[:n_in], *refs[n_in + 1:n_in + 1 + n_out], *refs[n_in + 2 + n_out:])
    return wrapped


def _norm_mm_kernel(*refs, extra_names, col_epilogues, sub, tiles_per_seq):
    n_extra = len(extra_names)
    x_ref, g_ref, w_ref = refs[:3]
    ex = dict(zip(extra_names, refs[3:3 + n_extra]))
    o_ref, hn_ref = refs[3 + n_extra:5 + n_extra]
    scratch = list(refs[5 + n_extra:])
    w_is_bf16 = w_ref.dtype == BF16
    wb_ref = w_ref if w_is_bf16 else scratch.pop(0)
    acc_ref = scratch.pop(0) if any(col_epilogues) else None
    halo_ref = scratch.pop(0) if ("conv",) in col_epilogues else None
    i, j = pl.program_id(0), pl.program_id(1)
    tm, tn = o_ref.shape
    n_sub = tm // sub

    def apply_epilogue(kind, a, r):
        rows = slice(r * sub, (r + 1) * sub)
        if kind[0] == "head_norm":
            heads = [a[:, s:s + kind[1]] for s in range(0, tn, kind[1])]
            return jnp.concatenate([h * _rms_scale(h) for h in heads], axis=-1) * ex["hg"][...]
        if kind[0] == "rotary":
            cos, sin = ex["cos"][rows, :], ex["sin"][rows, :]
            if kind[1] != 1.0:
                cos, sin = cos * kind[1], sin * kind[1]
            half = RET_DK // 2
            out = []
            for s in range(0, tn, RET_DK):
                x1, x2 = a[:, s:s + half], a[:, s + half:s + RET_DK]
                out += [x1 * cos - x2 * sin, x1 * sin + x2 * cos]
            return jnp.concatenate(out, axis=-1)
        if kind[0] == "silu":
            return a * jax.nn.sigmoid(a)
        if kind[0] == "gelu":
            return _gelu_tanh(a)
        if kind[0] == "conv":
            taps = ex["cw"].shape[0]
            fresh = (i % tiles_per_seq == 0) if r == 0 else False
            halo = jnp.where(fresh, 0.0, halo_ref[...])
            halo_ref[...] = a[sub - SUBLANES:, :]
            ext = jnp.concatenate([halo, a], axis=0)
            out = a * ex["cw"][taps - 1:taps, :] + ex["cb"][...]
            for d in range(1, taps):
                prev = pltpu.roll(ext, d, axis=0)[SUBLANES:, :]
                out = out + prev * ex["cw"][taps - 1 - d:taps - d, :]
            return out
        raise ValueError(kind)

    def plain():
        o_ref[...] = jnp.dot(hn_ref[...], w_ref[...].astype(BF16),
                             preferred_element_type=F32).astype(o_ref.dtype)

    def staged(norm_rows, kind):
        if not w_is_bf16:
            wb_ref[...] = w_ref[...].astype(BF16)

        def norm(r):
            x = x_ref[r * sub:(r + 1) * sub, :]
            hn_ref[r * sub:(r + 1) * sub, :] = (x * _rms_scale(x) * g_ref[...]).astype(BF16)

        def mm(r):
            acc = jnp.dot(hn_ref[r * sub:(r + 1) * sub, :], wb_ref[...],
                          preferred_element_type=F32)
            if kind:
                acc_ref[r % 2] = acc
            else:
                o_ref[r * sub:(r + 1) * sub, :] = acc.astype(o_ref.dtype)

        def epilogue(r):
            o_ref[r * sub:(r + 1) * sub, :] = apply_epilogue(
                kind, acc_ref[r % 2], r).astype(o_ref.dtype)

        if norm_rows:
            norm(0)
        for r in range(n_sub):
            if norm_rows and r + 1 < n_sub:
                norm(r + 1)
            mm(r)
            if kind and r > 0:
                epilogue(r - 1)
        if kind:
            epilogue(n_sub - 1)

    for col, kind in enumerate(col_epilogues):
        if col == 0 or kind:
            pl.when(j == col)(functools.partial(staged, col == 0, kind))
        else:
            pl.when(j == col)(plain)


def norm_matmul(x, g, w, layer, *, tm, tn, out_dtype, col_epilogues, extras=None,
                sub=EPILOGUE_SUB_ROWS):
    t, d = x.shape
    n = w.shape[-1]
    assert len(col_epilogues) == n // tn
    if layer is None:
        assert w.dtype == BF16
        w_spec = pl.BlockSpec((d, tn), lambda i, j: (0, j))
    else:
        w_spec = pl.BlockSpec((None, d, tn), lambda i, j: (layer, 0, j))
    extras = extras or {}
    scratch = [pltpu.VMEM((tm, d), BF16)]
    if layer is not None:
        scratch.append(pltpu.VMEM((d, tn), BF16))
    if any(col_epilogues):
        scratch.append(pltpu.VMEM((2, sub, tn), F32))
    if ("conv",) in col_epilogues:
        scratch.append(pltpu.VMEM((SUBLANES, tn), F32))
    return pl.pallas_call(
        functools.partial(_norm_mm_kernel, extra_names=tuple(extras),
                          col_epilogues=tuple(col_epilogues), sub=sub, tiles_per_seq=SEQ // tm),
        grid=(t // tm, n // tn),
        in_specs=[
            pl.BlockSpec((tm, d), lambda i, j: (i, 0)),
            pl.BlockSpec((1, d), lambda i, j: (0, 0)),
            w_spec,
        ] + [pl.BlockSpec(block, index_map) for _, block, index_map in extras.values()],
        out_specs=pl.BlockSpec((tm, tn), lambda i, j: (i, j)),
        out_shape=jax.ShapeDtypeStruct((t, n), out_dtype),
        scratch_shapes=scratch,
        compiler_params=_params("arbitrary", "arbitrary"),
        name="norm_matmul",
    )(x, g.reshape(1, d), w, *[arr for arr, _, _ in extras.values()])


def _retention_kernel(q_ref, k_ref, v_ref, g_ref, decay_ref, zeta_ref, xi_ref,
                      gch_ref, rn_ref, o_ref, kv_ref, st_ref):
    n_chunks = q_ref.shape[0] // RET_CHUNK
    heads = range(kv_ref.shape[0])
    chunk_rows = lambda n: slice(n * RET_CHUNK, (n + 1) * RET_CHUNK)
    head_cols = lambda hh: slice(hh * RET_DK, (hh + 1) * RET_DK)

    for n in range(n_chunks):
        rows = chunk_rows(n)
        for hh in heads:
            cols = head_cols(hh)
            kz = (k_ref[rows, cols].astype(F32) * zeta_ref[hh]).astype(BF16)
            kv_ref[hh, n] = lax.dot_general(kz, v_ref[rows, cols], (((0,), (0,)), ((), ())),
                                            preferred_element_type=F32)

    states = [jnp.zeros((RET_DK, RET_DV), F32) for _ in heads]
    for n in range(n_chunks):
        for hh in heads:
            st_ref[hh, n] = states[hh].astype(BF16)
            states[hh] = gch_ref[hh] * states[hh] + kv_ref[hh, n]

    for n in range(n_chunks):
        rows = chunk_rows(n)
        for hh in heads:
            cols = head_cols(hh)
            qb, vb = q_ref[rows, cols], v_ref[rows, cols]
            scores = lax.dot_general(qb, k_ref[rows, cols], (((1,), (1,)), ((), ())),
                                     preferred_element_type=F32) * decay_ref[hh]
            inner = jnp.dot(scores.astype(BF16), vb, preferred_element_type=F32)
            cross = jnp.dot(qb, st_ref[hh, n], preferred_element_type=F32) * xi_ref[hh]
            out = inner + cross
            out = out * _rms_scale(out) * rn_ref[hh]
            o_ref[rows, cols] = (out * g_ref[rows, cols].astype(F32)).astype(o_ref.dtype)


def retention(proj, tables, ret_norm, batch, side_w, side_layer):
    decay, zeta, xi, gch = tables
    t = proj.shape[0]
    hps = RET_HEADS_PER_STEP
    groups = RET_HEADS // hps
    col = lambda off: pl.BlockSpec((SEQ, hps * RET_DK), lambda b, hg: (b, off + hg))
    per_head = lambda shape: pl.BlockSpec((hps,) + shape, lambda b, hg: (hg, 0, 0))
    side = SideCast(side_w, side_layer, batch * groups, lambda b, hg: b * groups + hg)
    in_specs = [col(0), col(groups), col(2 * groups), col(3 * groups),
                per_head((RET_CHUNK, RET_CHUNK)), per_head((RET_CHUNK, RET_DK)),
                per_head((RET_CHUNK, RET_DV)), per_head((1, RET_DV)), per_head((1, RET_DV))]
    return pl.pallas_call(
        _with_side_cast(_retention_kernel, len(in_specs), 1),
        grid=(batch, groups),
        in_specs=in_specs + [side.in_spec],
        out_specs=[pl.BlockSpec((SEQ, hps * RET_DV), lambda b, hg: (b, hg)), side.out_spec],
        out_shape=[jax.ShapeDtypeStruct((t, RET_HEADS * RET_DV), BF16), side.out_shape],
        scratch_shapes=[pltpu.VMEM((hps, SEQ // RET_CHUNK, RET_DK, RET_DV), F32),
                        pltpu.VMEM((hps, SEQ // RET_CHUNK, RET_DK, RET_DV), BF16)],
        compiler_params=_params("arbitrary", "arbitrary"),
        name="retention",
    )(proj, proj, proj, proj, decay, zeta, xi, gch,
      ret_norm.reshape(RET_HEADS, 1, RET_DV), side.w)


def rotary_tables():
    f32 = np.float32
    half = RET_DK // 2
    freqs = f32(ROPE_BASE) ** (-np.arange(half, dtype=f32) / f32(half))
    ang = np.arange(SEQ).astype(f32)[:, None] * freqs[None, :]
    return jnp.asarray(np.cos(ang)), jnp.asarray(np.sin(ang))


def retention_tables():
    f32 = np.float32
    log_g = np.log1p(-np.exp2(f32(-5.0) - np.arange(RET_HEADS, dtype=f32)))
    i = np.arange(RET_CHUNK)
    diff = i[:, None] - i[None, :]
    decay = np.where(diff >= 0, np.exp(log_g[:, None, None] * np.maximum(diff, 0).astype(f32)),
                     f32(0.0))
    zeta = np.exp(log_g[None, :] * (RET_CHUNK - 1 - i).astype(f32)[:, None])
    xi = np.exp(log_g[None, :] * (i + 1).astype(f32)[:, None])
    g_chunk = np.exp(log_g * f32(RET_CHUNK))
    zeta_b = np.broadcast_to(zeta.T[:, :, None], (RET_HEADS, RET_CHUNK, RET_DK))
    xi_b = np.broadcast_to(xi.T[:, :, None], (RET_HEADS, RET_CHUNK, RET_DV))
    gch_b = np.broadcast_to(g_chunk[:, None, None], (RET_HEADS, 1, RET_DV))
    return tuple(jnp.asarray(a.astype(f32)) for a in (decay, zeta_b, xi_b, gch_b))


def _shift_rows(v, d, fill, row):
    if d % SUBLANES == 0:
        return jnp.concatenate([jnp.full((d, v.shape[1]), fill, v.dtype), v[:-d]], axis=0)
    return jnp.where(row >= d, pltpu.roll(v, d, axis=0), fill)


def _rglru_kernel(xc_ref, gg_ref, wai_ref, bai_ref, lam_ref, o_ref, a_ref, u_ref):
    s, width = xc_ref.shape
    for blk in range(width // LRU_BW):
        lanes = slice(blk * LRU_BW, (blk + 1) * LRU_BW)
        gates = slice(2 * blk * LRU_BW, 2 * (blk + 1) * LRU_BW)
        xc = xc_ref[:, lanes].astype(F32)
        y = jnp.dot(xc_ref[:, lanes], wai_ref[blk], preferred_element_type=F32) + bai_ref[:, gates]
        r = jax.nn.sigmoid(y[:, :LRU_BW])
        gi = jax.nn.sigmoid(y[:, LRU_BW:])
        neg_lam = -lam_ref[:, lanes]
        softplus = jnp.maximum(neg_lam, 0.0) + jnp.log1p(jnp.exp(-jnp.abs(neg_lam)))
        log_a = (-LRU_C) * r * softplus
        a = jnp.exp(log_a)
        a_ref[:, lanes] = a
        u_ref[:, lanes] = jnp.sqrt(1.0 - a * a) * (gi * xc)

    c = LRU_SCAN_CHUNK
    crow = lax.broadcasted_iota(jnp.int32, (c, width), 0)

    def chunk(n, h_prev):
        rows = pl.ds(pl.multiple_of(n * c, c), c)
        av, bv = a_ref[rows, :], u_ref[rows, :]
        d = 1
        while d < c:
            a_sh = _shift_rows(av, d, 1.0, crow)
            b_sh = _shift_rows(bv, d, 0.0, crow)
            bv = av * b_sh + bv
            av = av * a_sh
            d *= 2
        h = bv + av * h_prev
        o_ref[rows, :] = (h * gg_ref[rows, :].astype(F32)).astype(o_ref.dtype)
        return h[c - 1:c, :]

    lax.fori_loop(0, s // c, chunk, jnp.zeros((1, width), F32))


def rglru(proj, w_ai, b_ai, lam, batch):
    t = proj.shape[0]
    bps = LRU_BLOCKS_PER_STEP
    width = bps * LRU_BW
    x_off = 4 * RET_W // width
    g_off = x_off + LRU_W // width
    return pl.pallas_call(
        _rglru_kernel,
        grid=(batch, LRU_BLOCKS // bps),
        in_specs=[
            pl.BlockSpec((SEQ, width), lambda b, g: (b, x_off + g)),
            pl.BlockSpec((SEQ, width), lambda b, g: (b, g_off + g)),
            pl.BlockSpec((bps, LRU_BW, 2 * LRU_BW), lambda b, g: (g, 0, 0)),
            pl.BlockSpec((1, 2 * width), lambda b, g: (0, g)),
            pl.BlockSpec((1, width), lambda b, g: (0, g)),
        ],
        out_specs=pl.BlockSpec((SEQ, width), lambda b, g: (b, g)),
        out_shape=jax.ShapeDtypeStruct((t, LRU_W), BF16),
        scratch_shapes=[pltpu.VMEM((SEQ, width), F32), pltpu.VMEM((SEQ, width), F32)],
        compiler_params=_params("arbitrary", "arbitrary"),
        name="rglru",
    )(proj, proj, w_ai, b_ai, lam.reshape(1, LRU_W))


def _proj_res_kernel(*refs, splits, sub):
    part_refs = refs[:len(splits)]
    w_ref, x_ref, g_ref, o_ref, hn_ref, acc_ref = refs[len(splits):]

    def mm(r):
        rows = slice(r * sub, (r + 1) * sub)
        acc, off = None, 0
        for a_ref, k in zip(part_refs, splits):
            part = jnp.dot(a_ref[rows, :], w_ref[off:off + k, :], preferred_element_type=F32)
            acc = part if acc is None else acc + part
            off += k
        acc_ref[r % 2] = acc

    def epilogue(r):
        rows = slice(r * sub, (r + 1) * sub)
        y = x_ref[rows, :] + acc_ref[r % 2]
        o_ref[rows, :] = y
        hn_ref[rows, :] = (y * _rms_scale(y) * g_ref[...]).astype(BF16)

    n_sub = x_ref.shape[0] // sub
    mm(0)
    for r in range(n_sub):
        if r + 1 < n_sub:
            mm(r + 1)
        epilogue(r)


def proj_residual(parts, w, x, g, *, tm, sub=EPILOGUE_SUB_ROWS):
    t, n = x.shape
    splits = tuple(p.shape[1] for p in parts)
    row = lambda i: (i, 0)
    return pl.pallas_call(
        functools.partial(_proj_res_kernel, splits=splits, sub=sub),
        grid=(t // tm,),
        in_specs=[pl.BlockSpec((tm, k), row) for k in splits] + [
            pl.BlockSpec(w.shape, lambda i: (0, 0)),
            pl.BlockSpec((tm, n), row),
            pl.BlockSpec((1, n), lambda i: (0, 0)),
        ],
        out_specs=[pl.BlockSpec((tm, n), row), pl.BlockSpec((tm, n), row)],
        out_shape=[jax.ShapeDtypeStruct((t, n), F32), jax.ShapeDtypeStruct((t, n), BF16)],
        scratch_shapes=[pltpu.VMEM((2, sub, n), F32)],
        compiler_params=_params("arbitrary"),
        name="proj_residual",
    )(*parts, w, x, g.reshape(1, n))


def _ffn_up_kernel(hn_ref, wg_ref, wv_ref, cwg_ref, cwv_ref, cbg_ref, cbv_ref, o_ref,
                   wb_ref, u_ref, carry_ref, *, tiles_per_seq, sub):
    i = pl.program_id(1)
    tm = hn_ref.shape[0]
    tn = wg_ref.shape[1]

    @pl.when(i == 0)
    def _():
        wb_ref[:, :tn] = wg_ref[...].astype(BF16)
        wb_ref[:, tn:] = wv_ref[...].astype(BF16)

    def up(r):
        u_ref[r % 2, SUBLANES:, :] = jnp.dot(hn_ref[r * sub:(r + 1) * sub, :], wb_ref[...],
                                             preferred_element_type=F32)

    def conv(slot, cols, cw_ref, cb_ref):
        out = u_ref[slot, SUBLANES:, cols] * cw_ref[FFN_CONV - 1:FFN_CONV, :] + cb_ref[...]
        for d in range(1, FFN_CONV):
            prev = u_ref[slot, SUBLANES - d:SUBLANES - d + sub, cols]
            out = out + prev * cw_ref[FFN_CONV - 1 - d:FFN_CONV - d, :]
        return out

    seq_start = (i % tiles_per_seq) == 0
    u_ref[0, :SUBLANES, :] = jnp.where(seq_start, 0.0, carry_ref[...])
    up(0)
    for r in range(tm // sub):
        slot = r % 2
        u_ref[1 - slot, :SUBLANES, :] = u_ref[slot, sub:, :]
        if (r + 1) * sub < tm:
            up(r + 1)
        gate = conv(slot, slice(0, tn), cwg_ref, cbg_ref)
        val = conv(slot, slice(tn, 2 * tn), cwv_ref, cbv_ref)
        o_ref[r * sub:(r + 1) * sub, :] = (_gelu_tanh(gate) * val).astype(o_ref.dtype)
    carry_ref[...] = u_ref[(tm // sub) % 2, :SUBLANES, :]


def ffn_up(hn, w_up, conv_w, conv_b, layer, side_w, side_layer, *, tm, tn, sub):
    t, d = hn.shape
    nj, ni = D_FF // tn, t // tm
    cb = conv_b.reshape(conv_b.shape[0], 1, 2 * D_FF)
    per_col = lambda rows, off: pl.BlockSpec((None, rows, tn), lambda j, i: (layer, 0, off + j))
    side = SideCast(side_w, side_layer, nj * ni, lambda j, i: j * ni + i)
    in_specs = [pl.BlockSpec((tm, d), lambda j, i: (i, 0)),
                per_col(d, 0), per_col(d, nj),
                per_col(FFN_CONV, 0), per_col(FFN_CONV, nj),
                per_col(1, 0), per_col(1, nj)]
    kern = functools.partial(_ffn_up_kernel, tiles_per_seq=SEQ // tm, sub=sub)
    return pl.pallas_call(
        _with_side_cast(kern, len(in_specs), 1),
        grid=(nj, ni),
        in_specs=in_specs + [side.in_spec],
        out_specs=[pl.BlockSpec((tm, tn), lambda j, i: (i, j)), side.out_spec],
        out_shape=[jax.ShapeDtypeStruct((t, D_FF), BF16), side.out_shape],
        scratch_shapes=[pltpu.VMEM((d, 2 * tn), BF16),
                        pltpu.VMEM((2, SUBLANES + sub, 2 * tn), F32),
                        pltpu.VMEM((SUBLANES, 2 * tn), F32)],
        compiler_params=_params("arbitrary", "arbitrary"),
        name="ffn_up",
    )(hn, w_up, w_up, conv_w, conv_w, cb, cb, side.w)


def _mm_res_kernel(a_ref, w_ref, x_ref, o_ref):
    o_ref[...] = x_ref[...] + jnp.dot(a_ref[...], w_ref[...], preferred_element_type=F32)


def matmul_residual(a, w, x, *, tm, tn, side_w=None, side_layer=None):
    t, k = a.shape
    n = w.shape[1]
    nj, ni = n // tn, t // tm
    in_specs = [pl.BlockSpec((tm, k), lambda j, i: (i, 0)),
                pl.BlockSpec((k, tn), lambda j, i: (0, j)),
                pl.BlockSpec((tm, tn), lambda j, i: (i, j))]
    out_specs = [pl.BlockSpec((tm, tn), lambda j, i: (i, j))]
    out_shape = [jax.ShapeDtypeStruct((t, n), F32)]
    operands = [a, w, x]
    kern = _mm_res_kernel
    if side_w is not None:
        side = SideCast(side_w, side_layer, nj * ni, lambda j, i: j * ni + i)
        kern = _with_side_cast(kern, len(in_specs), 1)
        in_specs.append(side.in_spec)
        out_specs.append(side.out_spec)
        out_shape.append(side.out_shape)
        operands.append(side.w)
    outs = pl.pallas_call(
        kern,
        grid=(nj, ni),
        in_specs=in_specs,
        out_specs=out_specs,
        out_shape=out_shape,
        compiler_params=_params("arbitrary", "arbitrary"),
        name="matmul_residual",
    )(*operands)
    return outs if side_w is not None else outs[0]


def _moba_kernel(relb_ref, bkt_ref, q_ref, k_ref, v_ref, o_ref,
                 bias_ref, qt_ref, ks_ref, vt_ref, s_ref, p_ref):
    hg, b = pl.program_id(0), pl.program_id(1)
    nb = SEQ // MOBA_BLOCK
    blk = MOBA_BLOCK
    dh = MOBA_DH
    heads = range(qt_ref.shape[0])

    @pl.when((hg == 0) & (b == 0))
    def _():
        key_blk = lax.broadcasted_iota(jnp.int32, (SEQ, dh), 0) // blk
        lane = lax.broadcasted_iota(jnp.int32, (SEQ, dh), 1)
        onehot = (lane < 2 * nb) & ((lane % nb) == key_blk)
        for hh in heads:
            ks_ref[hh, :, dh:] = jnp.where(onehot, 1.0, 0.0).astype(BF16)
            qt_ref[hh, dh:, :] = jnp.zeros((dh, SEQ), BF16)
            vt_ref[hh, dh:, :] = jnp.ones((vt_ref.shape[1] - dh, SEQ), BF16)

    @pl.when(b == 0)
    def _():
        for hh in heads:
            for m in range(2):
                bk = bkt_ref[m]
                strip = jnp.full(bk.shape, NEG_INF, F32)
                for t in range(REL_BUCKETS):
                    strip = jnp.where(bk == t, relb_ref[t, hg * len(heads) + hh] * LOG2E, strip)
                tiled = jnp.concatenate([strip] * (blk // SUBLANES), axis=0)
                toeplitz = pltpu.roll(tiled, 0, axis=1, stride=1, stride_axis=0)
                bias_ref[hh, m] = toeplitz[:, :blk]

    blk_id = lax.broadcasted_iota(jnp.int32, (nb, SEQ), 0)
    q_blk = lax.broadcasted_iota(jnp.int32, (nb, SEQ), 1) // blk
    averager = jnp.where(blk_id == q_blk, 1.0 / blk, 0.0).astype(BF16)
    valid = blk_id < q_blk

    def prepare(hh):
        lanes = slice(hh * dh, (hh + 1) * dh)
        far_bias = relb_ref[REL_BUCKETS - 1, hg * len(heads) + hh] * LOG2E
        qt_ref[hh, :dh, :] = q_ref[:, lanes].astype(F32).T.astype(BF16)
        ks_ref[hh, :, :dh] = k_ref[:, lanes]
        vt_ref[hh, :dh, :] = v_ref[:, lanes].astype(F32).T.astype(BF16)

        kmean = jnp.dot(averager, k_ref[:, lanes], preferred_element_type=F32)
        gate = jnp.dot(kmean.astype(BF16), qt_ref[hh, :dh, :], preferred_element_type=F32)
        gm = jnp.where(valid, gate, NEG_INF)
        cnt = jnp.zeros(gate.shape, F32)
        for s in range(1, nb):
            gs = pltpu.roll(gm, s, axis=0)
            tie = jnp.where(((blk_id - s) % nb) < blk_id, 1.0, 0.0)
            cnt = cnt + jnp.where(gs > gm, 1.0, jnp.where(gs == gm, tie, 0.0))
        picked = jnp.where(blk_id < q_blk - 1, far_bias, 0.0)
        row = jnp.where(valid, jnp.where(cnt < MOBA_TOPK, picked, MASK_VALUE), 0.0)
        row_hi = row.astype(BF16).astype(F32)
        row_lo = jnp.where(row > 0.5 * MASK_VALUE, row - row_hi, 0.0)
        qt_ref[hh, dh:dh + 2 * nb, :] = jnp.concatenate([row_hi, row_lo], axis=0).astype(BF16)

    def scores(hh, pos, i):
        for jb in range(i + 1):
            rows = slice(jb * blk, (jb + 1) * blk)
            s_ref[hh, pos % 2, rows, :] = jnp.dot(
                ks_ref[hh, rows, :], qt_ref[hh, :, i * blk:(i + 1) * blk],
                preferred_element_type=F32)

    def softmax(hh, pos, i):
        slot = pos % 2
        maxes = []
        for jb in range(i + 1):
            rows = slice(jb * blk, (jb + 1) * blk)
            s_t = s_ref[hh, slot, rows, :]
            if jb >= i - 1:
                s_t = s_t + bias_ref[hh, i - jb]
                s_ref[hh, slot, rows, :] = s_t
            maxes.append(jnp.max(s_t, axis=0, keepdims=True))
        m = functools.reduce(jnp.maximum, maxes)
        for jb in range(i + 1):
            rows = slice(jb * blk, (jb + 1) * blk)
            p_ref[hh, slot, rows, :] = jnp.exp2(s_ref[hh, slot, rows, :] - m).astype(BF16)

    def attend(hh, pos, i):
        acc = jnp.dot(vt_ref[hh, :, :(i + 1) * blk], p_ref[hh, pos % 2, :(i + 1) * blk, :],
                      preferred_element_type=F32)
        out = acc[:dh, :] / acc[dh:dh + 1, :]
        o_ref[i * blk:(i + 1) * blk, hh * dh:(hh + 1) * dh] = out.T.astype(o_ref.dtype)

    for hh in heads:
        prepare(hh)

    order = list(range(nb - 1, -1, -1))
    for hh in heads:
        scores(hh, 0, order[0])
    for pos, i in enumerate(order):
        for hh in heads:
            if pos + 1 < nb:
                scores(hh, pos + 1, order[pos + 1])
            softmax(hh, pos, i)
            if pos > 0:
                attend(hh, pos - 1, order[pos - 1])
    for hh in heads:
        attend(hh, nb - 1, order[-1])


def moba_bucket_tables():
    f32 = np.float32

    def bucket(dist):
        n = np.maximum(dist, 0)
        max_exact = REL_BUCKETS // 2
        nf = np.maximum(n, 1).astype(f32)
        large = max_exact + (np.log(nf / f32(max_exact)) / f32(math.log(REL_MAX_DIST / max_exact))
                             * f32(REL_BUCKETS - max_exact)).astype(np.int32)
        large = np.minimum(large, REL_BUCKETS - 1)
        return np.where(n < max_exact, n, large)

    lane = np.arange(2 * MOBA_BLOCK)
    own = np.where(lane < MOBA_BLOCK, bucket(lane), -1)
    adj = bucket((lane + MOBA_BLOCK) % (2 * MOBA_BLOCK))
    strips = np.stack([own, adj]).astype(np.int32)
    return jnp.asarray(np.broadcast_to(strips[:, None, :], (2, SUBLANES, 2 * MOBA_BLOCK)))


def moba(proj, rel_bias, batch, side_w, side_layer):
    t = proj.shape[0]
    hps = MOBA_HEADS_PER_STEP
    groups = MOBA_HEADS // hps
    width = hps * MOBA_DH
    col = lambda off: pl.BlockSpec((SEQ, width), lambda hg, b: (b, off + hg))
    side = SideCast(side_w, side_layer, groups * batch, lambda hg, b: hg * batch + b)
    in_specs = [pl.BlockSpec(memory_space=pltpu.SMEM),
                pl.BlockSpec((2, SUBLANES, 2 * MOBA_BLOCK), lambda hg, b: (0, 0, 0)),
                col(0), col(groups), col(2 * groups)]
    return pl.pallas_call(
        _with_side_cast(_moba_kernel, len(in_specs), 1),
        grid=(groups, batch),
        in_specs=in_specs + [side.in_spec],
        out_specs=[pl.BlockSpec((SEQ, width), lambda hg, b: (b, hg)), side.out_spec],
        out_shape=[jax.ShapeDtypeStruct((t, MOBA_HEADS * MOBA_DH), BF16), side.out_shape],
        scratch_shapes=[pltpu.VMEM((hps, 2, MOBA_BLOCK, MOBA_BLOCK), F32),
                        pltpu.VMEM((hps, 2 * MOBA_DH, SEQ), BF16),
                        pltpu.VMEM((hps, SEQ, 2 * MOBA_DH), BF16),
                        pltpu.VMEM((hps, MOBA_DH + 2 * SUBLANES, SEQ), BF16),
                        pltpu.VMEM((hps, 2, SEQ, MOBA_BLOCK), F32),
                        pltpu.VMEM((hps, 2, SEQ, MOBA_BLOCK), BF16)],
        compiler_params=_params("arbitrary", "arbitrary"),
        name="moba",
    )(rel_bias, moba_bucket_tables(), proj, proj, proj, side.w)


def moba_head_gain(q_norm, k_norm):
    qg = jnp.tile(q_norm * (MOBA_DH ** -0.5 * LOG2E), MOBA_HEADS)
    return jnp.concatenate([qg, jnp.tile(k_norm, MOBA_HEADS)])


def kernel(x, norm_mix, norm_ffn, ev_w_in, ret_norm, lru_conv_w, lru_conv_b, lru_w_a, lru_b_a,
           lru_w_i, lru_b_i, lru_lambda, ev_w_out, od_w_in, q_norm, k_norm, od_w_out, rel_bias,
           ffn_w_up, ffn_conv_w, ffn_conv_b, ffn_w_down):
    batch, seq, d = x.shape
    assert (seq, d) == (SEQ, D_MODEL)
    xf = x.reshape(batch * seq, d)


    tm_in = IN_PROJ_ROWS
    cos, sin = rotary_tables()
    pos_rows = lambda i, j: (i % (SEQ // tm_in), 0)
    fixed = lambda i, j: (0, 0)
    proj = norm_matmul(
        xf, norm_mix[0], ev_w_in, 0, tm=tm_in, tn=RET_W, out_dtype=BF16,
        col_epilogues=[("rotary", 1.0), ("rotary", RET_DK ** -0.5), None, ("silu",), ("conv",),
                       ("gelu",)],
        extras={"cos": (cos, (tm_in, RET_DK // 2), pos_rows),
                "sin": (sin, (tm_in, RET_DK // 2), pos_rows),
                "cw": (lru_conv_w[0], (LRU_CONV, LRU_W), fixed),
                "cb": (lru_conv_b[0].reshape(1, LRU_W), (1, LRU_W), fixed)})
    ret, w_out_b = retention(proj, retention_tables(), ret_norm[0], batch, ev_w_out, 0)
    w_ai = jnp.concatenate([lru_w_a[0], lru_w_i[0]], axis=-1).astype(BF16)
    b_ai = jnp.concatenate([lru_b_a[0].reshape(LRU_BLOCKS, LRU_BW),
                            lru_b_i[0].reshape(LRU_BLOCKS, LRU_BW)], axis=-1).reshape(1, 2 * LRU_W)
    lru = rglru(proj, w_ai, b_ai, lru_lambda[0], batch)
    xf, hn = proj_residual([ret, lru], w_out_b, xf, norm_ffn[0], tm=OUT_PROJ_ROWS)
    act, w_down_b = ffn_up(hn, ffn_w_up, ffn_conv_w, ffn_conv_b, 0, ffn_w_down, 0,
                           tm=FFN_UP_TILE[0], tn=FFN_UP_TILE[1], sub=EPILOGUE_SUB_ROWS)
    xf, w_in_b = matmul_residual(act, w_down_b, xf, tm=FFN_DOWN_TILE[0], tn=FFN_DOWN_TILE[1],
                                 side_w=od_w_in, side_layer=0)

    head_norm = ("head_norm", MOBA_DH)
    proj = norm_matmul(
        xf, norm_mix[1], w_in_b, None, tm=IN_PROJ_ROWS, tn=MOBA_W, out_dtype=BF16,
        col_epilogues=[head_norm, head_norm, None],
        extras={"hg": (moba_head_gain(q_norm[0], k_norm[0]).reshape(1, 2 * MOBA_W),
                       (1, MOBA_W), lambda i, j: (0, jnp.minimum(j, 1)))})
    att, w_out_b = moba(proj, rel_bias, batch, od_w_out, 0)
    xf, hn = proj_residual([att], w_out_b, xf, norm_ffn[1], tm=OUT_PROJ_ROWS)
    act, w_down_b = ffn_up(hn, ffn_w_up, ffn_conv_w, ffn_conv_b, 1, ffn_w_down, 1,
                           tm=FFN_UP_TILE[0], tn=FFN_UP_TILE[1], sub=EPILOGUE_SUB_ROWS)
    xf = matmul_residual(act, w_down_b, xf, tm=FFN_DOWN_TILE[0], tn=FFN_DOWN_TILE[1])
    return xf.reshape(batch, seq, d)
```

```python
import functools
import math

import jax
import jax.numpy as jnp
import numpy as np
from jax import lax
from jax.experimental import pallas as pl
from jax.experimental.pallas import tpu as pltpu

D_MODEL = 2048
SEQ = 2048
RET_HEADS = 4
RET_DK = 256
RET_DV = 256
RET_W = RET_HEADS * RET_DK
RET_CHUNK = 128
RET_HEADS_PER_STEP = 1
ROPE_BASE = 10000.0
LRU_W = 1024
LRU_BLOCKS = 8
LRU_BW = 128
LRU_CONV = 4
LRU_C = 8.0
LRU_SCAN_CHUNK = 64
LRU_BLOCKS_PER_STEP = 2
MOBA_HEADS = 16
MOBA_DH = 128
MOBA_W = MOBA_HEADS * MOBA_DH
MOBA_HEADS_PER_STEP = 4
MOBA_BLOCK = 256
MOBA_TOPK = 3
REL_BUCKETS = 32
REL_MAX_DIST = 128
D_FF = 5632
FFN_CONV = 3
EPS = 1e-6

V7X_VMEM_BYTES = 64 * 1024 * 1024
VMEM_LIMIT_BYTES = V7X_VMEM_BYTES - 8 * 1024 * 1024
SUBLANES = 8

IN_PROJ_ROWS = 1024
OUT_PROJ_ROWS = 512
FFN_UP_TILE = (2048, 512)
FFN_DOWN_TILE = (512, 1024)
EPILOGUE_SUB_ROWS = 256

BF16 = jnp.bfloat16
F32 = jnp.float32
NEG_INF = float("-inf")
MASK_VALUE = -1e30
LOG2E = math.log2(math.e)


def _params(*sem):
    return pltpu.CompilerParams(dimension_semantics=sem, vmem_limit_bytes=VMEM_LIMIT_BYTES)


def _gelu_tanh(x):
    c = math.sqrt(2.0 / math.pi)
    return 0.5 * x * (1.0 + jnp.tanh(c * (x + 0.044715 * (x * x * x))))


def _rms_scale(x):
    return lax.rsqrt(jnp.mean(x * x, axis=-1, keepdims=True) + EPS)


class SideCast:
    def __init__(self, w, layer, n_steps, step_of):
        k, n = w.shape[1:]
        rows = k // n_steps
        assert rows * n_steps == k and rows % (2 * SUBLANES) == 0
        self.w = w
        self.in_spec = pl.BlockSpec((None, rows, n), lambda *ids: (layer, step_of(*ids), 0))
        self.out_spec = pl.BlockSpec((rows, n), lambda *ids: (step_of(*ids), 0))
        self.out_shape = jax.ShapeDtypeStruct((k, n), BF16)


def _with_side_cast(body, n_in, n_out):
    def wrapped(*refs):
        side_in, side_out = refs[n_in], refs[n_in + 1 + n_out]
        side_out[...] = side_in[...].astype(BF16)
        body(*refs[:n_in], *refs[n_in + 1:n_in + 1 + n_out], *refs[n_in + 2 + n_out:])
    return wrapped


def _norm_mm_kernel(*refs, extra_names, col_epilogues, sub, tiles_per_seq):
    n_extra = len(extra_names)
    x_ref, g_ref, w_ref = refs[:3]
    ex = dict(zip(extra_names, refs[3:3 + n_extra]))
    o_ref, hn_ref = refs[3 + n_extra:5 + n_extra]
    scratch = list(refs[5 + n_extra:])
    w_is_bf16 = w_ref.dtype == BF16
    wb_ref = w_ref if w_is_bf16 else scratch.pop(0)
    acc_ref = scratch.pop(0) if any(col_epilogues) else None
    halo_ref = scratch.pop(0) if ("conv",) in col_epilogues else None
    i, j = pl.program_id(0), pl.program_id(1)
    tm, tn = o_ref.shape
    n_sub = tm // sub

    def apply_epilogue(kind, a, r):
        rows = slice(r * sub, (r + 1) * sub)
        if kind[0] == "head_norm":
            heads = [a[:, s:s + kind[1]] for s in range(0, tn, kind[1])]
            return jnp.concatenate([h * _rms_scale(h) for h in heads], axis=-1) * ex["hg"][...]
        if kind[0] == "rotary":
            cos, sin = ex["cos"][rows, :], ex["sin"][rows, :]
            if kind[1] != 1.0:
                cos, sin = cos * kind[1], sin * kind[1]
            half = RET_DK // 2
            out = []
            for s in range(0, tn, RET_DK):
                x1, x2 = a[:, s:s + half], a[:, s + half:s + RET_DK]
                out += [x1 * cos - x2 * sin, x1 * sin + x2 * cos]
            return jnp.concatenate(out, axis=-1)
        if kind[0] == "silu":
            return a * jax.nn.sigmoid(a)
        if kind[0] == "gelu":
            return _gelu_tanh(a)
        if kind[0] == "conv":
            taps = ex["cw"].shape[0]
            fresh = (i % tiles_per_seq == 0) if r == 0 else False
            halo = jnp.where(fresh, 0.0, halo_ref[...])
            halo_ref[...] = a[sub - SUBLANES:, :]
            ext = jnp.concatenate([halo, a], axis=0)
            out = a * ex["cw"][taps - 1:taps, :] + ex["cb"][...]
            for d in range(1, taps):
                prev = pltpu.roll(ext, d, axis=0)[SUBLANES:, :]
                out = out + prev * ex["cw"][taps - 1 - d:taps - d, :]
            return out
        raise ValueError(kind)

    def plain():
        o_ref[...] = jnp.dot(hn_ref[...], w_ref[...].astype(BF16),
                             preferred_element_type=F32).astype(o_ref.dtype)

    def staged(norm_rows, kind):
        if not w_is_bf16:
            wb_ref[...] = w_ref[...].astype(BF16)

        def norm(r):
            x = x_ref[r * sub:(r + 1) * sub, :]
            hn_ref[r * sub:(r + 1) * sub, :] = (x * _rms_scale(x) * g_ref[...]).astype(BF16)

        def mm(r):
            acc = jnp.dot(hn_ref[r * sub:(r + 1) * sub, :], wb_ref[...],
                          preferred_element_type=F32)
            if kind:
                acc_ref[r % 2] = acc
            else:
                o_ref[r * sub:(r + 1) * sub, :] = acc.astype(o_ref.dtype)

        def epilogue(r):
            o_ref[r * sub:(r + 1) * sub, :] = apply_epilogue(
                kind, acc_ref[r % 2], r).astype(o_ref.dtype)

        if norm_rows:
            norm(0)
        for r in range(n_sub):
            if norm_rows and r + 1 < n_sub:
                norm(r + 1)
            mm(r)
            if kind and r > 0:
                epilogue(r - 1)
        if kind:
            epilogue(n_sub - 1)

    for col, kind in enumerate(col_epilogues):
        if col == 0 or kind:
            pl.when(j == col)(functools.partial(staged, col == 0, kind))
        else:
            pl.when(j == col)(plain)


def norm_matmul(x, g, w, layer, *, tm, tn, out_dtype, col_epilogues, extras=None,
                sub=EPILOGUE_SUB_ROWS):
    t, d = x.shape
    n = w.shape[-1]
    assert len(col_epilogues) == n // tn
    if layer is None:
        assert w.dtype == BF16
        w_spec = pl.BlockSpec((d, tn), lambda i, j: (0, j))
    else:
        w_spec = pl.BlockSpec((None, d, tn), lambda i, j: (layer, 0, j))
    extras = extras or {}
    scratch = [pltpu.VMEM((tm, d), BF16)]
    if layer is not None:
        scratch.append(pltpu.VMEM((d, tn), BF16))
    if any(col_epilogues):
        scratch.append(pltpu.VMEM((2, sub, tn), F32))
    if ("conv",) in col_epilogues:
        scratch.append(pltpu.VMEM((SUBLANES, tn), F32))
    return pl.pallas_call(
        functools.partial(_norm_mm_kernel, extra_names=tuple(extras),
                          col_epilogues=tuple(col_epilogues), sub=sub, tiles_per_seq=SEQ // tm),
        grid=(t // tm, n // tn),
        in_specs=[
            pl.BlockSpec((tm, d), lambda i, j: (i, 0)),
            pl.BlockSpec((1, d), lambda i, j: (0, 0)),
            w_spec,
        ] + [pl.BlockSpec(block, index_map) for _, block, index_map in extras.values()],
        out_specs=pl.BlockSpec((tm, tn), lambda i, j: (i, j)),
        out_shape=jax.ShapeDtypeStruct((t, n), out_dtype),
        scratch_shapes=scratch,
        compiler_params=_params("arbitrary", "arbitrary"),
        name="norm_matmul",
    )(x, g.reshape(1, d), w, *[arr for arr, _, _ in extras.values()])


def _retention_kernel(q_ref, k_ref, v_ref, g_ref, decay_ref, zeta_ref, xi_ref,
                      gch_ref, rn_ref, o_ref, kv_ref, st_ref):
    n_chunks = q_ref.shape[0] // RET_CHUNK
    heads = range(kv_ref.shape[0])
    chunk_rows = lambda n: slice(n * RET_CHUNK, (n + 1) * RET_CHUNK)
    head_cols = lambda hh: slice(hh * RET_DK, (hh + 1) * RET_DK)

    for n in range(n_chunks):
        rows = chunk_rows(n)
        for hh in heads:
            cols = head_cols(hh)
            kz = (k_ref[rows, cols].astype(F32) * zeta_ref[hh]).astype(BF16)
            kv_ref[hh, n] = lax.dot_general(kz, v_ref[rows, cols], (((0,), (0,)), ((), ())),
                                            preferred_element_type=F32)

    states = [jnp.zeros((RET_DK, RET_DV), F32) for _ in heads]
    for n in range(n_chunks):
        for hh in heads:
            st_ref[hh, n] = states[hh].astype(BF16)
            states[hh] = gch_ref[hh] * states[hh] + kv_ref[hh, n]

    for n in range(n_chunks):
        rows = chunk_rows(n)
        for hh in heads:
            cols = head_cols(hh)
            qb, vb = q_ref[rows, cols], v_ref[rows, cols]
            scores = lax.dot_general(qb, k_ref[rows, cols], (((1,), (1,)), ((), ())),
                                     preferred_element_type=F32) * decay_ref[hh]
            inner = jnp.dot(scores.astype(BF16), vb, preferred_element_type=F32)
            cross = jnp.dot(qb, st_ref[hh, n], preferred_element_type=F32) * xi_ref[hh]
            out = inner + cross
            out = out * _rms_scale(out) * rn_ref[hh]
            o_ref[rows, cols] = (out * g_ref[rows, cols].astype(F32)).astype(o_ref.dtype)


def retention(proj, tables, ret_norm, batch, side_w, side_layer):
    decay, zeta, xi, gch = tables
    t = proj.shape[0]
    hps = RET_HEADS_PER_STEP
    groups = RET_HEADS // hps
    col = lambda off: pl.BlockSpec((SEQ, hps * RET_DK), lambda b, hg: (b, off + hg))
    per_head = lambda shape: pl.BlockSpec((hps,) + shape, lambda b, hg: (hg, 0, 0))
    side = SideCast(side_w, side_layer, batch * groups, lambda b, hg: b * groups + hg)
    in_specs = [col(0), col(groups), col(2 * groups), col(3 * groups),
                per_head((RET_CHUNK, RET_CHUNK)), per_head((RET_CHUNK, RET_DK)),
                per_head((RET_CHUNK, RET_DV)), per_head((1, RET_DV)), per_head((1, RET_DV))]
    return pl.pallas_call(
        _with_side_cast(_retention_kernel, len(in_specs), 1),
        grid=(batch, groups),
        in_specs=in_specs + [side.in_spec],
        out_specs=[pl.BlockSpec((SEQ, hps * RET_DV), lambda b, hg: (b, hg)), side.out_spec],
        out_shape=[jax.ShapeDtypeStruct((t, RET_HEADS * RET_DV), BF16), side.out_shape],
        scratch_shapes=[pltpu.VMEM((hps, SEQ // RET_CHUNK, RET_DK, RET_DV), F32),
                        pltpu.VMEM((hps, SEQ // RET_CHUNK, RET_DK, RET_DV), BF16)],
        compiler_params=_params("arbitrary", "arbitrary"),
        name="retention",
    )(proj, proj, proj, proj, decay, zeta, xi, gch,
      ret_norm.reshape(RET_HEADS, 1, RET_DV), side.w)


def rotary_tables():
    f32 = np.float32
    half = RET_DK // 2
    freqs = f32(ROPE_BASE) ** (-np.arange(half, dtype=f32) / f32(half))
    ang = np.arange(SEQ).astype(f32)[:, None] * freqs[None, :]
    return jnp.asarray(np.cos(ang)), jnp.asarray(np.sin(ang))


def retention_tables():
    f32 = np.float32
    log_g = np.log1p(-np.exp2(f32(-5.0) - np.arange(RET_HEADS, dtype=f32)))
    i = np.arange(RET_CHUNK)
    diff = i[:, None] - i[None, :]
    decay = np.where(diff >= 0, np.exp(log_g[:, None, None] * np.maximum(diff, 0).astype(f32)),
                     f32(0.0))
    zeta = np.exp(log_g[None, :] * (RET_CHUNK - 1 - i).astype(f32)[:, None])
    xi = np.exp(log_g[None, :] * (i + 1).astype(f32)[:, None])
    g_chunk = np.exp(log_g * f32(RET_CHUNK))
    zeta_b = np.broadcast_to(zeta.T[:, :, None], (RET_HEADS, RET_CHUNK, RET_DK))
    xi_b = np.broadcast_to(xi.T[:, :, None], (RET_HEADS, RET_CHUNK, RET_DV))
    gch_b = np.broadcast_to(g_chunk[:, None, None], (RET_HEADS, 1, RET_DV))
    return tuple(jnp.asarray(a.astype(f32)) for a in (decay, zeta_b, xi_b, gch_b))


def _shift_rows(v, d, fill, row):
    if d % SUBLANES == 0:
        return jnp.concatenate([jnp.full((d, v.shape[1]), fill, v.dtype), v[:-d]], axis=0)
    return jnp.where(row >= d, pltpu.roll(v, d, axis=0), fill)


def _rglru_kernel(xc_ref, gg_ref, wai_ref, bai_ref, lam_ref, o_ref, a_ref, u_ref):
    s, width = xc_ref.shape
    for blk in range(width // LRU_BW):
        lanes = slice(blk * LRU_BW, (blk + 1) * LRU_BW)
        gates = slice(2 * blk * LRU_BW, 2 * (blk + 1) * LRU_BW)
        xc = xc_ref[:, lanes].astype(F32)
        y = jnp.dot(xc_ref[:, lanes], wai_ref[blk], preferred_element_type=F32) + bai_ref[:, gates]
        r = jax.nn.sigmoid(y[:, :LRU_BW])
        gi = jax.nn.sigmoid(y[:, LRU_BW:])
        neg_lam = -lam_ref[:, lanes]
        softplus = jnp.maximum(neg_lam, 0.0) + jnp.log1p(jnp.exp(-jnp.abs(neg_lam)))
        log_a = (-LRU_C) * r * softplus
        a = jnp.exp(log_a)
        a_ref[:, lanes] = a
        u_ref[:, lanes] = jnp.sqrt(1.0 - a * a) * (gi * xc)

    c = LRU_SCAN_CHUNK
    crow = lax.broadcasted_iota(jnp.int32, (c, width), 0)

    def chunk(n, h_prev):
        rows = pl.ds(pl.multiple_of(n * c, c), c)
        av, bv = a_ref[rows, :], u_ref[rows, :]
        d = 1
        while d < c:
            a_sh = _shift_rows(av, d, 1.0, crow)
            b_sh = _shift_rows(bv, d, 0.0, crow)
            bv = av * b_sh + bv
            av = av * a_sh
            d *= 2
        h = bv + av * h_prev
        o_ref[rows, :] = (h * gg_ref[rows, :].astype(F32)).astype(o_ref.dtype)
        return h[c - 1:c, :]

    lax.fori_loop(0, s // c, chunk, jnp.zeros((1, width), F32))


def rglru(proj, w_ai, b_ai, lam, batch):
    t = proj.shape[0]
    bps = LRU_BLOCKS_PER_STEP
    width = bps * LRU_BW
    x_off = 4 * RET_W // width
    g_off = x_off + LRU_W // width
    return pl.pallas_call(
        _rglru_kernel,
        grid=(batch, LRU_BLOCKS // bps),
        in_specs=[
            pl.BlockSpec((SEQ, width), lambda b, g: (b, x_off + g)),
            pl.BlockSpec((SEQ, width), lambda b, g: (b, g_off + g)),
            pl.BlockSpec((bps, LRU_BW, 2 * LRU_BW), lambda b, g: (g, 0, 0)),
            pl.BlockSpec((1, 2 * width), lambda b, g: (0, g)),
            pl.BlockSpec((1, width), lambda b, g: (0, g)),
        ],
        out_specs=pl.BlockSpec((SEQ, width), lambda b, g: (b, g)),
        out_shape=jax.ShapeDtypeStruct((t, LRU_W), BF16),
        scratch_shapes=[pltpu.VMEM((SEQ, width), F32), pltpu.VMEM((SEQ, width), F32)],
        compiler_params=_params("arbitrary", "arbitrary"),
        name="rglru",
    )(proj, proj, w_ai, b_ai, lam.reshape(1, LRU_W))


def _proj_res_kernel(*refs, splits, sub):
    part_refs = refs[:len(splits)]
    w_ref, x_ref, g_ref, o_ref, hn_ref, acc_ref = refs[len(splits):]

    def mm(r):
        rows = slice(r * sub, (r + 1) * sub)
        acc, off = None, 0
        for a_ref, k in zip(part_refs, splits):
            part = jnp.dot(a_ref[rows, :], w_ref[off:off + k, :], preferred_element_type=F32)
            acc = part if acc is None else acc + part
            off += k
        acc_ref[r % 2] = acc

    def epilogue(r):
        rows = slice(r * sub, (r + 1) * sub)
        y = x_ref[rows, :] + acc_ref[r % 2]
        o_ref[rows, :] = y
        hn_ref[rows, :] = (y * _rms_scale(y) * g_ref[...]).astype(BF16)

    n_sub = x_ref.shape[0] // sub
    mm(0)
    for r in range(n_sub):
        if r + 1 < n_sub:
            mm(r + 1)
        epilogue(r)


def proj_residual(parts, w, x, g, *, tm, sub=EPILOGUE_SUB_ROWS):
    t, n = x.shape
    splits = tuple(p.shape[1] for p in parts)
    row = lambda i: (i, 0)
    return pl.pallas_call(
        functools.partial(_proj_res_kernel, splits=splits, sub=sub),
        grid=(t // tm,),
        in_specs=[pl.BlockSpec((tm, k), row) for k in splits] + [
            pl.BlockSpec(w.shape, lambda i: (0, 0)),
            pl.BlockSpec((tm, n), row),
            pl.BlockSpec((1, n), lambda i: (0, 0)),
        ],
        out_specs=[pl.BlockSpec((tm, n), row), pl.BlockSpec((tm, n), row)],
        out_shape=[jax.ShapeDtypeStruct((t, n), F32), jax.ShapeDtypeStruct((t, n), BF16)],
        scratch_shapes=[pltpu.VMEM((2, sub, n), F32)],
        compiler_params=_params("arbitrary"),
        name="proj_residual",
    )(*parts, w, x, g.reshape(1, n))


def _ffn_up_kernel(hn_ref, wg_ref, wv_ref, cwg_ref, cwv_ref, cbg_ref, cbv_ref, o_ref,
                   wb_ref, u_ref, carry_ref, *, tiles_per_seq, sub):
    i = pl.program_id(1)
    tm = hn_ref.shape[0]
    tn = wg_ref.shape[1]

    @pl.when(i == 0)
    def _():
        wb_ref[:, :tn] = wg_ref[...].astype(BF16)
        wb_ref[:, tn:] = wv_ref[...].astype(BF16)

    def up(r):
        u_ref[r % 2, SUBLANES:, :] = jnp.dot(hn_ref[r * sub:(r + 1) * sub, :], wb_ref[...],
                                             preferred_element_type=F32)

    def conv(slot, cols, cw_ref, cb_ref):
        out = u_ref[slot, SUBLANES:, cols] * cw_ref[FFN_CONV - 1:FFN_CONV, :] + cb_ref[...]
        for d in range(1, FFN_CONV):
            prev = u_ref[slot, SUBLANES - d:SUBLANES - d + sub, cols]
            out = out + prev * cw_ref[FFN_CONV - 1 - d:FFN_CONV - d, :]
        return out

    seq_start = (i % tiles_per_seq) == 0
    u_ref[0, :SUBLANES, :] = jnp.where(seq_start, 0.0, carry_ref[...])
    up(0)
    for r in range(tm // sub):
        slot = r % 2
        u_ref[1 - slot, :SUBLANES, :] = u_ref[slot, sub:, :]
        if (r + 1) * sub < tm:
            up(r + 1)
        gate = conv(slot, slice(0, tn), cwg_ref, cbg_ref)
        val = conv(slot, slice(tn, 2 * tn), cwv_ref, cbv_ref)
        o_ref[r * sub:(r + 1) * sub, :] = (_gelu_tanh(gate) * val).astype(o_ref.dtype)
    carry_ref[...] = u_ref[(tm // sub) % 2, :SUBLANES, :]


def ffn_up(hn, w_up, conv_w, conv_b, layer, side_w, side_layer, *, tm, tn, sub):
    t, d = hn.shape
    nj, ni = D_FF // tn, t // tm
    cb = conv_b.reshape(conv_b.shape[0], 1, 2 * D_FF)
    per_col = lambda rows, off: pl.BlockSpec((None, rows, tn), lambda j, i: (layer, 0, off + j))
    side = SideCast(side_w, side_layer, nj * ni, lambda j, i: j * ni + i)
    in_specs = [pl.BlockSpec((tm, d), lambda j, i: (i, 0)),
                per_col(d, 0), per_col(d, nj),
                per_col(FFN_CONV, 0), per_col(FFN_CONV, nj),
                per_col(1, 0), per_col(1, nj)]
    kern = functools.partial(_ffn_up_kernel, tiles_per_seq=SEQ // tm, sub=sub)
    return pl.pallas_call(
        _with_side_cast(kern, len(in_specs), 1),
        grid=(nj, ni),
        in_specs=in_specs + [side.in_spec],
        out_specs=[pl.BlockSpec((tm, tn), lambda j, i: (i, j)), side.out_spec],
        out_shape=[jax.ShapeDtypeStruct((t, D_FF), BF16), side.out_shape],
        scratch_shapes=[pltpu.VMEM((d, 2 * tn), BF16),
                        pltpu.VMEM((2, SUBLANES + sub, 2 * tn), F32),
                        pltpu.VMEM((SUBLANES, 2 * tn), F32)],
        compiler_params=_params("arbitrary", "arbitrary"),
        name="ffn_up",
    )(hn, w_up, w_up, conv_w, conv_w, cb, cb, side.w)


def _mm_res_kernel(a_ref, w_ref, x_ref, o_ref):
    o_ref[...] = x_ref[...] + jnp.dot(a_ref[...], w_ref[...], preferred_element_type=F32)


def matmul_residual(a, w, x, *, tm, tn, side_w=None, side_layer=None):
    t, k = a.shape
    n = w.shape[1]
    nj, ni = n // tn, t // tm
    in_specs = [pl.BlockSpec((tm, k), lambda j, i: (i, 0)),
                pl.BlockSpec((k, tn), lambda j, i: (0, j)),
                pl.BlockSpec((tm, tn), lambda j, i: (i, j))]
    out_specs = [pl.BlockSpec((tm, tn), lambda j, i: (i, j))]
    out_shape = [jax.ShapeDtypeStruct((t, n), F32)]
    operands = [a, w, x]
    kern = _mm_res_kernel
    if side_w is not None:
        side = SideCast(side_w, side_layer, nj * ni, lambda j, i: j * ni + i)
        kern = _with_side_cast(kern, len(in_specs), 1)
        in_specs.append(side.in_spec)
        out_specs.append(side.out_spec)
        out_shape.append(side.out_shape)
        operands.append(side.w)
    outs = pl.pallas_call(
        kern,
        grid=(nj, ni),
        in_specs=in_specs,
        out_specs=out_specs,
        out_shape=out_shape,
        compiler_params=_params("arbitrary", "arbitrary"),
        name="matmul_residual",
    )(*operands)
    return outs if side_w is not None else outs[0]


def _moba_kernel(relb_ref, bkt_ref, q_ref, k_ref, v_ref, o_ref,
                 bias_ref, qt_ref, ks_ref, vt_ref, s_ref, p_ref):
    hg, b = pl.program_id(0), pl.program_id(1)
    nb = SEQ // MOBA_BLOCK
    blk = MOBA_BLOCK
    dh = MOBA_DH
    heads = range(qt_ref.shape[0])

    @pl.when((hg == 0) & (b == 0))
    def _():
        key_blk = lax.broadcasted_iota(jnp.int32, (SEQ, dh), 0) // blk
        lane = lax.broadcasted_iota(jnp.int32, (SEQ, dh), 1)
        onehot = (lane < 2 * nb) & ((lane % nb) == key_blk)
        for hh in heads:
            ks_ref[hh, :, dh:] = jnp.where(onehot, 1.0, 0.0).astype(BF16)
            qt_ref[hh, dh:, :] = jnp.zeros((dh, SEQ), BF16)
            vt_ref[hh, dh:, :] = jnp.ones((vt_ref.shape[1] - dh, SEQ), BF16)

    @pl.when(b == 0)
    def _():
        for hh in heads:
            for m in range(2):
                bk = bkt_ref[m]
                strip = jnp.full(bk.shape, NEG_INF, F32)
                for t in range(REL_BUCKETS):
                    strip = jnp.where(bk == t, relb_ref[t, hg * len(heads) + hh] * LOG2E, strip)
                tiled = jnp.concatenate([strip] * (blk // SUBLANES), axis=0)
                toeplitz = pltpu.roll(tiled, 0, axis=1, stride=1, stride_axis=0)
                bias_ref[hh, m] = toeplitz[:, :blk]

    blk_id = lax.broadcasted_iota(jnp.int32, (nb, SEQ), 0)
    q_blk = lax.broadcasted_iota(jnp.int32, (nb, SEQ), 1) // blk
    averager = jnp.where(blk_id == q_blk, 1.0 / blk, 0.0).astype(BF16)
    valid = blk_id < q_blk

    def prepare(hh):
        lanes = slice(hh * dh, (hh + 1) * dh)
        far_bias = relb_ref[REL_BUCKETS - 1, hg * len(heads) + hh] * LOG2E
        qt_ref[hh, :dh, :] = q_ref[:, lanes].astype(F32).T.astype(BF16)
        ks_ref[hh, :, :dh] = k_ref[:, lanes]
        vt_ref[hh, :dh, :] = v_ref[:, lanes].astype(F32).T.astype(BF16)

        kmean = jnp.dot(averager, k_ref[:, lanes], preferred_element_type=F32)
        gate = jnp.dot(kmean.astype(BF16), qt_ref[hh, :dh, :], preferred_element_type=F32)
        gm = jnp.where(valid, gate, NEG_INF)
        cnt = jnp.zeros(gate.shape, F32)
        for s in range(1, nb):
            gs = pltpu.roll(gm, s, axis=0)
            tie = jnp.where(((blk_id - s) % nb) < blk_id, 1.0, 0.0)
            cnt = cnt + jnp.where(gs > gm, 1.0, jnp.where(gs == gm, tie, 0.0))
        picked = jnp.where(blk_id < q_blk - 1, far_bias, 0.0)
        row = jnp.where(valid, jnp.where(cnt < MOBA_TOPK, picked, MASK_VALUE), 0.0)
        row_hi = row.astype(BF16).astype(F32)
        row_lo = jnp.where(row > 0.5 * MASK_VALUE, row - row_hi, 0.0)
        qt_ref[hh, dh:dh + 2 * nb, :] = jnp.concatenate([row_hi, row_lo], axis=0).astype(BF16)

    def scores(hh, pos, i):
        for jb in range(i + 1):
            rows = slice(jb * blk, (jb + 1) * blk)
            s_ref[hh, pos % 2, rows, :] = jnp.dot(
                ks_ref[hh, rows, :], qt_ref[hh, :, i * blk:(i + 1) * blk],
                preferred_element_type=F32)

    def softmax(hh, pos, i):
        slot = pos % 2
        maxes = []
        for jb in range(i + 1):
            rows = slice(jb * blk, (jb + 1) * blk)
            s_t = s_ref[hh, slot, rows, :]
            if jb >= i - 1:
                s_t = s_t + bias_ref[hh, i - jb]
                s_ref[hh, slot, rows, :] = s_t
            maxes.append(jnp.max(s_t, axis=0, keepdims=True))
        m = functools.reduce(jnp.maximum, maxes)
        for jb in range(i + 1):
            rows = slice(jb * blk, (jb + 1) * blk)
            p_ref[hh, slot, rows, :] = jnp.exp2(s_ref[hh, slot, rows, :] - m).astype(BF16)

    def attend(hh, pos, i):
        acc = jnp.dot(vt_ref[hh, :, :(i + 1) * blk], p_ref[hh, pos % 2, :(i + 1) * blk, :],
                      preferred_element_type=F32)
        out = acc[:dh, :] / acc[dh:dh + 1, :]
        o_ref[i * blk:(i + 1) * blk, hh * dh:(hh + 1) * dh] = out.T.astype(o_ref.dtype)

    for hh in heads:
        prepare(hh)

    order = list(range(nb - 1, -1, -1))
    for hh in heads:
        scores(hh, 0, order[0])
    for pos, i in enumerate(order):
        for hh in heads:
            if pos + 1 < nb:
                scores(hh, pos + 1, order[pos + 1])
            softmax(hh, pos, i)
            if pos > 0:
                attend(hh, pos - 1, order[pos - 1])
    for hh in heads:
        attend(hh, nb - 1, order[-1])


def moba_bucket_tables():
    f32 = np.float32

    def bucket(dist):
        n = np.maximum(dist, 0)
        max_exact = REL_BUCKETS // 2
        nf = np.maximum(n, 1).astype(f32)
        large = max_exact + (np.log(nf / f32(max_exact)) / f32(math.log(REL_MAX_DIST / max_exact))
                             * f32(REL_BUCKETS - max_exact)).astype(np.int32)
        large = np.minimum(large, REL_BUCKETS - 1)
        return np.where(n < max_exact, n, large)

    lane = np.arange(2 * MOBA_BLOCK)
    own = np.where(lane < MOBA_BLOCK, bucket(lane), -1)
    adj = bucket((lane + MOBA_BLOCK) % (2 * MOBA_BLOCK))
    strips = np.stack([own, adj]).astype(np.int32)
    return jnp.asarray(np.broadcast_to(strips[:, None, :], (2, SUBLANES, 2 * MOBA_BLOCK)))


def moba(proj, rel_bias, batch, side_w, side_layer):
    t = proj.shape[0]
    hps = MOBA_HEADS_PER_STEP
    groups = MOBA_HEADS // hps
    width = hps * MOBA_DH
    col = lambda off: pl.BlockSpec((SEQ, width), lambda hg, b: (b, off + hg))
    side = SideCast(side_w, side_layer, groups * batch, lambda hg, b: hg * batch + b)
    in_specs = [pl.BlockSpec(memory_space=pltpu.SMEM),
                pl.BlockSpec((2, SUBLANES, 2 * MOBA_BLOCK), lambda hg, b: (0, 0, 0)),
                col(0), col(groups), col(2 * groups)]
    return pl.pallas_call(
        _with_side_cast(_moba_kernel, len(in_specs), 1),
        grid=(groups, batch),
        in_specs=in_specs + [side.in_spec],
        out_specs=[pl.BlockSpec((SEQ, width), lambda hg, b: (b, hg)), side.out_spec],
        out_shape=[jax.ShapeDtypeStruct((t, MOBA_HEADS * MOBA_DH), BF16), side.out_shape],
        scratch_shapes=[pltpu.VMEM((hps, 2, MOBA_BLOCK, MOBA_BLOCK), F32),
                        pltpu.VMEM((hps, 2 * MOBA_DH, SEQ), BF16),
                        pltpu.VMEM((hps, SEQ, 2 * MOBA_DH), BF16),
                        pltpu.VMEM((hps, MOBA_DH + 2 * SUBLANES, SEQ), BF16),
                        pltpu.VMEM((hps, 2, SEQ, MOBA_BLOCK), F32),
                        pltpu.VMEM((hps, 2, SEQ, MOBA_BLOCK), BF16)],
        compiler_params=_params("arbitrary", "arbitrary"),
        name="moba",
    )(rel_bias, moba_bucket_tables(), proj, proj, proj, side.w)


def moba_head_gain(q_norm, k_norm):
    qg = jnp.tile(q_norm * (MOBA_DH ** -0.5 * LOG2E), MOBA_HEADS)
    return jnp.concatenate([qg, jnp.tile(k_norm, MOBA_HEADS)])


def kernel(x, norm_mix, norm_ffn, ev_w_in, ret_norm, lru_conv_w, lru_conv_b, lru_w_a, lru_b_a,
           lru_w_i, lru_b_i, lru_lambda, ev_w_out, od_w_in, q_norm, k_norm, od_w_out, rel_bias,
           ffn_w_up, ffn_conv_w, ffn_conv_b, ffn_w_down):
    batch, seq, d = x.shape
    assert (seq, d) == (SEQ, D_MODEL)
    xf = x.reshape(batch * seq, d)


    tm_in = IN_PROJ_ROWS
    cos, sin = rotary_tables()
    pos_rows = lambda i, j: (i % (SEQ // tm_in), 0)
    fixed = lambda i, j: (0, 0)
    proj = norm_matmul(
        xf, norm_mix[0], ev_w_in, 0, tm=tm_in, tn=RET_W, out_dtype=BF16,
        col_epilogues=[("rotary", 1.0), ("rotary", RET_DK ** -0.5), None, ("silu",), ("conv",),
                       ("gelu",)],
        extras={"cos": (cos, (tm_in, RET_DK // 2), pos_rows),
                "sin": (sin, (tm_in, RET_DK // 2), pos_rows),
                "cw": (lru_conv_w[0], (LRU_CONV, LRU_W), fixed),
                "cb": (lru_conv_b[0].reshape(1, LRU_W), (1, LRU_W), fixed)})
    ret, w_out_b = retention(proj, retention_tables(), ret_norm[0], batch, ev_w_out, 0)
    w_ai = jnp.concatenate([lru_w_a[0], lru_w_i[0]], axis=-1).astype(BF16)
    b_ai = jnp.concatenate([lru_b_a[0].reshape(LRU_BLOCKS, LRU_BW),
                            lru_b_i[0].reshape(LRU_BLOCKS, LRU_BW)], axis=-1).reshape(1, 2 * LRU_W)
    lru = rglru(proj, w_ai, b_ai, lru_lambda[0], batch)
    xf, hn = proj_residual([ret, lru], w_out_b, xf, norm_ffn[0], tm=OUT_PROJ_ROWS)
    act, w_down_b = ffn_up(hn, ffn_w_up, ffn_conv_w, ffn_conv_b, 0, ffn_w_down, 0,
                           tm=FFN_UP_TILE[0], tn=FFN_UP_TILE[1], sub=EPILOGUE_SUB_ROWS)
    xf, w_in_b = matmul_residual(act, w_down_b, xf, tm=FFN_DOWN_TILE[0], tn=FFN_DOWN_TILE[1],
                                 side_w=od_w_in, side_layer=0)

    head_norm = ("head_norm", MOBA_DH)
    proj = norm_matmul(
        xf, norm_mix[1], w_in_b, None, tm=IN_PROJ_ROWS, tn=MOBA_W, out_dtype=BF16,
        col_epilogues=[head_norm, head_norm, None],
        extras={"hg": (moba_head_gain(q_norm[0], k_norm[0]).reshape(1, 2 * MOBA_W),
                       (1, MOBA_W), lambda i, j: (0, jnp.minimum(j, 1)))})
    att, w_out_b = moba(proj, rel_bias, batch, od_w_out, 0)
    xf, hn = proj_residual([att], w_out_b, xf, norm_ffn[1], tm=OUT_PROJ_ROWS)
    act, w_down_b = ffn_up(hn, ffn_w_up, ffn_conv_w, ffn_conv_b, 1, ffn_w_down, 1,
                           tm=FFN_UP_TILE[0], tn=FFN_UP_TILE[1], sub=EPILOGUE_SUB_ROWS)
    xf = matmul_residual(act, w_down_b, xf, tm=FFN_DOWN_TILE[0], tn=FFN_DOWN_TILE[1])
    return xf.reshape(batch, seq, d)
```

```python
import functools
import math

import jax
import jax.numpy as jnp
import numpy as np
from jax import lax
from jax.experimental import pallas as pl
from jax.experimental.pallas import tpu as pltpu

D_MODEL = 2048
SEQ = 2048
RET_HEADS = 4
RET_DK = 256
RET_DV = 256
RET_W = RET_HEADS * RET_DK
RET_CHUNK = 128
RET_HEADS_PER_STEP = 1
ROPE_BASE = 10000.0
LRU_W = 1024
LRU_BLOCKS = 8
LRU_BW = 128
LRU_CONV = 4
LRU_C = 8.0
LRU_SCAN_CHUNK = 64
LRU_BLOCKS_PER_STEP = 2
MOBA_HEADS = 16
MOBA_DH = 128
MOBA_W = MOBA_HEADS * MOBA_DH
MOBA_HEADS_PER_STEP = 4
MOBA_BLOCK = 256
MOBA_TOPK = 3
REL_BUCKETS = 32
REL_MAX_DIST = 128
D_FF = 5632
FFN_CONV = 3
EPS = 1e-6

V7X_VMEM_BYTES = 64 * 1024 * 1024
VMEM_LIMIT_BYTES = V7X_VMEM_BYTES - 8 * 1024 * 1024
SUBLANES = 8

IN_PROJ_ROWS = 1024
OUT_PROJ_ROWS = 512
FFN_UP_TILE = (2048, 512)
FFN_DOWN_TILE = (512, 1024)
EPILOGUE_SUB_ROWS = 256
FFN_UP_SUB_ROWS = 128

BF16 = jnp.bfloat16
F32 = jnp.float32
NEG_INF = float("-inf")
MASK_VALUE = -1e30
LOG2E = math.log2(math.e)


def _params(*sem):
    return pltpu.CompilerParams(dimension_semantics=sem, vmem_limit_bytes=VMEM_LIMIT_BYTES)


def _gelu_tanh(x):
    c = math.sqrt(2.0 / math.pi)
    return 0.5 * x * (1.0 + jnp.tanh(c * (x + 0.044715 * (x * x * x))))


def _rms_scale(x):
    return lax.rsqrt(jnp.mean(x * x, axis=-1, keepdims=True) + EPS)


class SideCast:
    def __init__(self, w, layer, n_steps, step_of):
        k, n = w.shape[1:]
        rows = k // n_steps
        assert rows * n_steps == k and rows % (2 * SUBLANES) == 0
        self.w = w
        self.in_spec = pl.BlockSpec((None, rows, n), lambda *ids: (layer, step_of(*ids), 0))
        self.out_spec = pl.BlockSpec((rows, n), lambda *ids: (step_of(*ids), 0))
        self.out_shape = jax.ShapeDtypeStruct((k, n), BF16)


def _with_side_cast(body, n_in, n_out):
    def wrapped(*refs):
        side_in, side_out = refs[n_in], refs[n_in + 1 + n_out]
        side_out[...] = side_in[...].astype(BF16)
        body(*refs[:n_in], *refs[n_in + 1:n_in + 1 + n_out], *refs[n_in + 2 + n_out:])
    return wrapped


def _norm_mm_kernel(*refs, extra_names, col_epilogues, sub, tiles_per_seq):
    n_extra = len(extra_names)
    x_ref, g_ref, w_ref = refs[:3]
    ex = dict(zip(extra_names, refs[3:3 + n_extra]))
    o_ref, hn_ref = refs[3 + n_extra:5 + n_extra]
    scratch = list(refs[5 + n_extra:])
    w_is_bf16 = w_ref.dtype == BF16
    wb_ref = w_ref if w_is_bf16 else scratch.pop(0)
    acc_ref = scratch.pop(0) if any(col_epilogues) else None
    halo_ref = scratch.pop(0) if ("conv",) in col_epilogues else None
    i, j = pl.program_id(0), pl.program_id(1)
    tm, tn = o_ref.shape
    n_sub = tm // sub

    def apply_epilogue(kind, a, r):
        rows = slice(r * sub, (r + 1) * sub)
        if kind[0] == "head_norm":
            heads = [a[:, s:s + kind[1]] for s in range(0, tn, kind[1])]
            return jnp.concatenate([h * _rms_scale(h) for h in heads], axis=-1) * ex["hg"][...]
        if kind[0] == "rotary":
            cos, sin = ex["cos"][rows, :], ex["sin"][rows, :]
            if kind[1] != 1.0:
                cos, sin = cos * kind[1], sin * kind[1]
            half = RET_DK // 2
            out = []
            for s in range(0, tn, RET_DK):
                x1, x2 = a[:, s:s + half], a[:, s + half:s + RET_DK]
                out += [x1 * cos - x2 * sin, x1 * sin + x2 * cos]
            return jnp.concatenate(out, axis=-1)
        if kind[0] == "silu":
            return a * jax.nn.sigmoid(a)
        if kind[0] == "gelu":
            return _gelu_tanh(a)
        if kind[0] == "conv":
            taps = ex["cw"].shape[0]
            fresh = (i % tiles_per_seq == 0) if r == 0 else False
            halo = jnp.where(fresh, 0.0, halo_ref[...])
            halo_ref[...] = a[sub - SUBLANES:, :]
            ext = jnp.concatenate([halo, a], axis=0)
            out = a * ex["cw"][taps - 1:taps, :] + ex["cb"][...]
            for d in range(1, taps):
                prev = pltpu.roll(ext, d, axis=0)[SUBLANES:, :]
                out = out + prev * ex["cw"][taps - 1 - d:taps - d, :]
            return out
        raise ValueError(kind)

    def plain():
        o_ref[...] = jnp.dot(hn_ref[...], w_ref[...].astype(BF16),
                             preferred_element_type=F32).astype(o_ref.dtype)

    def staged(norm_rows, kind):
        if not w_is_bf16:
            wb_ref[...] = w_ref[...].astype(BF16)

        def norm(r):
            x = x_ref[r * sub:(r + 1) * sub, :]
            hn_ref[r * sub:(r + 1) * sub, :] = (x * _rms_scale(x) * g_ref[...]).astype(BF16)

        def mm(r):
            acc = jnp.dot(hn_ref[r * sub:(r + 1) * sub, :], wb_ref[...],
                          preferred_element_type=F32)
            if kind:
                acc_ref[r % 2] = acc
            else:
                o_ref[r * sub:(r + 1) * sub, :] = acc.astype(o_ref.dtype)

        def epilogue(r):
            o_ref[r * sub:(r + 1) * sub, :] = apply_epilogue(
                kind, acc_ref[r % 2], r).astype(o_ref.dtype)

        if norm_rows:
            norm(0)
        for r in range(n_sub):
            if norm_rows and r + 1 < n_sub:
                norm(r + 1)
            mm(r)
            if kind and r > 0:
                epilogue(r - 1)
        if kind:
            epilogue(n_sub - 1)

    for col, kind in enumerate(col_epilogues):
        if col == 0 or kind:
            pl.when(j == col)(functools.partial(staged, col == 0, kind))
        else:
            pl.when(j == col)(plain)


def norm_matmul(x, g, w, layer, *, tm, tn, out_dtype, col_epilogues, extras=None,
                sub=EPILOGUE_SUB_ROWS):
    t, d = x.shape
    n = w.shape[-1]
    assert len(col_epilogues) == n // tn
    if layer is None:
        assert w.dtype == BF16
        w_spec = pl.BlockSpec((d, tn), lambda i, j: (0, j))
    else:
        w_spec = pl.BlockSpec((None, d, tn), lambda i, j: (layer, 0, j))
    extras = extras or {}
    scratch = [pltpu.VMEM((tm, d), BF16)]
    if layer is not None:
        scratch.append(pltpu.VMEM((d, tn), BF16))
    if any(col_epilogues):
        scratch.append(pltpu.VMEM((2, sub, tn), F32))
    if ("conv",) in col_epilogues:
        scratch.append(pltpu.VMEM((SUBLANES, tn), F32))
    return pl.pallas_call(
        functools.partial(_norm_mm_kernel, extra_names=tuple(extras),
                          col_epilogues=tuple(col_epilogues), sub=sub, tiles_per_seq=SEQ // tm),
        grid=(t // tm, n // tn),
        in_specs=[
            pl.BlockSpec((tm, d), lambda i, j: (i, 0)),
            pl.BlockSpec((1, d), lambda i, j: (0, 0)),
            w_spec,
        ] + [pl.BlockSpec(block, index_map) for _, block, index_map in extras.values()],
        out_specs=pl.BlockSpec((tm, tn), lambda i, j: (i, j)),
        out_shape=jax.ShapeDtypeStruct((t, n), out_dtype),
        scratch_shapes=scratch,
        compiler_params=_params("arbitrary", "arbitrary"),
        name="norm_matmul",
    )(x, g.reshape(1, d), w, *[arr for arr, _, _ in extras.values()])


def _retention_kernel(q_ref, k_ref, v_ref, g_ref, decay_ref, zeta_ref, xi_ref,
                      gch_ref, rn_ref, o_ref, kv_ref, st_ref):
    n_chunks = q_ref.shape[0] // RET_CHUNK
    heads = range(kv_ref.shape[0])
    chunk_rows = lambda n: slice(n * RET_CHUNK, (n + 1) * RET_CHUNK)
    head_cols = lambda hh: slice(hh * RET_DK, (hh + 1) * RET_DK)

    for n in range(n_chunks):
        rows = chunk_rows(n)
        for hh in heads:
            cols = head_cols(hh)
            kz = (k_ref[rows, cols].astype(F32) * zeta_ref[hh]).astype(BF16)
            kv_ref[hh, n] = lax.dot_general(kz, v_ref[rows, cols], (((0,), (0,)), ((), ())),
                                            preferred_element_type=F32)

    states = [jnp.zeros((RET_DK, RET_DV), F32) for _ in heads]
    for n in range(n_chunks):
        for hh in heads:
            st_ref[hh, n] = states[hh].astype(BF16)
            states[hh] = gch_ref[hh] * states[hh] + kv_ref[hh, n]

    for n in range(n_chunks):
        rows = chunk_rows(n)
        for hh in heads:
            cols = head_cols(hh)
            qb, vb = q_ref[rows, cols], v_ref[rows, cols]
            scores = lax.dot_general(qb, k_ref[rows, cols], (((1,), (1,)), ((), ())),
                                     preferred_element_type=F32) * decay_ref[hh]
            inner = jnp.dot(scores.astype(BF16), vb, preferred_element_type=F32)
            cross = jnp.dot(qb, st_ref[hh, n], preferred_element_type=F32) * xi_ref[hh]
            out = inner + cross
            out = out * _rms_scale(out) * rn_ref[hh]
            o_ref[rows, cols] = (out * g_ref[rows, cols].astype(F32)).astype(o_ref.dtype)


def retention(proj, tables, ret_norm, batch, side_w, side_layer):
    decay, zeta, xi, gch = tables
    t = proj.shape[0]
    hps = RET_HEADS_PER_STEP
    groups = RET_HEADS // hps
    col = lambda off: pl.BlockSpec((SEQ, hps * RET_DK), lambda b, hg: (b, off + hg))
    per_head = lambda shape: pl.BlockSpec((hps,) + shape, lambda b, hg: (hg, 0, 0))
    side = SideCast(side_w, side_layer, batch * groups, lambda b, hg: b * groups + hg)
    in_specs = [col(0), col(groups), col(2 * groups), col(3 * groups),
                per_head((RET_CHUNK, RET_CHUNK)), per_head((RET_CHUNK, RET_DK)),
                per_head((RET_CHUNK, RET_DV)), per_head((1, RET_DV)), per_head((1, RET_DV))]
    return pl.pallas_call(
        _with_side_cast(_retention_kernel, len(in_specs), 1),
        grid=(batch, groups),
        in_specs=in_specs + [side.in_spec],
        out_specs=[pl.BlockSpec((SEQ, hps * RET_DV), lambda b, hg: (b, hg)), side.out_spec],
        out_shape=[jax.ShapeDtypeStruct((t, RET_HEADS * RET_DV), BF16), side.out_shape],
        scratch_shapes=[pltpu.VMEM((hps, SEQ // RET_CHUNK, RET_DK, RET_DV), F32),
                        pltpu.VMEM((hps, SEQ // RET_CHUNK, RET_DK, RET_DV), BF16)],
        compiler_params=_params("arbitrary", "arbitrary"),
        name="retention",
    )(proj, proj, proj, proj, decay, zeta, xi, gch,
      ret_norm.reshape(RET_HEADS, 1, RET_DV), side.w)


def rotary_tables():
    f32 = np.float32
    half = RET_DK // 2
    freqs = f32(ROPE_BASE) ** (-np.arange(half, dtype=f32) / f32(half))
    ang = np.arange(SEQ).astype(f32)[:, None] * freqs[None, :]
    return jnp.asarray(np.cos(ang)), jnp.asarray(np.sin(ang))


def retention_tables():
    f32 = np.float32
    log_g = np.log1p(-np.exp2(f32(-5.0) - np.arange(RET_HEADS, dtype=f32)))
    i = np.arange(RET_CHUNK)
    diff = i[:, None] - i[None, :]
    decay = np.where(diff >= 0, np.exp(log_g[:, None, None] * np.maximum(diff, 0).astype(f32)),
                     f32(0.0))
    zeta = np.exp(log_g[None, :] * (RET_CHUNK - 1 - i).astype(f32)[:, None])
    xi = np.exp(log_g[None, :] * (i + 1).astype(f32)[:, None])
    g_chunk = np.exp(log_g * f32(RET_CHUNK))
    zeta_b = np.broadcast_to(zeta.T[:, :, None], (RET_HEADS, RET_CHUNK, RET_DK))
    xi_b = np.broadcast_to(xi.T[:, :, None], (RET_HEADS, RET_CHUNK, RET_DV))
    gch_b = np.broadcast_to(g_chunk[:, None, None], (RET_HEADS, 1, RET_DV))
    return tuple(jnp.asarray(a.astype(f32)) for a in (decay, zeta_b, xi_b, gch_b))


def _shift_rows(v, d, fill, row):
    if d % SUBLANES == 0:
        return jnp.concatenate([jnp.full((d, v.shape[1]), fill, v.dtype), v[:-d]], axis=0)
    return jnp.where(row >= d, pltpu.roll(v, d, axis=0), fill)


def _rglru_kernel(xc_ref, gg_ref, wai_ref, bai_ref, lam_ref, o_ref, a_ref, u_ref):
    s, width = xc_ref.shape
    for blk in range(width // LRU_BW):
        lanes = slice(blk * LRU_BW, (blk + 1) * LRU_BW)
        gates = slice(2 * blk * LRU_BW, 2 * (blk + 1) * LRU_BW)
        xc = xc_ref[:, lanes].astype(F32)
        y = jnp.dot(xc_ref[:, lanes], wai_ref[blk], preferred_element_type=F32) + bai_ref[:, gates]
        r = jax.nn.sigmoid(y[:, :LRU_BW])
        gi = jax.nn.sigmoid(y[:, LRU_BW:])
        neg_lam = -lam_ref[:, lanes]
        softplus = jnp.maximum(neg_lam, 0.0) + jnp.log1p(jnp.exp(-jnp.abs(neg_lam)))
        log_a = (-LRU_C) * r * softplus
        a = jnp.exp(log_a)
        a_ref[:, lanes] = a
        u_ref[:, lanes] = jnp.sqrt(1.0 - a * a) * (gi * xc)

    c = LRU_SCAN_CHUNK
    crow = lax.broadcasted_iota(jnp.int32, (c, width), 0)

    def chunk(n, h_prev):
        rows = pl.ds(pl.multiple_of(n * c, c), c)
        av, bv = a_ref[rows, :], u_ref[rows, :]
        d = 1
        while d < c:
            a_sh = _shift_rows(av, d, 1.0, crow)
            b_sh = _shift_rows(bv, d, 0.0, crow)
            bv = av * b_sh + bv
            av = av * a_sh
            d *= 2
        h = bv + av * h_prev
        o_ref[rows, :] = (h * gg_ref[rows, :].astype(F32)).astype(o_ref.dtype)
        return h[c - 1:c, :]

    lax.fori_loop(0, s // c, chunk, jnp.zeros((1, width), F32))


def rglru(proj, w_ai, b_ai, lam, batch):
    t = proj.shape[0]
    bps = LRU_BLOCKS_PER_STEP
    width = bps * LRU_BW
    x_off = 4 * RET_W // width
    g_off = x_off + LRU_W // width
    return pl.pallas_call(
        _rglru_kernel,
        grid=(batch, LRU_BLOCKS // bps),
        in_specs=[
            pl.BlockSpec((SEQ, width), lambda b, g: (b, x_off + g)),
            pl.BlockSpec((SEQ, width), lambda b, g: (b, g_off + g)),
            pl.BlockSpec((bps, LRU_BW, 2 * LRU_BW), lambda b, g: (g, 0, 0)),
            pl.BlockSpec((1, 2 * width), lambda b, g: (0, g)),
            pl.BlockSpec((1, width), lambda b, g: (0, g)),
        ],
        out_specs=pl.BlockSpec((SEQ, width), lambda b, g: (b, g)),
        out_shape=jax.ShapeDtypeStruct((t, LRU_W), BF16),
        scratch_shapes=[pltpu.VMEM((SEQ, width), F32), pltpu.VMEM((SEQ, width), F32)],
        compiler_params=_params("arbitrary", "arbitrary"),
        name="rglru",
    )(proj, proj, w_ai, b_ai, lam.reshape(1, LRU_W))


def _proj_res_kernel(*refs, splits, sub):
    part_refs = refs[:len(splits)]
    w_ref, x_ref, g_ref, o_ref, hn_ref, acc_ref = refs[len(splits):]

    def mm(r):
        rows = slice(r * sub, (r + 1) * sub)
        acc, off = None, 0
        for a_ref, k in zip(part_refs, splits):
            part = jnp.dot(a_ref[rows, :], w_ref[off:off + k, :], preferred_element_type=F32)
            acc = part if acc is None else acc + part
            off += k
        acc_ref[r % 2] = acc

    def epilogue(r):
        rows = slice(r * sub, (r + 1) * sub)
        y = x_ref[rows, :] + acc_ref[r % 2]
        o_ref[rows, :] = y
        hn_ref[rows, :] = (y * _rms_scale(y) * g_ref[...]).astype(BF16)

    n_sub = x_ref.shape[0] // sub
    mm(0)
    for r in range(n_sub):
        if r + 1 < n_sub:
            mm(r + 1)
        epilogue(r)


def proj_residual(parts, w, x, g, *, tm, sub=EPILOGUE_SUB_ROWS):
    t, n = x.shape
    splits = tuple(p.shape[1] for p in parts)
    row = lambda i: (i, 0)
    return pl.pallas_call(
        functools.partial(_proj_res_kernel, splits=splits, sub=sub),
        grid=(t // tm,),
        in_specs=[pl.BlockSpec((tm, k), row) for k in splits] + [
            pl.BlockSpec(w.shape, lambda i: (0, 0)),
            pl.BlockSpec((tm, n), row),
            pl.BlockSpec((1, n), lambda i: (0, 0)),
        ],
        out_specs=[pl.BlockSpec((tm, n), row), pl.BlockSpec((tm, n), row)],
        out_shape=[jax.ShapeDtypeStruct((t, n), F32), jax.ShapeDtypeStruct((t, n), BF16)],
        scratch_shapes=[pltpu.VMEM((2, sub, n), F32)],
        compiler_params=_params("arbitrary"),
        name="proj_residual",
    )(*parts, w, x, g.reshape(1, n))


def _ffn_up_kernel(hn_ref, wg_ref, wv_ref, cwg_ref, cwv_ref, cbg_ref, cbv_ref, o_ref,
                   wb_ref, u_ref, carry_ref, *, tiles_per_seq, sub):
    i = pl.program_id(1)
    tm = hn_ref.shape[0]
    tn = wg_ref.shape[1]

    @pl.when(i == 0)
    def _():
        wb_ref[:, :tn] = wg_ref[...].astype(BF16)
        wb_ref[:, tn:] = wv_ref[...].astype(BF16)

    def up(r):
        u_ref[r % 2, SUBLANES:, :] = jnp.dot(hn_ref[r * sub:(r + 1) * sub, :], wb_ref[...],
                                             preferred_element_type=F32)

    def conv(slot, cols, cw_ref, cb_ref):
        out = u_ref[slot, SUBLANES:, cols] * cw_ref[FFN_CONV - 1:FFN_CONV, :] + cb_ref[...]
        for d in range(1, FFN_CONV):
            prev = u_ref[slot, SUBLANES - d:SUBLANES - d + sub, cols]
            out = out + prev * cw_ref[FFN_CONV - 1 - d:FFN_CONV - d, :]
        return out

    seq_start = (i % tiles_per_seq) == 0
    u_ref[0, :SUBLANES, :] = jnp.where(seq_start, 0.0, carry_ref[...])
    up(0)
    for r in range(tm // sub):
        slot = r % 2
        u_ref[1 - slot, :SUBLANES, :] = u_ref[slot, sub:, :]
        if (r + 1) * sub < tm:
            up(r + 1)
        gate = conv(slot, slice(0, tn), cwg_ref, cbg_ref)
        val = conv(slot, slice(tn, 2 * tn), cwv_ref, cbv_ref)
        o_ref[r * sub:(r + 1) * sub, :] = (_gelu_tanh(gate) * val).astype(o_ref.dtype)
    carry_ref[...] = u_ref[(tm // sub) % 2, :SUBLANES, :]


def ffn_up(hn, w_up, conv_w, conv_b, layer, side_w, side_layer, *, tm, tn, sub):
    t, d = hn.shape
    nj, ni = D_FF // tn, t // tm
    cb = conv_b.reshape(conv_b.shape[0], 1, 2 * D_FF)
    per_col = lambda rows, off: pl.BlockSpec((None, rows, tn), lambda j, i: (layer, 0, off + j))
    side = SideCast(side_w, side_layer, nj * ni, lambda j, i: j * ni + i)
    in_specs = [pl.BlockSpec((tm, d), lambda j, i: (i, 0)),
                per_col(d, 0), per_col(d, nj),
                per_col(FFN_CONV, 0), per_col(FFN_CONV, nj),
                per_col(1, 0), per_col(1, nj)]
    kern = functools.partial(_ffn_up_kernel, tiles_per_seq=SEQ // tm, sub=sub)
    return pl.pallas_call(
        _with_side_cast(kern, len(in_specs), 1),
        grid=(nj, ni),
        in_specs=in_specs + [side.in_spec],
        out_specs=[pl.BlockSpec((tm, tn), lambda j, i: (i, j)), side.out_spec],
        out_shape=[jax.ShapeDtypeStruct((t, D_FF), BF16), side.out_shape],
        scratch_shapes=[pltpu.VMEM((d, 2 * tn), BF16),
                        pltpu.VMEM((2, SUBLANES + sub, 2 * tn), F32),
                        pltpu.VMEM((SUBLANES, 2 * tn), F32)],
        compiler_params=_params("arbitrary", "arbitrary"),
        name="ffn_up",
    )(hn, w_up, w_up, conv_w, conv_w, cb, cb, side.w)


def _mm_res_kernel(a_ref, w_ref, x_ref, o_ref):
    o_ref[...] = x_ref[...] + jnp.dot(a_ref[...], w_ref[...], preferred_element_type=F32)


def matmul_residual(a, w, x, *, tm, tn, side_w=None, side_layer=None):
    t, k = a.shape
    n = w.shape[1]
    nj, ni = n // tn, t // tm
    in_specs = [pl.BlockSpec((tm, k), lambda j, i: (i, 0)),
                pl.BlockSpec((k, tn), lambda j, i: (0, j)),
                pl.BlockSpec((tm, tn), lambda j, i: (i, j))]
    out_specs = [pl.BlockSpec((tm, tn), lambda j, i: (i, j))]
    out_shape = [jax.ShapeDtypeStruct((t, n), F32)]
    operands = [a, w, x]
    kern = _mm_res_kernel
    if side_w is not None:
        side = SideCast(side_w, side_layer, nj * ni, lambda j, i: j * ni + i)
        kern = _with_side_cast(kern, len(in_specs), 1)
        in_specs.append(side.in_spec)
        out_specs.append(side.out_spec)
        out_shape.append(side.out_shape)
        operands.append(side.w)
    outs = pl.pallas_call(
        kern,
        grid=(nj, ni),
        in_specs=in_specs,
        out_specs=out_specs,
        out_shape=out_shape,
        compiler_params=_params("arbitrary", "arbitrary"),
        name="matmul_residual",
    )(*operands)
    return outs if side_w is not None else outs[0]


def _moba_kernel(relb_ref, bkt_ref, q_ref, k_ref, v_ref, o_ref,
                 bias_ref, qt_ref, ks_ref, vt_ref, s_ref, p_ref):
    hg, b = pl.program_id(0), pl.program_id(1)
    nb = SEQ // MOBA_BLOCK
    blk = MOBA_BLOCK
    dh = MOBA_DH
    heads = range(qt_ref.shape[0])

    @pl.when((hg == 0) & (b == 0))
    def _():
        key_blk = lax.broadcasted_iota(jnp.int32, (SEQ, dh), 0) // blk
        lane = lax.broadcasted_iota(jnp.int32, (SEQ, dh), 1)
        onehot = (lane < 2 * nb) & ((lane % nb) == key_blk)
        for hh in heads:
            ks_ref[hh, :, dh:] = jnp.where(onehot, 1.0, 0.0).astype(BF16)
            qt_ref[hh, dh:, :] = jnp.zeros((dh, SEQ), BF16)
            vt_ref[hh, dh:, :] = jnp.ones((vt_ref.shape[1] - dh, SEQ), BF16)

    @pl.when(b == 0)
    def _():
        for hh in heads:
            for m in range(2):
                bk = bkt_ref[m]
                strip = jnp.full(bk.shape, NEG_INF, F32)
                for t in range(REL_BUCKETS):
                    strip = jnp.where(bk == t, relb_ref[t, hg * len(heads) + hh] * LOG2E, strip)
                tiled = jnp.concatenate([strip] * (blk // SUBLANES), axis=0)
                toeplitz = pltpu.roll(tiled, 0, axis=1, stride=1, stride_axis=0)
                bias_ref[hh, m] = toeplitz[:, :blk]

    blk_id = lax.broadcasted_iota(jnp.int32, (nb, SEQ), 0)
    q_blk = lax.broadcasted_iota(jnp.int32, (nb, SEQ), 1) // blk
    averager = jnp.where(blk_id == q_blk, 1.0 / blk, 0.0).astype(BF16)
    valid = blk_id < q_blk

    def prepare(hh):
        lanes = slice(hh * dh, (hh + 1) * dh)
        far_bias = relb_ref[REL_BUCKETS - 1, hg * len(heads) + hh] * LOG2E
        qt_ref[hh, :dh, :] = q_ref[:, lanes].astype(F32).T.astype(BF16)
        ks_ref[hh, :, :dh] = k_ref[:, lanes]
        vt_ref[hh, :dh, :] = v_ref[:, lanes].astype(F32).T.astype(BF16)

        kmean = jnp.dot(averager, k_ref[:, lanes], preferred_element_type=F32)
        gate = jnp.dot(kmean.astype(BF16), qt_ref[hh, :dh, :], preferred_element_type=F32)
        gm = jnp.where(valid, gate, NEG_INF)
        cnt = jnp.zeros(gate.shape, F32)
        for s in range(1, nb):
            gs = pltpu.roll(gm, s, axis=0)
            tie = jnp.where(((blk_id - s) % nb) < blk_id, 1.0, 0.0)
            cnt = cnt + jnp.where(gs > gm, 1.0, jnp.where(gs == gm, tie, 0.0))
        picked = jnp.where(blk_id < q_blk - 1, far_bias, 0.0)
        row = jnp.where(valid, jnp.where(cnt < MOBA_TOPK, picked, MASK_VALUE), 0.0)
        row_hi = row.astype(BF16).astype(F32)
        row_lo = jnp.where(row > 0.5 * MASK_VALUE, row - row_hi, 0.0)
        qt_ref[hh, dh:dh + 2 * nb, :] = jnp.concatenate([row_hi, row_lo], axis=0).astype(BF16)

    def scores(hh, pos, i):
        for jb in range(i + 1):
            rows = slice(jb * blk, (jb + 1) * blk)
            s_ref[hh, pos % 2, rows, :] = jnp.dot(
                ks_ref[hh, rows, :], qt_ref[hh, :, i * blk:(i + 1) * blk],
                preferred_element_type=F32)

    def softmax(hh, pos, i):
        slot = pos % 2
        maxes = []
        for jb in range(i + 1):
            rows = slice(jb * blk, (jb + 1) * blk)
            s_t = s_ref[hh, slot, rows, :]
            if jb >= i - 1:
                s_t = s_t + bias_ref[hh, i - jb]
                s_ref[hh, slot, rows, :] = s_t
            maxes.append(jnp.max(s_t, axis=0, keepdims=True))
        m = functools.reduce(jnp.maximum, maxes)
        for jb in range(i + 1):
            rows = slice(jb * blk, (jb + 1) * blk)
            p_ref[hh, slot, rows, :] = jnp.exp2(s_ref[hh, slot, rows, :] - m).astype(BF16)

    def attend(hh, pos, i):
        acc = jnp.dot(vt_ref[hh, :, :(i + 1) * blk], p_ref[hh, pos % 2, :(i + 1) * blk, :],
                      preferred_element_type=F32)
        out = acc[:dh, :] / acc[dh:dh + 1, :]
        o_ref[i * blk:(i + 1) * blk, hh * dh:(hh + 1) * dh] = out.T.astype(o_ref.dtype)

    for hh in heads:
        prepare(hh)

    order = list(range(nb - 1, -1, -1))
    for hh in heads:
        scores(hh, 0, order[0])
    for pos, i in enumerate(order):
        for hh in heads:
            if pos + 1 < nb:
                scores(hh, pos + 1, order[pos + 1])
            softmax(hh, pos, i)
            if pos > 0:
                attend(hh, pos - 1, order[pos - 1])
    for hh in heads:
        attend(hh, nb - 1, order[-1])


def moba_bucket_tables():
    f32 = np.float32

    def bucket(dist):
        n = np.maximum(dist, 0)
        max_exact = REL_BUCKETS // 2
        nf = np.maximum(n, 1).astype(f32)
        large = max_exact + (np.log(nf / f32(max_exact)) / f32(math.log(REL_MAX_DIST / max_exact))
                             * f32(REL_BUCKETS - max_exact)).astype(np.int32)
        large = np.minimum(large, REL_BUCKETS - 1)
        return np.where(n < max_exact, n, large)

    lane = np.arange(2 * MOBA_BLOCK)
    own = np.where(lane < MOBA_BLOCK, bucket(lane), -1)
    adj = bucket((lane + MOBA_BLOCK) % (2 * MOBA_BLOCK))
    strips = np.stack([own, adj]).astype(np.int32)
    return jnp.asarray(np.broadcast_to(strips[:, None, :], (2, SUBLANES, 2 * MOBA_BLOCK)))


def moba(proj, rel_bias, batch, side_w, side_layer):
    t = proj.shape[0]
    hps = MOBA_HEADS_PER_STEP
    groups = MOBA_HEADS // hps
    width = hps * MOBA_DH
    col = lambda off: pl.BlockSpec((SEQ, width), lambda hg, b: (b, off + hg))
    side = SideCast(side_w, side_layer, groups * batch, lambda hg, b: hg * batch + b)
    in_specs = [pl.BlockSpec(memory_space=pltpu.SMEM),
                pl.BlockSpec((2, SUBLANES, 2 * MOBA_BLOCK), lambda hg, b: (0, 0, 0)),
                col(0), col(groups), col(2 * groups)]
    return pl.pallas_call(
        _with_side_cast(_moba_kernel, len(in_specs), 1),
        grid=(groups, batch),
        in_specs=in_specs + [side.in_spec],
        out_specs=[pl.BlockSpec((SEQ, width), lambda hg, b: (b, hg)), side.out_spec],
        out_shape=[jax.ShapeDtypeStruct((t, MOBA_HEADS * MOBA_DH), BF16), side.out_shape],
        scratch_shapes=[pltpu.VMEM((hps, 2, MOBA_BLOCK, MOBA_BLOCK), F32),
                        pltpu.VMEM((hps, 2 * MOBA_DH, SEQ), BF16),
                        pltpu.VMEM((hps, SEQ, 2 * MOBA_DH), BF16),
                        pltpu.VMEM((hps, MOBA_DH + 2 * SUBLANES, SEQ), BF16),
                        pltpu.VMEM((hps, 2, SEQ, MOBA_BLOCK), F32),
                        pltpu.VMEM((hps, 2, SEQ, MOBA_BLOCK), BF16)],
        compiler_params=_params("arbitrary", "arbitrary"),
        name="moba",
    )(rel_bias, moba_bucket_tables(), proj, proj, proj, side.w)


def moba_head_gain(q_norm, k_norm):
    qg = jnp.tile(q_norm * (MOBA_DH ** -0.5 * LOG2E), MOBA_HEADS)
    return jnp.concatenate([qg, jnp.tile(k_norm, MOBA_HEADS)])


def kernel(x, norm_mix, norm_ffn, ev_w_in, ret_norm, lru_conv_w, lru_conv_b, lru_w_a, lru_b_a,
           lru_w_i, lru_b_i, lru_lambda, ev_w_out, od_w_in, q_norm, k_norm, od_w_out, rel_bias,
           ffn_w_up, ffn_conv_w, ffn_conv_b, ffn_w_down):
    batch, seq, d = x.shape
    assert (seq, d) == (SEQ, D_MODEL)
    xf = x.reshape(batch * seq, d)


    tm_in = IN_PROJ_ROWS
    cos, sin = rotary_tables()
    pos_rows = lambda i, j: (i % (SEQ // tm_in), 0)
    fixed = lambda i, j: (0, 0)
    proj = norm_matmul(
        xf, norm_mix[0], ev_w_in, 0, tm=tm_in, tn=RET_W, out_dtype=BF16,
        col_epilogues=[("rotary", 1.0), ("rotary", RET_DK ** -0.5), None, ("silu",), ("conv",),
                       ("gelu",)],
        extras={"cos": (cos, (tm_in, RET_DK // 2), pos_rows),
                "sin": (sin, (tm_in, RET_DK // 2), pos_rows),
                "cw": (lru_conv_w[0], (LRU_CONV, LRU_W), fixed),
                "cb": (lru_conv_b[0].reshape(1, LRU_W), (1, LRU_W), fixed)})
    ret, w_out_b = retention(proj, retention_tables(), ret_norm[0], batch, ev_w_out, 0)
    w_ai = jnp.concatenate([lru_w_a[0], lru_w_i[0]], axis=-1).astype(BF16)
    b_ai = jnp.concatenate([lru_b_a[0].reshape(LRU_BLOCKS, LRU_BW),
                            lru_b_i[0].reshape(LRU_BLOCKS, LRU_BW)], axis=-1).reshape(1, 2 * LRU_W)
    lru = rglru(proj, w_ai, b_ai, lru_lambda[0], batch)
    xf, hn = proj_residual([ret, lru], w_out_b, xf, norm_ffn[0], tm=OUT_PROJ_ROWS)
    act, w_down_b = ffn_up(hn, ffn_w_up, ffn_conv_w, ffn_conv_b, 0, ffn_w_down, 0,
                           tm=FFN_UP_TILE[0], tn=FFN_UP_TILE[1], sub=FFN_UP_SUB_ROWS)
    xf, w_in_b = matmul_residual(act, w_down_b, xf, tm=FFN_DOWN_TILE[0], tn=FFN_DOWN_TILE[1],
                                 side_w=od_w_in, side_layer=0)

    head_norm = ("head_norm", MOBA_DH)
    proj = norm_matmul(
        xf, norm_mix[1], w_in_b, None, tm=IN_PROJ_ROWS, tn=MOBA_W, out_dtype=BF16,
        col_epilogues=[head_norm, head_norm, None],
        extras={"hg": (moba_head_gain(q_norm[0], k_norm[0]).reshape(1, 2 * MOBA_W),
                       (1, MOBA_W), lambda i, j: (0, jnp.minimum(j, 1)))})
    att, w_out_b = moba(proj, rel_bias, batch, od_w_out, 0)
    xf, hn = proj_residual([att], w_out_b, xf, norm_ffn[1], tm=OUT_PROJ_ROWS)
    act, w_down_b = ffn_up(hn, ffn_w_up, ffn_conv_w, ffn_conv_b, 1, ffn_w_down, 1,
                           tm=FFN_UP_TILE[0], tn=FFN_UP_TILE[1], sub=FFN_UP_SUB_ROWS)
    xf = matmul_residual(act, w_down_b, xf, tm=FFN_DOWN_TILE[0], tn=FFN_DOWN_TILE[1])
    return xf.reshape(batch, seq, d)
```
